```python
import math
import jax, jax.numpy as jnp
from jax import lax
import numpy as np

D_MODEL = 1024
BATCH = 32
SEQ = 256
DEPTH = 1
DEC_BATCH = 2
DEC_SEQ = 1024
PAST_LEN = 256

GRID_W = 64
MIX_WIDTH = D_MODEL
ATTN_WIDTH = MIX_WIDTH // 2
N_HEADS = 4
V_DIM = ATTN_WIDTH // N_HEADS
QK_DIM = V_DIM // 2
QK_WIDTH = N_HEADS * 2 * QK_DIM
FOURIER_WIDTH = MIX_WIDTH - ATTN_WIDTH
N_FGROUPS = 4
F_GROUP_DIM = FOURIER_WIDTH // N_FGROUPS
IN_WIDTH = 2 * QK_WIDTH + ATTN_WIDTH + FOURIER_WIDTH
N_EXPERTS = 32
TOP_K = 4
D_FF = D_MODEL
SWIGLU_LIMIT = 7.0
SWIGLU_ALPHA = 1.702
ROPE_THETA = 10000.0
NORM_EPS = 1e-6
Q_BLOCK = 128

kernel_name = "hymba_diffattn_fnet_moe_prefix_diffusion_step"


def rmsnorm(x, g):
    xf = x.astype(jnp.float32)
    y = xf * lax.rsqrt(jnp.mean(xf * xf, axis=-1, keepdims=True) + NORM_EPS)
    return (y * g.astype(jnp.float32)).astype(x.dtype)


def modulation(cvec, w_mod, b_mod):
    m = jax.nn.silu(cvec) @ w_mod + b_mod
    return jnp.split(m[:, None, :], 6, axis=-1)


def lambda_init(layer):
    return 0.8 - 0.6 * math.exp(-0.3 * layer)


def axial_rope_tables(n):
    rows = n // GRID_W
    row = jnp.repeat(jnp.arange(rows), GRID_W)
    col = jnp.tile(jnp.arange(GRID_W), rows)
    pos = jnp.stack([row, col], axis=-1).astype(jnp.float32)
    quarter = QK_DIM // 4
    freqs = ROPE_THETA ** (-jnp.arange(quarter, dtype=jnp.float32) / quarter)
    ang = pos[:, :, None] * freqs
    return jnp.cos(ang), jnp.sin(ang)


def apply_rope(x, cos, sin):
    xf = x.astype(jnp.float32).reshape(x.shape[:-1] + (2, 2, QK_DIM // 4))
    x1, x2 = xf[..., 0, :], xf[..., 1, :]
    c = cos[None, :, None, None, :, :]
    s = sin[None, :, None, None, :, :]
    out = jnp.stack([x1 * c - x2 * s, x2 * c + x1 * s], axis=-2)
    return out.reshape(x.shape).astype(x.dtype)


def project(x, shift, scale, norm_g, w_in):
    h = rmsnorm(x, norm_g) * (1 + scale) + shift
    p = h @ w_in
    B, n = x.shape[:2]
    q, k, v, f = jnp.split(p, [QK_WIDTH, 2 * QK_WIDTH, 2 * QK_WIDTH + ATTN_WIDTH], axis=-1)
    q = q.reshape(B, n, N_HEADS, 2, QK_DIM)
    k = k.reshape(B, n, N_HEADS, 2, QK_DIM)
    v = v.reshape(B, n, N_HEADS, V_DIM)
    f = f.reshape(B, n, N_FGROUPS, F_GROUP_DIM)
    return q, k, v, f


def diff_lambda(lq1, lk1, lq2, lk2, lam_init):
    f32 = jnp.float32
    return (jnp.exp(jnp.sum(lq1.astype(f32) * lk1.astype(f32)))
            - jnp.exp(jnp.sum(lq2.astype(f32) * lk2.astype(f32))) + lam_init)


def diff_attention(q, k, v, lam, subln_g, lam_init):
    B, nq = q.shape[:2]
    nblk = nq // Q_BLOCK
    qb = jnp.moveaxis(q.reshape(B, nblk, Q_BLOCK, N_HEADS, 2, QK_DIM), 1, 0)
    kf = k.astype(jnp.float32)
    vf = v.astype(jnp.float32)
    scale = QK_DIM ** -0.5

    def block(qblk):
        s = jnp.einsum('bqhmd,bhkmd->bhmqk', qblk.astype(jnp.float32), kf) * scale
        p = jax.nn.softmax(s, axis=-1)
        a = p[:, :, 0] - lam * p[:, :, 1]
        return jnp.einsum('bhqk,bhkd->bqhd', a, vf)

    o = lax.map(block, qb)
    o = jnp.moveaxis(o, 0, 1).reshape(B, nq, N_HEADS, V_DIM)
    o = rmsnorm(o, subln_g) * (1.0 - lam_init)
    return o.reshape(B, nq, ATTN_WIDTH).astype(q.dtype)


def fourier_mix(f):
    B, n = f.shape[:2]
    y = jnp.fft.fftn(f.astype(jnp.float32), axes=(1, 3), norm="ortho").real
    return y.reshape(B, n, FOURIER_WIDTH).astype(f.dtype)


def moe(h, router_w, router_b, w_gate_up, b_gate_up, w_down, b_down):
    B, n, D = h.shape
    x2 = h.reshape(B * n, D)
    logits = (x2 @ router_w + router_b).astype(jnp.float32)
    top_vals, top_idx = lax.top_k(logits, TOP_K)
    weights = jax.nn.softmax(top_vals, axis=-1)
    gates = jnp.sum(jax.nn.one_hot(top_idx, N_EXPERTS, dtype=jnp.float32)
                    * weights[..., None], axis=1)

    def expert_step(acc, params):
        wgu, bgu, wd, bd, g = params
        gu = x2 @ wgu + bgu
        glu = jnp.minimum(gu[:, :D_FF], SWIGLU_LIMIT)
        lin = jnp.clip(gu[:, D_FF:], -SWIGLU_LIMIT, SWIGLU_LIMIT)
        y = (glu * jax.nn.sigmoid(SWIGLU_ALPHA * glu) * (lin + 1)) @ wd + bd
        return acc + g[:, None].astype(acc.dtype) * y.astype(acc.dtype), None

    acc0 = jnp.zeros_like(x2)
    acc, _ = lax.scan(expert_step, acc0, (w_gate_up, b_gate_up, w_down, b_down, gates.T))
    return acc.reshape(B, n, D)


def layer_tail(x, attn_o, fourier_o, gate1, shift2, scale2, gate2, w_out, norm2_g,
               router_w, router_b, w_gate_up, b_gate_up, w_down, b_down):
    mixed = jnp.concatenate([attn_o, fourier_o], axis=-1) @ w_out
    x = x + gate1 * mixed
    h2 = rmsnorm(x, norm2_g) * (1 + scale2) + shift2
    return x + gate2 * moe(h2, router_w, router_b, w_gate_up, b_gate_up, w_down, b_down)


def setup_inputs(seed: int = 0) -> dict:
    key = jax.random.key(seed)
    ks = jax.random.split(key, 24)
    nrm = jax.random.normal
    f32 = jnp.float32
    D = D_MODEL
    return {
        "x_prompt": nrm(ks[0], (BATCH, SEQ, D), f32),
        "x_sample": nrm(ks[1], (DEC_BATCH, DEC_SEQ, D), f32),
        "cache_k": nrm(ks[2], (DEC_BATCH, DEPTH, N_HEADS, PAST_LEN, 2, QK_DIM), f32),
        "cache_v": nrm(ks[3], (DEC_BATCH, DEPTH, N_HEADS, PAST_LEN, V_DIM), f32),
        "c": nrm(ks[4], (DEC_BATCH, D), f32),
        "c_ctx": nrm(ks[5], (D,), f32),
        "w_mod": nrm(ks[6], (DEPTH, D, 6 * D), f32) * (0.2 * D ** -0.5),
        "b_mod": nrm(ks[7], (DEPTH, 6 * D), f32) * 0.01,
        "norm1_g": 1.0 + 0.02 * nrm(ks[8], (DEPTH, D), f32),
        "w_in": nrm(ks[9], (DEPTH, D, IN_WIDTH), f32) * D ** -0.5,
        "lambda_q1": 0.1 * nrm(ks[10], (DEPTH, QK_DIM), f32),
        "lambda_k1": 0.1 * nrm(ks[11], (DEPTH, QK_DIM), f32),
        "lambda_q2": 0.1 * nrm(ks[12], (DEPTH, QK_DIM), f32),
        "lambda_k2": 0.1 * nrm(ks[13], (DEPTH, QK_DIM), f32),
        "subln_g": 1.0 + 0.02 * nrm(ks[14], (DEPTH, V_DIM), f32),
        "w_out": nrm(ks[15], (DEPTH, MIX_WIDTH, D), f32) * MIX_WIDTH ** -0.5,
        "norm2_g": 1.0 + 0.02 * nrm(ks[16], (DEPTH, D), f32),
        "router_w": nrm(ks[17], (DEPTH, D, N_EXPERTS), f32) * D ** -0.5,
        "router_b": 0.01 * nrm(ks[18], (DEPTH, N_EXPERTS), f32),
        "w_gate_up": nrm(ks[19], (DEPTH, N_EXPERTS, D, 2 * D_FF), f32) * D ** -0.5,
        "b_gate_up": 0.01 * nrm(ks[20], (DEPTH, N_EXPERTS, 2 * D_FF), f32),
        "w_down": nrm(ks[21], (DEPTH, N_EXPERTS, D_FF, D), f32) * D_FF ** -0.5,
        "b_down": 0.01 * nrm(ks[22], (DEPTH, N_EXPERTS, D), f32),
        "final_g": 1.0 + 0.02 * nrm(ks[23], (D,), f32),
    }


def reference(x_prompt, x_sample, cache_k, cache_v, c, c_ctx, w_mod, b_mod, norm1_g, w_in,
              lambda_q1, lambda_k1, lambda_q2, lambda_k2, subln_g, w_out, norm2_g,
              router_w, router_b, w_gate_up, b_gate_up, w_down, b_down, final_g):
    xp = x_prompt
    new_k_layers = []
    new_v_layers = []
    for l in range(DEPTH):
        lam_init = lambda_init(l)
        lam = diff_lambda(lambda_q1[l], lambda_k1[l], lambda_q2[l], lambda_k2[l], lam_init)
        shift1, scale1, gate1, shift2, scale2, gate2 = modulation(c_ctx[None, :], w_mod[l], b_mod[l])
        q, k, v, f = project(xp, shift1, scale1, norm1_g[l], w_in[l])
        k_c = jnp.transpose(k, (0, 2, 1, 3, 4))
        v_c = jnp.transpose(v, (0, 2, 1, 3))
        new_k_layers.append(k_c)
        new_v_layers.append(v_c)
        attn_o = diff_attention(q, k_c, v_c, lam, subln_g[l], lam_init)
        xp = layer_tail(xp, attn_o, fourier_mix(f), gate1, shift2, scale2, gate2, w_out[l],
                        norm2_g[l], router_w[l], router_b[l], w_gate_up[l], b_gate_up[l],
                        w_down[l], b_down[l])
    y_prompt = rmsnorm(xp, final_g)
    new_k = jnp.stack(new_k_layers, axis=1)
    new_v = jnp.stack(new_v_layers, axis=1)

    xs = x_sample
    cos, sin = axial_rope_tables(xs.shape[1])
    for l in range(DEPTH):
        lam_init = lambda_init(l)
        lam = diff_lambda(lambda_q1[l], lambda_k1[l], lambda_q2[l], lambda_k2[l], lam_init)
        shift1, scale1, gate1, shift2, scale2, gate2 = modulation(c, w_mod[l], b_mod[l])
        q, k, v, f = project(xs, shift1, scale1, norm1_g[l], w_in[l])
        q = apply_rope(q, cos, sin)
        k = apply_rope(k, cos, sin)
        k_all = jnp.concatenate([cache_k[:, l].astype(k.dtype), jnp.transpose(k, (0, 2, 1, 3, 4))], axis=2)
        v_all = jnp.concatenate([cache_v[:, l].astype(v.dtype), jnp.transpose(v, (0, 2, 1, 3))], axis=2)
        attn_o = diff_attention(q, k_all, v_all, lam, subln_g[l], lam_init)
        xs = layer_tail(xs, attn_o, fourier_mix(f), gate1, shift2, scale2, gate2, w_out[l],
                        norm2_g[l], router_w[l], router_b[l], w_gate_up[l], b_gate_up[l],
                        w_down[l], b_down[l])
    y_sample = rmsnorm(xs, final_g)
    return (y_prompt, y_sample, new_k, new_v)
```

```python
import functools
import math

import jax
import jax.numpy as jnp
from jax import lax
from jax.experimental import pallas as pl
from jax.experimental.pallas import tpu as pltpu

F32 = jnp.float32
BF16 = jnp.bfloat16

GRID_W = 64
N_FGROUPS = 4
TOP_K = 4
SWIGLU_LIMIT = 7.0
SWIGLU_ALPHA = 1.702
ROPE_THETA = 10000.0
NORM_EPS = 1e-6
LAMBDA_INIT = 0.8 - 0.6 * math.exp(-0.3 * 0)

LANES = 128
ROW_CHUNK = 256
MOE_TILE = 256
COMBINE_TILE = 256
VMEM_LIMIT = 56 * 1024 * 1024


def _rsqrt_mean_sq(x):
    return lax.rsqrt(jnp.mean(x * x, axis=-1, keepdims=True) + NORM_EPS)


def _sigmoid(z):
    return 1.0 / (1.0 + jnp.exp(-z))


def _mod_kernel(c_ref, w_ref, b_ref, o_ref):
    c = c_ref[...]
    s = c * _sigmoid(c)
    o_ref[...] = jnp.dot(s.astype(BF16), w_ref[...].astype(BF16),
                         preferred_element_type=F32) + b_ref[...]


def _modulation(cvec, w_mod, b_mod):
    rows, d = cvec.shape
    n_out = w_mod.shape[1]
    return pl.pallas_call(
        _mod_kernel,
        grid=(n_out // d,),
        in_specs=[
            pl.BlockSpec((rows, d), lambda j: (0, 0)),
            pl.BlockSpec((d, d), lambda j: (0, j)),
            pl.BlockSpec((1, d), lambda j: (0, j)),
        ],
        out_specs=pl.BlockSpec((rows, d), lambda j: (0, j)),
        out_shape=jax.ShapeDtypeStruct((rows, n_out), F32),
        name="modulation",
    )(cvec, w_mod, b_mod.reshape(1, n_out))


def _loop(n, body):
    if n == 1:
        body(0)
    else:
        def step(i, carry):
            body(i)
            return carry
        lax.fori_loop(0, n, step, 0)


def _layer_kernel(*refs, n_live, n_inputs, **static):
    if n_live is None:
        _layer_body(*refs, **static)
        return
    b = pl.program_id(0)
    h2_ref = refs[n_inputs + 1]
    pl.when(b < n_live)(functools.partial(_layer_body, *refs, **static))

    @pl.when(b >= n_live)
    def _():
        h2_ref[...] = jnp.zeros(h2_ref.shape, F32)


def _layer_body(*refs, n, n_cache, heads, n_experts, rope, emit_kv):
    it = iter(refs)
    x_ref = next(it); mod_ref = next(it); g1_ref = next(it); win_ref = next(it)
    lamv_ref = next(it); subg_ref = next(it); fcs_ref = next(it); dftn_ref = next(it)
    wout_ref = next(it); g2_ref = next(it); rwh_ref = next(it); rwl_ref = next(it); rb_ref = next(it)
    if rope:
        ck_ref = next(it); cv_ref = next(it); cos_ref = next(it); sin_ref = next(it)
        next(it)
    x1_ref = next(it); h2_ref = next(it); idx_ref = next(it); wts_ref = next(it)
    if emit_kv:
        newk_ref = next(it); newv_ref = next(it)
    q1_scr = next(it); q2_scr = next(it); kall = next(it); vall = next(it)
    f_scr = next(it); stk = next(it); mix = next(it)

    d = x_ref.shape[-1]
    qk_w = heads * LANES
    rc = min(ROW_CHUNK, n)
    n_chunks = n // rc

    def mod_row(j):
        return mod_ref[0, :, j * d:(j + 1) * d]

    shift1, scale1, gate1 = mod_row(0), mod_row(1), mod_row(2)
    shift2, scale2, gate2 = mod_row(3), mod_row(4), mod_row(5)
    del gate2

    lv = lamv_ref[...]
    lam = (jnp.exp(jnp.sum(lv[0:1] * lv[1:2], axis=-1, keepdims=True))
           - jnp.exp(jnp.sum(lv[2:3] * lv[3:4], axis=-1, keepdims=True)) + LAMBDA_INIT)

    if rope:
        for hd in range(heads):
            kall[hd, 0:n_cache, :] = ck_ref[0, hd].astype(BF16)
            vall[hd, 0:n_cache, :] = cv_ref[0, hd].astype(BF16)

    lane = lax.broadcasted_iota(jnp.int32, (rc, LANES), 1)
    first_map = lane < (LANES // 2)

    first_of_pair = jnp.bitwise_and(lane, 31) < 16

    def rotate(t, cos, sin):
        partner = jnp.where(first_of_pair, pltpu.roll(t, LANES - 16, 1), pltpu.roll(t, 16, 1))
        return t * cos + partner * sin

    def project(c):
        r0 = pl.multiple_of(c * rc, rc)
        x = x_ref[0, pl.ds(r0, rc), :]
        h = (x * _rsqrt_mean_sq(x) * g1_ref[...]) * (1.0 + scale1) + shift1
        p = jnp.dot(h.astype(BF16), win_ref[...], preferred_element_type=F32)
        if rope:
            cos = cos_ref[pl.ds(r0, rc), :]
            sin = sin_ref[pl.ds(r0, rc), :]
        for hd in range(heads):
            qh = p[:, hd * LANES:(hd + 1) * LANES]
            kh = p[:, qk_w + hd * LANES:qk_w + (hd + 1) * LANES]
            vh = p[:, 2 * qk_w + hd * LANES:2 * qk_w + (hd + 1) * LANES]
            if rope:
                qh = rotate(qh, cos, sin)
                kh = rotate(kh, cos, sin)
            if emit_kv:
                newk_ref[0, 0, hd, pl.ds(r0, rc), :] = kh
                newv_ref[0, 0, hd, pl.ds(r0, rc), :] = vh
            qs = qh * (LANES // 2) ** -0.5
            q1_scr[pl.ds(r0, rc), hd * LANES:(hd + 1) * LANES] = jnp.where(first_map, qs, 0.0).astype(BF16)
            q2_scr[pl.ds(r0, rc), hd * LANES:(hd + 1) * LANES] = jnp.where(first_map, 0.0, qs).astype(BF16)
            kall[hd, pl.ds(n_cache + r0, rc), :] = kh.astype(BF16)
            vall[hd, pl.ds(n_cache + r0, rc), :] = vh.astype(BF16)
        f_scr[pl.ds(r0, rc), :] = p[:, 3 * qk_w:].astype(BF16)

    _loop(n_chunks, project)

    contract_last = (((1,), (1,)), ((), ()))

    def softmax(s):
        e = jnp.exp(s - jnp.max(s, axis=-1, keepdims=True))
        return e * (1.0 / jnp.sum(e, axis=-1, keepdims=True))

    for hd in range(heads):
        def attend(c, hd=hd):
            r0 = pl.multiple_of(c * rc, rc)
            kh = kall[hd]
            s1 = lax.dot_general(q1_scr[pl.ds(r0, rc), hd * LANES:(hd + 1) * LANES], kh,
                                 contract_last, preferred_element_type=F32)
            s2 = lax.dot_general(q2_scr[pl.ds(r0, rc), hd * LANES:(hd + 1) * LANES], kh,
                                 contract_last, preferred_element_type=F32)
            a = softmax(s1) - lam * softmax(s2)
            o = jnp.dot(a.astype(BF16), vall[hd], preferred_element_type=F32)
            o = o * _rsqrt_mean_sq(o) * subg_ref[...] * (1.0 - LAMBDA_INIT)
            mix[pl.ds(r0, rc), hd * LANES:(hd + 1) * LANES] = o.astype(BF16)

        _loop(n_chunks, attend)

    def dft_channels(c):
        r0 = pl.multiple_of(c * rc, rc)
        for g in range(N_FGROUPS):
            a = jnp.dot(f_scr[pl.ds(r0, rc), g * LANES:(g + 1) * LANES], fcs_ref[...],
                        preferred_element_type=F32)
            stk[pl.ds(r0, rc), g * LANES:(g + 1) * LANES] = a[:, :LANES].astype(BF16)
            stk[pl.ds(pl.multiple_of(n + r0, rc), rc), g * LANES:(g + 1) * LANES] = (
                a[:, LANES:].astype(BF16))

    _loop(n_chunks, dft_channels)

    fscale = 1.0 / math.sqrt(n * LANES)

    def dft_positions(c):
        r0 = pl.multiple_of(c * rc, rc)
        y = jnp.dot(dftn_ref[pl.ds(r0, rc), :], stk[...], preferred_element_type=F32) * fscale
        mix[pl.ds(r0, rc), qk_w:] = y.astype(BF16)

    _loop(n_chunks, dft_positions)

    klane = lax.broadcasted_iota(jnp.int32, (rc, LANES), 1)
    neg_inf = jnp.float32(-jnp.inf)

    def tail(c):
        r0 = pl.multiple_of(c * rc, rc)
        x = x_ref[0, pl.ds(r0, rc), :]
        mixed = jnp.dot(mix[pl.ds(r0, rc), :], wout_ref[...], preferred_element_type=F32)
        x1 = x + gate1 * mixed
        x1_ref[pl.ds(r0, rc), :] = x1
        h2 = (x1 * _rsqrt_mean_sq(x1) * g2_ref[...]) * (1.0 + scale2) + shift2
        h2_ref[pl.ds(r0, rc), :] = h2
        hi = h2.astype(BF16)
        lo = (h2 - hi.astype(F32)).astype(BF16)
        logits = (jnp.dot(hi, rwh_ref[...], preferred_element_type=F32)
                  + jnp.dot(lo, rwh_ref[...], preferred_element_type=F32)
                  + jnp.dot(hi, rwl_ref[...], preferred_element_type=F32))
        l = jnp.where(klane < n_experts, logits + rb_ref[...], neg_inf)
        vals, ids = [], []
        for _ in range(TOP_K):
            m = jnp.max(l, axis=-1, keepdims=True)
            cand = jnp.where(l == m, klane, LANES).astype(F32)
            i = jnp.min(cand, axis=-1, keepdims=True).astype(jnp.int32)
            vals.append(m)
            ids.append(i)
            l = jnp.where(klane == i, neg_inf, l)
        es = [jnp.exp(v - vals[0]) for v in vals]
        inv = 1.0 / functools.reduce(lambda a, b: a + b, es)
        idx_out = jnp.zeros((rc, LANES), jnp.int32)
        wts_out = jnp.zeros((rc, LANES), F32)
        for k in range(TOP_K):
            idx_out = jnp.where(klane == k, ids[k], idx_out)
            wts_out = jnp.where(klane == k, es[k] * inv, wts_out)
        idx_ref[pl.ds(r0, rc), :] = idx_out
        wts_ref[pl.ds(r0, rc), :] = wts_out

    _loop(n_chunks, tail)


def _const_spec(shape):
    nd = len(shape)
    return pl.BlockSpec(shape, lambda b: (0,) * nd, pipeline_mode=pl.Buffered(1))


def _layer(x, mod, g1, win, lamv, subg, fcs, dftn, wout, g2, rwh, rwl, rb, *, n_experts,
           mod_row0, mod_row_step, h2_rows, h2_block0, zero_blocks=0, cache=None, h2_buf=None):
    bsz, n, d = x.shape
    heads = win.shape[1] // (4 * LANES)
    rope = cache is not None
    n_cache = cache[0].shape[2] if rope else 0
    nk = n_cache + n
    live = lambda b: jnp.minimum(b, bsz - 1)
    in_specs = [
        pl.BlockSpec((1, n, d), lambda b: (live(b), 0, 0)),
        pl.BlockSpec((1, 1, mod.shape[-1]),
                     lambda b: (mod_row0 + mod_row_step * live(b), 0, 0)),
        _const_spec(g1.shape), _const_spec(win.shape), _const_spec(lamv.shape),
        _const_spec(subg.shape), _const_spec(fcs.shape), _const_spec(dftn.shape),
        _const_spec(wout.shape), _const_spec(g2.shape), _const_spec(rwh.shape),
        _const_spec(rwl.shape), _const_spec(rb.shape),
    ]
    args = [x, mod, g1, win, lamv, subg, fcs, dftn, wout, g2, rwh, rwl, rb]
    out_specs = [
        pl.BlockSpec((n, d), lambda b: (live(b), 0)),
        pl.BlockSpec((n, d), lambda b: (h2_block0 + b, 0)),
        pl.BlockSpec((n, LANES), lambda b: (live(b), 0)),
        pl.BlockSpec((n, LANES), lambda b: (live(b), 0)),
    ]
    out_shape = [
        jax.ShapeDtypeStruct((bsz * n, d), F32),
        jax.ShapeDtypeStruct((h2_rows, d), F32),
        jax.ShapeDtypeStruct((bsz * n, LANES), jnp.int32),
        jax.ShapeDtypeStruct((bsz * n, LANES), F32),
    ]
    aliases = {}
    if rope:
        ck, cv, cos, sin = cache
        in_specs += [
            pl.BlockSpec((1, heads, n_cache, LANES), lambda b: (live(b), 0, 0, 0)),
            pl.BlockSpec((1, heads, n_cache, LANES), lambda b: (live(b), 0, 0, 0)),
            _const_spec(cos.shape), _const_spec(sin.shape),
            pl.BlockSpec(memory_space=pl.ANY),
        ]
        args += [ck, cv, cos, sin, h2_buf]
        aliases = {len(args) - 1: 1}
    else:
        kv_spec = pl.BlockSpec((1, 1, heads, n, LANES), lambda b: (live(b), 0, 0, 0, 0))
        out_specs += [kv_spec, kv_spec]
        kv_shape = jax.ShapeDtypeStruct((bsz, 1, heads, n, LANES), F32)
        out_shape += [kv_shape, kv_shape]
    scratch = [
        pltpu.VMEM((n, heads * LANES), BF16),
        pltpu.VMEM((n, heads * LANES), BF16),
        pltpu.VMEM((heads, nk, LANES), BF16),
        pltpu.VMEM((heads, nk, LANES), BF16),
        pltpu.VMEM((n, N_FGROUPS * LANES), BF16),
        pltpu.VMEM((2 * n, N_FGROUPS * LANES), BF16),
        pltpu.VMEM((n, d), BF16),
    ]
    kern = functools.partial(_layer_kernel, n_live=bsz if zero_blocks else None,
                             n_inputs=len(args), n=n, n_cache=n_cache, heads=heads,
                             n_experts=n_experts, rope=rope, emit_kv=not rope)
    return pl.pallas_call(
        kern,
        grid=(bsz + zero_blocks,),
        in_specs=in_specs,
        out_specs=out_specs,
        out_shape=out_shape,
        scratch_shapes=scratch,
        input_output_aliases=aliases,
        compiler_params=pltpu.CompilerParams(dimension_semantics=("arbitrary",),
                                             vmem_limit_bytes=VMEM_LIMIT),
        name="layer_latent" if rope else "layer_context",
    )(*args)


def _moe_kernel(texp_ref, nvalid_ref, tok_ref, tok_next_ref, dst_ref, h2_hbm, wgu_ref, bgu_ref,
                wd_ref, bd_ref, y_hbm, xbuf, ybuf, wgu_bf, wd_bf, gsem, ssem, *, tm):
    i = pl.program_id(0)
    nv = nvalid_ref[0]
    d_ff = wd_ref.shape[1]

    def gather_row(idx_ref, slot, r):
        return pltpu.make_async_copy(h2_hbm.at[pl.ds(idx_ref[0, 0, r], 1), :],
                                     xbuf.at[slot, pl.ds(r, 1), :], gsem.at[slot])

    def scatter_row(slot, r):
        return pltpu.make_async_copy(ybuf.at[slot, pl.ds(r, 1), :],
                                     y_hbm.at[pl.ds(dst_ref[0, 0, r], 1), :], ssem.at[slot])

    @pl.when(i < nv)
    def _():
        slot = i % 2

        @pl.when(i == 0)
        def _():
            for r in range(tm):
                gather_row(tok_ref, 0, r).start()
            ybuf[...] = jnp.zeros(ybuf.shape, F32)
            spare0 = y_hbm.shape[0] - 2 * tm
            fills = [pltpu.make_async_copy(ybuf.at[s], y_hbm.at[pl.ds(spare0 + s * tm, tm), :],
                                           ssem.at[s]) for s in range(2)]
            for f in fills:
                f.start()
            for f in fills:
                f.wait()

        @pl.when(i + 1 < nv)
        def _():
            for r in range(tm):
                gather_row(tok_next_ref, 1 - slot, r).start()

        @pl.when(jnp.logical_or(i == 0, texp_ref[i] != texp_ref[jnp.maximum(i - 1, 0)]))
        def _():
            wgu_bf[...] = wgu_ref[0].astype(BF16)
            wd_bf[...] = wd_ref[0].astype(BF16)

        for r in range(tm):
            gather_row(tok_ref, slot, r).wait()

        @pl.when(i >= 2)
        def _():
            for r in range(tm):
                scatter_row(slot, r).wait()

        x = xbuf[slot].astype(BF16)
        gu = jnp.dot(x, wgu_bf[...], preferred_element_type=F32) + bgu_ref[0]
        glu = jnp.minimum(gu[:, :d_ff], SWIGLU_LIMIT)
        lin = jnp.clip(gu[:, d_ff:], -SWIGLU_LIMIT, SWIGLU_LIMIT)
        act = glu * _sigmoid(SWIGLU_ALPHA * glu) * (lin + 1.0)
        ybuf[slot] = jnp.dot(act.astype(BF16), wd_bf[...], preferred_element_type=F32) + bd_ref[0]

        for r in range(tm):
            scatter_row(slot, r).start()

        @pl.when(i == nv - 1)
        def _():
            for r in range(tm):
                scatter_row(slot, r).wait()

            @pl.when(i >= 1)
            def _():
                for r in range(tm):
                    scatter_row(1 - slot, r).wait()


def _moe(h2, tile_expert, n_valid, src_tok, dest, wgu, bgu, wd, bd):
    t_rows, d = h2.shape
    n_exp, _, two_f = wgu.shape
    d_ff = two_f // 2
    n_tiles = tile_expert.shape[0]
    tm = src_tok.shape[-1]
    last = n_tiles - 1
    grid_spec = pltpu.PrefetchScalarGridSpec(
        num_scalar_prefetch=2,
        grid=(n_tiles,),
        in_specs=[
            pl.BlockSpec((1, 1, tm), lambda i, te, nv: (i, 0, 0), memory_space=pltpu.SMEM),
            pl.BlockSpec((1, 1, tm), lambda i, te, nv: (jnp.minimum(i + 1, last), 0, 0),
                         memory_space=pltpu.SMEM),
            pl.BlockSpec((1, 1, tm), lambda i, te, nv: (i, 0, 0), memory_space=pltpu.SMEM),
            pl.BlockSpec(memory_space=pl.ANY),
            pl.BlockSpec((1, d, two_f), lambda i, te, nv: (te[i], 0, 0)),
            pl.BlockSpec((1, 1, two_f), lambda i, te, nv: (te[i], 0, 0)),
            pl.BlockSpec((1, d_ff, d), lambda i, te, nv: (te[i], 0, 0)),
            pl.BlockSpec((1, 1, d), lambda i, te, nv: (te[i], 0, 0)),
        ],
        out_specs=pl.BlockSpec(memory_space=pl.ANY),
        scratch_shapes=[
            pltpu.VMEM((2, tm, d), F32),
            pltpu.VMEM((2, tm, d), F32),
            pltpu.VMEM((d, two_f), BF16),
            pltpu.VMEM((d_ff, d), BF16),
            pltpu.SemaphoreType.DMA((2,)),
            pltpu.SemaphoreType.DMA((2,)),
        ],
    )
    return pl.pallas_call(
        functools.partial(_moe_kernel, tm=tm),
        grid_spec=grid_spec,
        out_shape=jax.ShapeDtypeStruct((TOP_K * t_rows + 2 * tm, d), F32),
        compiler_params=pltpu.CompilerParams(dimension_semantics=("arbitrary",),
                                             vmem_limit_bytes=VMEM_LIMIT),
        name="routed_moe",
    )(tile_expert, n_valid, src_tok, src_tok, dest, h2, wgu,
      bgu.reshape(n_exp, 1, two_f), wd, bd.reshape(n_exp, 1, d))


def _route(idx, n_exp, tm, n_tiles):
    t_rows, top_k = idx.shape
    e_flat = idx.T.reshape(-1)
    order = jnp.argsort(e_flat, stable=True).astype(jnp.int32)
    counts = jnp.sum(e_flat[:, None] == jnp.arange(n_exp, dtype=jnp.int32)[None, :], axis=0,
                     dtype=jnp.int32)
    padded = ((counts + tm - 1) // tm) * tm
    pad_end = jnp.cumsum(padded)
    pad_start = pad_end - padded
    start = jnp.cumsum(counts) - counts
    pos = jnp.arange(n_tiles * tm, dtype=jnp.int32)
    e_pos = jnp.minimum(jnp.sum(pos[:, None] >= pad_end[None, :], axis=1, dtype=jnp.int32),
                        n_exp - 1)
    rank = pos - pad_start[e_pos]
    valid = jnp.logical_and(rank < counts[e_pos], pos < pad_end[-1])
    pair = order[jnp.clip(start[e_pos] + rank, 0, t_rows * top_k - 1)]
    spare = t_rows * top_k + pos % (2 * tm)
    dest = jnp.where(valid, pair, spare)
    src_tok = jnp.where(valid, pair % t_rows, 0)
    n_valid = pad_end[-1] // tm
    tile_first = jnp.minimum(jnp.arange(n_tiles, dtype=jnp.int32), n_valid - 1) * tm
    tile_expert = e_pos[tile_first]
    return (tile_expert, n_valid.reshape(1), src_tok.reshape(n_tiles, 1, tm),
            dest.reshape(n_tiles, 1, tm))


def _combine_kernel(x1_ref, y0_ref, y1_ref, y2_ref, y3_ref, wts_ref, gate_ref, fg_ref, o_ref):
    w = wts_ref[...]
    acc = w[:, 0:1] * y0_ref[...]
    for k, y_ref in enumerate((y1_ref, y2_ref, y3_ref), start=1):
        acc = acc + w[:, k:k + 1] * y_ref[...]
    x = x1_ref[...] + gate_ref[0] * acc
    o_ref[...] = x * _rsqrt_mean_sq(x) * fg_ref[...]


def _combine(x1, y, wts, gate2, final_g, *, t_all, row0, rows_per_gate):
    rows, d = x1.shape
    tc = COMBINE_TILE
    y_specs = [
        pl.BlockSpec((tc, d), functools.partial(
            lambda i, k: ((k * t_all + row0) // tc + i, 0), k=k))
        for k in range(TOP_K)
    ]
    return pl.pallas_call(
        _combine_kernel,
        grid=(rows // tc,),
        in_specs=[pl.BlockSpec((tc, d), lambda i: (i, 0))] + y_specs + [
            pl.BlockSpec((tc, LANES), lambda i: (i, 0)),
            pl.BlockSpec((1, 1, d), lambda i: ((i * tc) // rows_per_gate, 0, 0)),
            pl.BlockSpec((1, d), lambda i: (0, 0)),
        ],
        out_specs=pl.BlockSpec((tc, d), lambda i: (i, 0)),
        out_shape=jax.ShapeDtypeStruct((rows, d), F32),
        compiler_params=pltpu.CompilerParams(dimension_semantics=("arbitrary",)),
        name="combine",
    )(x1, y, y, y, y, wts, gate2, final_g)


def _dft_tables(n):
    def angles(m):
        k = jnp.arange(m, dtype=jnp.int32)
        return (2.0 * math.pi / m) * ((k[:, None] * k[None, :]) % m).astype(F32)
    an = angles(n)
    ac = angles(LANES)
    dftn = jnp.concatenate([jnp.cos(an), -jnp.sin(an)], axis=1).astype(BF16)
    fcs = jnp.concatenate([jnp.cos(ac), jnp.sin(ac)], axis=1).astype(BF16)
    return dftn, fcs


def _rope_tables(n, qk_dim):
    quarter = qk_dim // 4
    tok = jnp.arange(n)
    pos = jnp.stack([tok // GRID_W, tok % GRID_W], axis=-1).astype(F32)
    freqs = ROPE_THETA ** (-jnp.arange(quarter, dtype=F32) / quarter)
    ang = (pos[:, :, None] * freqs).reshape(n, 2 * quarter)
    cos, sin = jnp.cos(ang), jnp.sin(ang)
    row_c, col_c = cos[:, :quarter], cos[:, quarter:]
    row_s, col_s = sin[:, :quarter], sin[:, quarter:]
    cos_map = jnp.concatenate([row_c, row_c, col_c, col_c], axis=-1)
    sin_map = jnp.concatenate([-row_s, row_s, -col_s, col_s], axis=-1)
    reps = LANES // qk_dim
    return jnp.tile(cos_map, (1, reps)), jnp.tile(sin_map, (1, reps))


def kernel(x_prompt, x_sample, cache_k, cache_v, c, c_ctx, w_mod, b_mod, norm1_g, w_in, lambda_q1,
           lambda_k1, lambda_q2, lambda_k2, subln_g, w_out, norm2_g, router_w, router_b, w_gate_up,
           b_gate_up, w_down, b_down, final_g):
    bsz, seq, d = x_prompt.shape
    dec_b, dec_seq, _ = x_sample.shape
    heads, past, qk_dim = cache_k.shape[2], cache_k.shape[3], cache_k.shape[5]
    n_exp = router_w.shape[-1]
    t_ctx, t_den = bsz * seq, dec_b * dec_seq
    t_all = t_ctx + t_den
    assert t_ctx % dec_seq == 0 and 2 * qk_dim == LANES and dec_seq % GRID_W == 0
    assert (t_all * TOP_K) % MOE_TILE == 0 and t_all % COMBINE_TILE == 0

    cvec = jnp.concatenate([c_ctx[None, :], c, jnp.zeros((8 - 1 - dec_b, d), F32)], axis=0)
    mod = _modulation(cvec, w_mod[0], b_mod[0])[:, None, :]

    win = w_in[0].astype(BF16)
    wout = w_out[0].astype(BF16)
    rw = jnp.pad(router_w[0], ((0, 0), (0, LANES - n_exp)))
    rwh = rw.astype(BF16)
    rwl = (rw - rwh.astype(F32)).astype(BF16)
    rb = jnp.pad(router_b[0], (0, LANES - n_exp)).reshape(1, LANES)
    lamv = jnp.stack([lambda_q1[0], lambda_k1[0], lambda_q2[0], lambda_k2[0]], axis=0)
    g1 = norm1_g[0].reshape(1, d)
    g2 = norm2_g[0].reshape(1, d)
    subg = subln_g[0].reshape(1, LANES)
    dft_ctx, fcs = _dft_tables(seq)
    dft_den, _ = _dft_tables(dec_seq)
    cos, sin = _rope_tables(dec_seq, qk_dim)

    shared = (g1, win, lamv, subg, fcs)
    tail = (wout, g2, rwh, rwl, rb)
    x1_ctx, h2_all, idx_ctx, wts_ctx, new_k, new_v = _layer(
        x_prompt, mod, *shared, dft_ctx, *tail, n_experts=n_exp, mod_row0=0, mod_row_step=0,
        h2_rows=t_all, h2_block0=0, zero_blocks=t_den // seq)
    ck = cache_k[:, 0].reshape(dec_b, heads, past, LANES)
    cv = cache_v[:, 0]
    x1_den, h2_all, idx_den, wts_den = _layer(
        x_sample, mod, *shared, dft_den, *tail, n_experts=n_exp, mod_row0=1, mod_row_step=1,
        h2_rows=t_all, h2_block0=t_ctx // dec_seq, cache=(ck, cv, cos, sin), h2_buf=h2_all)

    idx = jnp.concatenate([idx_ctx[:, :TOP_K], idx_den[:, :TOP_K]], axis=0)
    n_tiles = (t_all * TOP_K) // MOE_TILE + n_exp
    tile_expert, n_valid, src_tok, dest = _route(idx, n_exp, MOE_TILE, n_tiles)
    y = _moe(h2_all, tile_expert, n_valid, src_tok, dest, w_gate_up[0], b_gate_up[0], w_down[0],
             b_down[0])

    gate2 = mod[:, :, 5 * d:]
    fg = final_g.reshape(1, d)
    y_prompt = _combine(x1_ctx, y, wts_ctx, gate2[0:1], fg, t_all=t_all, row0=0,
                        rows_per_gate=t_ctx)
    y_sample = _combine(x1_den, y, wts_den, gate2[1:1 + dec_b], fg, t_all=t_all, row0=t_ctx,
                        rows_per_gate=dec_seq)
    return (y_prompt.reshape(bsz, seq, d), y_sample.reshape(dec_b, dec_seq, d),
            new_k.reshape(bsz, 1, heads, seq, 2, qk_dim), new_v)
```

```python
import functools
import math

import numpy as np
import jax
import jax.numpy as jnp
from jax import lax
from jax.experimental import pallas as pl
from jax.experimental.pallas import tpu as pltpu

F32 = jnp.float32
BF16 = jnp.bfloat16

GRID_W = 64
N_FGROUPS = 4
TOP_K = 4
SWIGLU_LIMIT = 7.0
SWIGLU_ALPHA = 1.702
ROPE_THETA = 10000.0
NORM_EPS = 1e-6
LAMBDA_INIT = 0.8 - 0.6 * math.exp(-0.3 * 0)

LANES = 128
ROW_CHUNK = 256
MOE_TILE = 256
COMBINE_TILE = 256
VMEM_LIMIT = 56 * 1024 * 1024


def _rsqrt_mean_sq(x):
    return lax.rsqrt(jnp.mean(x * x, axis=-1, keepdims=True) + NORM_EPS)


def _sigmoid(z):
    return 1.0 / (1.0 + jnp.exp(-z))


def _mod_kernel(c_ref, w_ref, b_ref, o_ref):
    c = c_ref[...]
    s = c * _sigmoid(c)
    o_ref[...] = jnp.dot(s.astype(BF16), w_ref[...].astype(BF16),
                         preferred_element_type=F32) + b_ref[...]


def _modulation(cvec, w_mod, b_mod):
    rows, d = cvec.shape
    n_out = w_mod.shape[1]
    return pl.pallas_call(
        _mod_kernel,
        grid=(n_out // d,),
        in_specs=[
            pl.BlockSpec((rows, d), lambda j: (0, 0)),
            pl.BlockSpec((d, d), lambda j: (0, j)),
            pl.BlockSpec((1, d), lambda j: (0, j)),
        ],
        out_specs=pl.BlockSpec((rows, d), lambda j: (0, j)),
        out_shape=jax.ShapeDtypeStruct((rows, n_out), F32),
        name="modulation",
    )(cvec, w_mod, b_mod.reshape(1, n_out))


def _loop(n, body):
    if n == 1:
        body(0)
    else:
        def step(i, carry):
            body(i)
            return carry
        lax.fori_loop(0, n, step, 0)


def _layer_kernel(*refs, n_live, n_inputs, **static):
    if n_live is None:
        _layer_body(*refs, **static)
        return
    b = pl.program_id(0)
    h2_ref = refs[n_inputs + 1]
    pl.when(b < n_live)(functools.partial(_layer_body, *refs, **static))

    @pl.when(b >= n_live)
    def _():
        h2_ref[...] = jnp.zeros(h2_ref.shape, F32)


def _layer_body(*refs, n, n_cache, heads, n_experts, rope, emit_kv):
    it = iter(refs)
    x_ref = next(it); mod_ref = next(it); g1_ref = next(it); win_ref = next(it)
    lamv_ref = next(it); subg_ref = next(it); fcs_ref = next(it); dftn_ref = next(it)
    wout_ref = next(it); g2_ref = next(it); rwh_ref = next(it); rwl_ref = next(it); rb_ref = next(it)
    if rope:
        ck_ref = next(it); cv_ref = next(it); cos_ref = next(it); sin_ref = next(it)
        next(it)
    x1_ref = next(it); h2_ref = next(it); idx_ref = next(it); wts_ref = next(it)
    if emit_kv:
        newk_ref = next(it); newv_ref = next(it)
    q1_scr = next(it); q2_scr = next(it); kall = next(it); vall = next(it)
    f_scr = next(it); stk = next(it); mix = next(it)

    d = x_ref.shape[-1]
    qk_w = heads * LANES
    rc = min(ROW_CHUNK, n)
    n_chunks = n // rc

    def mod_row(j):
        return mod_ref[0, :, j * d:(j + 1) * d]

    shift1, scale1, gate1 = mod_row(0), mod_row(1), mod_row(2)
    shift2, scale2, gate2 = mod_row(3), mod_row(4), mod_row(5)
    del gate2

    lv = lamv_ref[...]
    lam = (jnp.exp(jnp.sum(lv[0:1] * lv[1:2], axis=-1, keepdims=True))
           - jnp.exp(jnp.sum(lv[2:3] * lv[3:4], axis=-1, keepdims=True)) + LAMBDA_INIT)

    if rope:
        for hd in range(heads):
            kall[hd, 0:n_cache, :] = ck_ref[0, hd].astype(BF16)
            vall[hd, 0:n_cache, :] = cv_ref[0, hd].astype(BF16)

    lane = lax.broadcasted_iota(jnp.int32, (rc, LANES), 1)
    first_map = lane < (LANES // 2)

    first_of_pair = jnp.bitwise_and(lane, 31) < 16

    def rotate(t, cos, sin):
        partner = jnp.where(first_of_pair, pltpu.roll(t, LANES - 16, 1), pltpu.roll(t, 16, 1))
        return t * cos + partner * sin

    def project(c):
        r0 = pl.multiple_of(c * rc, rc)
        x = x_ref[0, pl.ds(r0, rc), :]
        h = (x * _rsqrt_mean_sq(x) * g1_ref[...]) * (1.0 + scale1) + shift1
        p = jnp.dot(h.astype(BF16), win_ref[...], preferred_element_type=F32)
        if rope:
            cos = cos_ref[pl.ds(r0, rc), :]
            sin = sin_ref[pl.ds(r0, rc), :]
        for hd in range(heads):
            qh = p[:, hd * LANES:(hd + 1) * LANES]
            kh = p[:, qk_w + hd * LANES:qk_w + (hd + 1) * LANES]
            vh = p[:, 2 * qk_w + hd * LANES:2 * qk_w + (hd + 1) * LANES]
            if rope:
                qh = rotate(qh, cos, sin)
                kh = rotate(kh, cos, sin)
            if emit_kv:
                newk_ref[0, 0, hd, pl.ds(r0, rc), :] = kh
                newv_ref[0, 0, hd, pl.ds(r0, rc), :] = vh
            qs = qh * (LANES // 2) ** -0.5
            q1_scr[pl.ds(r0, rc), hd * LANES:(hd + 1) * LANES] = jnp.where(first_map, qs, 0.0).astype(BF16)
            q2_scr[pl.ds(r0, rc), hd * LANES:(hd + 1) * LANES] = jnp.where(first_map, 0.0, qs).astype(BF16)
            kall[hd, pl.ds(n_cache + r0, rc), :] = kh.astype(BF16)
            vall[hd, pl.ds(n_cache + r0, rc), :] = vh.astype(BF16)
        f_scr[pl.ds(r0, rc), :] = p[:, 3 * qk_w:].astype(BF16)

    _loop(n_chunks, project)

    contract_last = (((1,), (1,)), ((), ()))

    def softmax(s):
        e = jnp.exp(s - jnp.max(s, axis=-1, keepdims=True))
        return e * (1.0 / jnp.sum(e, axis=-1, keepdims=True))

    for hd in range(heads):
        def attend(c, hd=hd):
            r0 = pl.multiple_of(c * rc, rc)
            kh = kall[hd]
            s1 = lax.dot_general(q1_scr[pl.ds(r0, rc), hd * LANES:(hd + 1) * LANES], kh,
                                 contract_last, preferred_element_type=F32)
            s2 = lax.dot_general(q2_scr[pl.ds(r0, rc), hd * LANES:(hd + 1) * LANES], kh,
                                 contract_last, preferred_element_type=F32)
            a = softmax(s1) - lam * softmax(s2)
            o = jnp.dot(a.astype(BF16), vall[hd], preferred_element_type=F32)
            o = o * _rsqrt_mean_sq(o) * subg_ref[...] * (1.0 - LAMBDA_INIT)
            mix[pl.ds(r0, rc), hd * LANES:(hd + 1) * LANES] = o.astype(BF16)

        _loop(n_chunks, attend)

    def dft_channels(c):
        r0 = pl.multiple_of(c * rc, rc)
        for g in range(N_FGROUPS):
            a = jnp.dot(f_scr[pl.ds(r0, rc), g * LANES:(g + 1) * LANES], fcs_ref[...],
                        preferred_element_type=F32)
            stk[pl.ds(r0, rc), g * LANES:(g + 1) * LANES] = a[:, :LANES].astype(BF16)
            stk[pl.ds(pl.multiple_of(n + r0, rc), rc), g * LANES:(g + 1) * LANES] = (
                a[:, LANES:].astype(BF16))

    _loop(n_chunks, dft_channels)

    fscale = 1.0 / math.sqrt(n * LANES)

    def dft_positions(c):
        r0 = pl.multiple_of(c * rc, rc)
        y = jnp.dot(dftn_ref[pl.ds(r0, rc), :], stk[...], preferred_element_type=F32) * fscale
        mix[pl.ds(r0, rc), qk_w:] = y.astype(BF16)

    _loop(n_chunks, dft_positions)

    klane = lax.broadcasted_iota(jnp.int32, (rc, LANES), 1)
    neg_inf = jnp.float32(-jnp.inf)

    def tail(c):
        r0 = pl.multiple_of(c * rc, rc)
        x = x_ref[0, pl.ds(r0, rc), :]
        mixed = jnp.dot(mix[pl.ds(r0, rc), :], wout_ref[...], preferred_element_type=F32)
        x1 = x + gate1 * mixed
        x1_ref[pl.ds(r0, rc), :] = x1
        h2 = (x1 * _rsqrt_mean_sq(x1) * g2_ref[...]) * (1.0 + scale2) + shift2
        tpr = d // LANES
        for j in range(tpr):
            h2_ref[pl.ds(r0 * tpr + j, rc, stride=tpr), :] = h2[:, j * LANES:(j + 1) * LANES]
        hi = h2.astype(BF16)
        lo = (h2 - hi.astype(F32)).astype(BF16)
        logits = (jnp.dot(hi, rwh_ref[...], preferred_element_type=F32)
                  + jnp.dot(lo, rwh_ref[...], preferred_element_type=F32)
                  + jnp.dot(hi, rwl_ref[...], preferred_element_type=F32))
        l = jnp.where(klane < n_experts, logits + rb_ref[...], neg_inf)
        vals, ids = [], []
        for _ in range(TOP_K):
            m = jnp.max(l, axis=-1, keepdims=True)
            cand = jnp.where(l == m, klane, LANES).astype(F32)
            i = jnp.min(cand, axis=-1, keepdims=True).astype(jnp.int32)
            vals.append(m)
            ids.append(i)
            l = jnp.where(klane == i, neg_inf, l)
        es = [jnp.exp(v - vals[0]) for v in vals]
        inv = 1.0 / functools.reduce(lambda a, b: a + b, es)
        idx_out = jnp.zeros((rc, LANES), jnp.int32)
        wts_out = jnp.zeros((rc, LANES), F32)
        for k in range(TOP_K):
            idx_out = jnp.where(klane == k, ids[k], idx_out)
            wts_out = jnp.where(klane == k, es[k] * inv, wts_out)
        idx_ref[pl.ds(r0, rc), :] = idx_out
        wts_ref[pl.ds(r0, rc), :] = wts_out

    _loop(n_chunks, tail)


def _const_spec(shape):
    nd = len(shape)
    return pl.BlockSpec(shape, lambda b: (0,) * nd, pipeline_mode=pl.Buffered(1))


def _layer(x, mod, g1, win, lamv, subg, fcs, dftn, wout, g2, rwh, rwl, rb, *, n_experts,
           mod_row0, mod_row_step, h2_rows, h2_block0, zero_blocks=0, cache=None, h2_buf=None):
    bsz, n, d = x.shape
    heads = win.shape[1] // (4 * LANES)
    rope = cache is not None
    n_cache = cache[0].shape[2] if rope else 0
    nk = n_cache + n
    live = lambda b: jnp.minimum(b, bsz - 1)
    in_specs = [
        pl.BlockSpec((1, n, d), lambda b: (live(b), 0, 0)),
        pl.BlockSpec((1, 1, mod.shape[-1]),
                     lambda b: (mod_row0 + mod_row_step * live(b), 0, 0)),
        _const_spec(g1.shape), _const_spec(win.shape), _const_spec(lamv.shape),
        _const_spec(subg.shape), _const_spec(fcs.shape), _const_spec(dftn.shape),
        _const_spec(wout.shape), _const_spec(g2.shape), _const_spec(rwh.shape),
        _const_spec(rwl.shape), _const_spec(rb.shape),
    ]
    args = [x, mod, g1, win, lamv, subg, fcs, dftn, wout, g2, rwh, rwl, rb]
    out_specs = [
        pl.BlockSpec((n, d), lambda b: (live(b), 0)),
        pl.BlockSpec((n * (d // LANES), LANES), lambda b: (h2_block0 + b, 0)),
        pl.BlockSpec((n, LANES), lambda b: (live(b), 0)),
        pl.BlockSpec((n, LANES), lambda b: (live(b), 0)),
    ]
    out_shape = [
        jax.ShapeDtypeStruct((bsz * n, d), F32),
        jax.ShapeDtypeStruct((h2_rows * (d // LANES), LANES), F32),
        jax.ShapeDtypeStruct((bsz * n, LANES), jnp.int32),
        jax.ShapeDtypeStruct((bsz * n, LANES), F32),
    ]
    aliases = {}
    if rope:
        ck, cv, cos, sin = cache
        in_specs += [
            pl.BlockSpec((1, heads, n_cache, LANES), lambda b: (live(b), 0, 0, 0)),
            pl.BlockSpec((1, heads, n_cache, LANES), lambda b: (live(b), 0, 0, 0)),
            _const_spec(cos.shape), _const_spec(sin.shape),
            pl.BlockSpec(memory_space=pl.ANY),
        ]
        args += [ck, cv, cos, sin, h2_buf]
        aliases = {len(args) - 1: 1}
    else:
        kv_spec = pl.BlockSpec((1, 1, heads, n, LANES), lambda b: (live(b), 0, 0, 0, 0))
        out_specs += [kv_spec, kv_spec]
        kv_shape = jax.ShapeDtypeStruct((bsz, 1, heads, n, LANES), F32)
        out_shape += [kv_shape, kv_shape]
    scratch = [
        pltpu.VMEM((n, heads * LANES), BF16),
        pltpu.VMEM((n, heads * LANES), BF16),
        pltpu.VMEM((heads, nk, LANES), BF16),
        pltpu.VMEM((heads, nk, LANES), BF16),
        pltpu.VMEM((n, N_FGROUPS * LANES), BF16),
        pltpu.VMEM((2 * n, N_FGROUPS * LANES), BF16),
        pltpu.VMEM((n, d), BF16),
    ]
    kern = functools.partial(_layer_kernel, n_live=bsz if zero_blocks else None,
                             n_inputs=len(args), n=n, n_cache=n_cache, heads=heads,
                             n_experts=n_experts, rope=rope, emit_kv=not rope)
    return pl.pallas_call(
        kern,
        grid=(bsz + zero_blocks,),
        in_specs=in_specs,
        out_specs=out_specs,
        out_shape=out_shape,
        scratch_shapes=scratch,
        input_output_aliases=aliases,
        compiler_params=pltpu.CompilerParams(dimension_semantics=("arbitrary",),
                                             vmem_limit_bytes=VMEM_LIMIT),
        name="layer_latent" if rope else "layer_context",
    )(*args)


def _moe_kernel(texp_ref, nvalid_ref, tok_ref, tok_next_ref, dst_ref, h2_hbm, wgu_ref, bgu_ref,
                wd_ref, bd_ref, y_hbm, xbuf, ybuf, wgu_bf, wd_bf, gsem, ssem, *, tm):
    i = pl.program_id(0)
    nv = nvalid_ref[0]
    d_ff, d = wd_ref.shape[1], wd_ref.shape[2]
    tpr = d // LANES

    def token_rows(t):
        return pl.ds(pl.multiple_of(t * tpr, tpr), tpr)

    def gather_row(idx_ref, slot, r):
        return pltpu.make_async_copy(h2_hbm.at[token_rows(idx_ref[0, 0, r]), :],
                                     xbuf.at[slot, pl.ds(r * tpr, tpr), :], gsem.at[slot])

    def scatter_row(slot, r):
        return pltpu.make_async_copy(ybuf.at[slot, pl.ds(r * tpr, tpr), :],
                                     y_hbm.at[token_rows(dst_ref[0, 0, r]), :], ssem.at[slot])

    @pl.when(i < nv)
    def _():
        slot = i % 2

        @pl.when(i == 0)
        def _():
            for r in range(tm):
                gather_row(tok_ref, 0, r).start()
            ybuf[...] = jnp.zeros(ybuf.shape, F32)
            spare0 = y_hbm.shape[0] - 2 * tm * tpr
            fills = [pltpu.make_async_copy(
                ybuf.at[s], y_hbm.at[pl.ds(spare0 + s * tm * tpr, tm * tpr), :], ssem.at[s])
                for s in range(2)]
            for f in fills:
                f.start()
            for f in fills:
                f.wait()

        @pl.when(i + 1 < nv)
        def _():
            for r in range(tm):
                gather_row(tok_next_ref, 1 - slot, r).start()

        @pl.when(jnp.logical_or(i == 0, texp_ref[i] != texp_ref[jnp.maximum(i - 1, 0)]))
        def _():
            wgu_bf[...] = wgu_ref[0].astype(BF16)
            wd_bf[...] = wd_ref[0].astype(BF16)

        for r in range(tm):
            gather_row(tok_ref, slot, r).wait()

        @pl.when(i >= 2)
        def _():
            for r in range(tm):
                scatter_row(slot, r).wait()

        x = jnp.concatenate(
            [xbuf[slot, pl.ds(j, tm, stride=tpr), :].astype(BF16) for j in range(tpr)], axis=1)
        gu = jnp.dot(x, wgu_bf[...], preferred_element_type=F32) + bgu_ref[0]
        glu = jnp.minimum(gu[:, :d_ff], SWIGLU_LIMIT)
        lin = jnp.clip(gu[:, d_ff:], -SWIGLU_LIMIT, SWIGLU_LIMIT)
        act = glu * _sigmoid(SWIGLU_ALPHA * glu) * (lin + 1.0)
        y = jnp.dot(act.astype(BF16), wd_bf[...], preferred_element_type=F32) + bd_ref[0]
        for j in range(tpr):
            ybuf[slot, pl.ds(j, tm, stride=tpr), :] = y[:, j * LANES:(j + 1) * LANES]

        for r in range(tm):
            scatter_row(slot, r).start()

        @pl.when(i == nv - 1)
        def _():
            for r in range(tm):
                scatter_row(slot, r).wait()

            @pl.when(i >= 1)
            def _():
                for r in range(tm):
                    scatter_row(1 - slot, r).wait()


def _moe(h2, tile_expert, n_valid, src_tok, dest, wgu, bgu, wd, bd):
    n_exp, d, two_f = wgu.shape
    tpr = d // LANES
    t_rows = h2.shape[0] // tpr
    d_ff = two_f // 2
    n_tiles = tile_expert.shape[0]
    tm = src_tok.shape[-1]
    last = n_tiles - 1
    grid_spec = pltpu.PrefetchScalarGridSpec(
        num_scalar_prefetch=2,
        grid=(n_tiles,),
        in_specs=[
            pl.BlockSpec((1, 1, tm), lambda i, te, nv: (i, 0, 0), memory_space=pltpu.SMEM),
            pl.BlockSpec((1, 1, tm), lambda i, te, nv: (jnp.minimum(i + 1, last), 0, 0),
                         memory_space=pltpu.SMEM),
            pl.BlockSpec((1, 1, tm), lambda i, te, nv: (i, 0, 0), memory_space=pltpu.SMEM),
            pl.BlockSpec(memory_space=pl.ANY),
            pl.BlockSpec((1, d, two_f), lambda i, te, nv: (te[i], 0, 0)),
            pl.BlockSpec((1, 1, two_f), lambda i, te, nv: (te[i], 0, 0)),
            pl.BlockSpec((1, d_ff, d), lambda i, te, nv: (te[i], 0, 0)),
            pl.BlockSpec((1, 1, d), lambda i, te, nv: (te[i], 0, 0)),
        ],
        out_specs=pl.BlockSpec(memory_space=pl.ANY),
        scratch_shapes=[
            pltpu.VMEM((2, tm * tpr, LANES), F32),
            pltpu.VMEM((2, tm * tpr, LANES), F32),
            pltpu.VMEM((d, two_f), BF16),
            pltpu.VMEM((d_ff, d), BF16),
            pltpu.SemaphoreType.DMA((2,)),
            pltpu.SemaphoreType.DMA((2,)),
        ],
    )
    return pl.pallas_call(
        functools.partial(_moe_kernel, tm=tm),
        grid_spec=grid_spec,
        out_shape=jax.ShapeDtypeStruct(((TOP_K * t_rows + 2 * tm) * tpr, LANES), F32),
        compiler_params=pltpu.CompilerParams(dimension_semantics=("arbitrary",),
                                             vmem_limit_bytes=VMEM_LIMIT),
        name="routed_moe",
    )(tile_expert, n_valid, src_tok, src_tok, dest, h2, wgu,
      bgu.reshape(n_exp, 1, two_f), wd, bd.reshape(n_exp, 1, d))


def _route(idx, n_exp, tm, n_tiles):
    t_rows, top_k = idx.shape
    e_flat = idx.T.reshape(-1)
    order = jnp.argsort(e_flat, stable=True).astype(jnp.int32)
    counts = jnp.sum(e_flat[:, None] == jnp.arange(n_exp, dtype=jnp.int32)[None, :], axis=0,
                     dtype=jnp.int32)
    padded = ((counts + tm - 1) // tm) * tm
    pad_end = jnp.cumsum(padded)
    pad_start = pad_end - padded
    start = jnp.cumsum(counts) - counts
    pos = jnp.arange(n_tiles * tm, dtype=jnp.int32)
    e_pos = jnp.minimum(jnp.sum(pos[:, None] >= pad_end[None, :], axis=1, dtype=jnp.int32),
                        n_exp - 1)
    rank = pos - pad_start[e_pos]
    valid = jnp.logical_and(rank < counts[e_pos], pos < pad_end[-1])
    pair = order[jnp.clip(start[e_pos] + rank, 0, t_rows * top_k - 1)]
    spare = t_rows * top_k + pos % (2 * tm)
    dest = jnp.where(valid, pair, spare)
    src_tok = jnp.where(valid, pair % t_rows, 0)
    n_valid = pad_end[-1] // tm
    tile_first = jnp.minimum(jnp.arange(n_tiles, dtype=jnp.int32), n_valid - 1) * tm
    tile_expert = e_pos[tile_first]
    return (tile_expert, n_valid.reshape(1), src_tok.reshape(n_tiles, 1, tm),
            dest.reshape(n_tiles, 1, tm))


def _combine_kernel(x1_ref, y0_ref, y1_ref, y2_ref, y3_ref, wts_ref, gate_ref, fg_ref, o_ref):
    w = wts_ref[...]
    tc, d = x1_ref.shape
    tpr = d // LANES
    cols = []
    for j in range(tpr):
        acc = w[:, 0:1] * y0_ref[pl.ds(j, tc, stride=tpr), :]
        for k, y_ref in enumerate((y1_ref, y2_ref, y3_ref), start=1):
            acc = acc + w[:, k:k + 1] * y_ref[pl.ds(j, tc, stride=tpr), :]
        cols.append(acc)
    x = x1_ref[...] + gate_ref[0] * jnp.concatenate(cols, axis=1)
    o_ref[...] = x * _rsqrt_mean_sq(x) * fg_ref[...]


def _combine(x1, y, wts, gate2, final_g, *, t_all, row0, rows_per_gate):
    rows, d = x1.shape
    tc = COMBINE_TILE
    y_specs = [
        pl.BlockSpec((tc * (d // LANES), LANES), functools.partial(
            lambda i, k: ((k * t_all + row0) // tc + i, 0), k=k))
        for k in range(TOP_K)
    ]
    return pl.pallas_call(
        _combine_kernel,
        grid=(rows // tc,),
        in_specs=[pl.BlockSpec((tc, d), lambda i: (i, 0))] + y_specs + [
            pl.BlockSpec((tc, LANES), lambda i: (i, 0)),
            pl.BlockSpec((1, 1, d), lambda i: ((i * tc) // rows_per_gate, 0, 0)),
            pl.BlockSpec((1, d), lambda i: (0, 0)),
        ],
        out_specs=pl.BlockSpec((tc, d), lambda i: (i, 0)),
        out_shape=jax.ShapeDtypeStruct((rows, d), F32),
        compiler_params=pltpu.CompilerParams(dimension_semantics=("arbitrary",)),
        name="combine",
    )(x1, y, y, y, y, wts, gate2, final_g)


def _dft_tables(n):
    def angles(m):
        k = np.arange(m, dtype=np.int64)
        return (2.0 * np.pi / m) * ((k[:, None] * k[None, :]) % m)
    an = angles(n)
    ac = angles(LANES)
    dftn = np.concatenate([np.cos(an), -np.sin(an)], axis=1).astype(np.float32)
    fcs = np.concatenate([np.cos(ac), np.sin(ac)], axis=1).astype(np.float32)
    return jnp.asarray(dftn).astype(BF16), jnp.asarray(fcs).astype(BF16)


def _rope_tables(n, qk_dim):
    quarter = qk_dim // 4
    tok = np.arange(n)
    pos = np.stack([tok // GRID_W, tok % GRID_W], axis=-1).astype(np.float64)
    freqs = ROPE_THETA ** (-np.arange(quarter, dtype=np.float64) / quarter)
    ang = (pos[:, :, None] * freqs).reshape(n, 2 * quarter)
    cos, sin = np.cos(ang), np.sin(ang)
    row_c, col_c = cos[:, :quarter], cos[:, quarter:]
    row_s, col_s = sin[:, :quarter], sin[:, quarter:]
    cos_map = np.concatenate([row_c, row_c, col_c, col_c], axis=-1)
    sin_map = np.concatenate([-row_s, row_s, -col_s, col_s], axis=-1)
    reps = LANES // qk_dim
    return (np.tile(cos_map, (1, reps)).astype(np.float32),
            np.tile(sin_map, (1, reps)).astype(np.float32))


def kernel(x_prompt, x_sample, cache_k, cache_v, c, c_ctx, w_mod, b_mod, norm1_g, w_in, lambda_q1,
           lambda_k1, lambda_q2, lambda_k2, subln_g, w_out, norm2_g, router_w, router_b, w_gate_up,
           b_gate_up, w_down, b_down, final_g):
    bsz, seq, d = x_prompt.shape
    dec_b, dec_seq, _ = x_sample.shape
    heads, past, qk_dim = cache_k.shape[2], cache_k.shape[3], cache_k.shape[5]
    n_exp = router_w.shape[-1]
    t_ctx, t_den = bsz * seq, dec_b * dec_seq
    t_all = t_ctx + t_den
    assert t_ctx % dec_seq == 0 and 2 * qk_dim == LANES and dec_seq % GRID_W == 0
    assert (t_all * TOP_K) % MOE_TILE == 0 and t_all % COMBINE_TILE == 0

    cvec = jnp.concatenate([c_ctx[None, :], c, jnp.zeros((8 - 1 - dec_b, d), F32)], axis=0)
    mod = _modulation(cvec, w_mod[0], b_mod[0])[:, None, :]

    win = w_in[0].astype(BF16)
    wout = w_out[0].astype(BF16)
    rw = jnp.pad(router_w[0], ((0, 0), (0, LANES - n_exp)))
    rwh = rw.astype(BF16)
    rwl = (rw - rwh.astype(F32)).astype(BF16)
    rb = jnp.pad(router_b[0], (0, LANES - n_exp)).reshape(1, LANES)
    lamv = jnp.stack([lambda_q1[0], lambda_k1[0], lambda_q2[0], lambda_k2[0]], axis=0)
    g1 = norm1_g[0].reshape(1, d)
    g2 = norm2_g[0].reshape(1, d)
    subg = subln_g[0].reshape(1, LANES)
    dft_ctx, fcs = _dft_tables(seq)
    dft_den, _ = _dft_tables(dec_seq)
    cos, sin = _rope_tables(dec_seq, qk_dim)

    shared = (g1, win, lamv, subg, fcs)
    tail = (wout, g2, rwh, rwl, rb)
    x1_ctx, h2_all, idx_ctx, wts_ctx, new_k, new_v = _layer(
        x_prompt, mod, *shared, dft_ctx, *tail, n_experts=n_exp, mod_row0=0, mod_row_step=0,
        h2_rows=t_all, h2_block0=0, zero_blocks=t_den // seq)
    ck = cache_k[:, 0].reshape(dec_b, heads, past, LANES)
    cv = cache_v[:, 0]
    x1_den, h2_all, idx_den, wts_den = _layer(
        x_sample, mod, *shared, dft_den, *tail, n_experts=n_exp, mod_row0=1, mod_row_step=1,
        h2_rows=t_all, h2_block0=t_ctx // dec_seq, cache=(ck, cv, cos, sin), h2_buf=h2_all)

    idx = jnp.concatenate([idx_ctx[:, :TOP_K], idx_den[:, :TOP_K]], axis=0)
    n_tiles = (t_all * TOP_K) // MOE_TILE + n_exp
    tile_expert, n_valid, src_tok, dest = _route(idx, n_exp, MOE_TILE, n_tiles)
    y = _moe(h2_all, tile_expert, n_valid, src_tok, dest, w_gate_up[0], b_gate_up[0], w_down[0],
             b_down[0])

    gate2 = mod[:, :, 5 * d:]
    fg = final_g.reshape(1, d)
    y_prompt = _combine(x1_ctx, y, wts_ctx, gate2[0:1], fg, t_all=t_all, row0=0,
                        rows_per_gate=t_ctx)
    y_sample = _combine(x1_den, y, wts_den, gate2[1:1 + dec_b], fg, t_all=t_all, row0=t_ctx,
                        rows_per_gate=dec_seq)
    return (y_prompt.reshape(bsz, seq, d), y_sample.reshape(dec_b, dec_seq, d),
            new_k.reshape(bsz, 1, heads, seq, 2, qk_dim), new_v)
```

```python
import functools
import math

import numpy as np
import jax
import jax.numpy as jnp
from jax import lax
from jax.experimental import pallas as pl
from jax.experimental.pallas import tpu as pltpu

F32 = jnp.float32
BF16 = jnp.bfloat16

GRID_W = 64
N_FGROUPS = 4
TOP_K = 4
SWIGLU_LIMIT = 7.0
SWIGLU_ALPHA = 1.702
ROPE_THETA = 10000.0
NORM_EPS = 1e-6
LAMBDA_INIT = 0.8 - 0.6 * math.exp(-0.3 * 0)

LANES = 128
ROW_CHUNK = 256
MOE_TILE = 256
COMBINE_TILE = 256
VMEM_LIMIT = 56 * 1024 * 1024


def _rsqrt_mean_sq(x):
    return lax.rsqrt(jnp.mean(x * x, axis=-1, keepdims=True) + NORM_EPS)


def _sigmoid(z):
    return 1.0 / (1.0 + jnp.exp(-z))


def _mod_kernel(c_ref, w_ref, b_ref, o_ref):
    c = c_ref[...]
    s = c * _sigmoid(c)
    o_ref[...] = jnp.dot(s.astype(BF16), w_ref[...].astype(BF16),
                         preferred_element_type=F32) + b_ref[...]


def _modulation(cvec, w_mod, b_mod):
    rows, d = cvec.shape
    n_out = w_mod.shape[1]
    return pl.pallas_call(
        _mod_kernel,
        grid=(n_out // d,),
        in_specs=[
            pl.BlockSpec((rows, d), lambda j: (0, 0)),
            pl.BlockSpec((d, d), lambda j: (0, j)),
            pl.BlockSpec((1, d), lambda j: (0, j)),
        ],
        out_specs=pl.BlockSpec((rows, d), lambda j: (0, j)),
        out_shape=jax.ShapeDtypeStruct((rows, n_out), F32),
        name="modulation",
    )(cvec, w_mod, b_mod.reshape(1, n_out))


def _loop(n, body):
    if n == 1:
        body(0)
    else:
        def step(i, carry):
            body(i)
            return carry
        lax.fori_loop(0, n, step, 0)


def _layer_kernel(*refs, n_live, n_inputs, **static):
    if n_live is None:
        _layer_body(*refs, **static)
        return
    b = pl.program_id(0)
    h2_ref = refs[n_inputs + 1]
    pl.when(b < n_live)(functools.partial(_layer_body, *refs, **static))

    @pl.when(b >= n_live)
    def _():
        h2_ref[...] = jnp.zeros(h2_ref.shape, F32)


def _layer_body(*refs, n, n_cache, heads, n_experts, rope, emit_kv):
    it = iter(refs)
    x_ref = next(it); mod_ref = next(it); g1_ref = next(it); win_ref = next(it)
    lamv_ref = next(it); subg_ref = next(it); fcs_ref = next(it); dftn_ref = next(it)
    wout_ref = next(it); g2_ref = next(it); rwh_ref = next(it); rwl_ref = next(it); rb_ref = next(it)
    if rope:
        ck_ref = next(it); cv_ref = next(it); cos_ref = next(it); sin_ref = next(it)
        next(it)
    x1_ref = next(it); h2_ref = next(it); idx_ref = next(it); wts_ref = next(it); cnt_ref = next(it)
    if emit_kv:
        newk_ref = next(it); newv_ref = next(it)
    q1_scr = next(it); q2_scr = next(it); kall = next(it); vall = next(it)
    f_scr = next(it); stk = next(it); mix = next(it)

    d = x_ref.shape[-1]
    qk_w = heads * LANES

    @pl.when(pl.program_id(0) == 0)
    def _():
        cnt_ref[...] = jnp.zeros(cnt_ref.shape, F32)
    rc = min(ROW_CHUNK, n)
    n_chunks = n // rc

    def mod_row(j):
        return mod_ref[0, :, j * d:(j + 1) * d]

    shift1, scale1, gate1 = mod_row(0), mod_row(1), mod_row(2)
    shift2, scale2, gate2 = mod_row(3), mod_row(4), mod_row(5)
    del gate2

    lv = lamv_ref[...]
    lam = (jnp.exp(jnp.sum(lv[0:1] * lv[1:2], axis=-1, keepdims=True))
           - jnp.exp(jnp.sum(lv[2:3] * lv[3:4], axis=-1, keepdims=True)) + LAMBDA_INIT)

    if rope:
        for hd in range(heads):
            kall[hd, 0:n_cache, :] = ck_ref[0, hd].astype(BF16)
            vall[hd, 0:n_cache, :] = cv_ref[0, hd].astype(BF16)

    lane = lax.broadcasted_iota(jnp.int32, (rc, LANES), 1)
    first_map = lane < (LANES // 2)

    first_of_pair = jnp.bitwise_and(lane, 31) < 16

    def rotate(t, cos, sin):
        partner = jnp.where(first_of_pair, pltpu.roll(t, LANES - 16, 1), pltpu.roll(t, 16, 1))
        return t * cos + partner * sin

    def project(c):
        r0 = pl.multiple_of(c * rc, rc)
        x = x_ref[0, pl.ds(r0, rc), :]
        h = (x * _rsqrt_mean_sq(x) * g1_ref[...]) * (1.0 + scale1) + shift1
        p = jnp.dot(h.astype(BF16), win_ref[...], preferred_element_type=F32)
        if rope:
            cos = cos_ref[pl.ds(r0, rc), :]
            sin = sin_ref[pl.ds(r0, rc), :]
        for hd in range(heads):
            qh = p[:, hd * LANES:(hd + 1) * LANES]
            kh = p[:, qk_w + hd * LANES:qk_w + (hd + 1) * LANES]
            vh = p[:, 2 * qk_w + hd * LANES:2 * qk_w + (hd + 1) * LANES]
            if rope:
                qh = rotate(qh, cos, sin)
                kh = rotate(kh, cos, sin)
            if emit_kv:
                newk_ref[0, 0, hd, pl.ds(r0, rc), :] = kh
                newv_ref[0, 0, hd, pl.ds(r0, rc), :] = vh
            qs = qh * (LANES // 2) ** -0.5
            q1_scr[pl.ds(r0, rc), hd * LANES:(hd + 1) * LANES] = jnp.where(first_map, qs, 0.0).astype(BF16)
            q2_scr[pl.ds(r0, rc), hd * LANES:(hd + 1) * LANES] = jnp.where(first_map, 0.0, qs).astype(BF16)
            kall[hd, pl.ds(n_cache + r0, rc), :] = kh.astype(BF16)
            vall[hd, pl.ds(n_cache + r0, rc), :] = vh.astype(BF16)
        f_scr[pl.ds(r0, rc), :] = p[:, 3 * qk_w:].astype(BF16)

    _loop(n_chunks, project)

    contract_last = (((1,), (1,)), ((), ()))

    def softmax(s):
        e = jnp.exp(s - jnp.max(s, axis=-1, keepdims=True))
        return e * (1.0 / jnp.sum(e, axis=-1, keepdims=True))

    for hd in range(heads):
        def attend(c, hd=hd):
            r0 = pl.multiple_of(c * rc, rc)
            kh = kall[hd]
            s1 = lax.dot_general(q1_scr[pl.ds(r0, rc), hd * LANES:(hd + 1) * LANES], kh,
                                 contract_last, preferred_element_type=F32)
            s2 = lax.dot_general(q2_scr[pl.ds(r0, rc), hd * LANES:(hd + 1) * LANES], kh,
                                 contract_last, preferred_element_type=F32)
            a = softmax(s1) - lam * softmax(s2)
            o = jnp.dot(a.astype(BF16), vall[hd], preferred_element_type=F32)
            o = o * _rsqrt_mean_sq(o) * subg_ref[...] * (1.0 - LAMBDA_INIT)
            mix[pl.ds(r0, rc), hd * LANES:(hd + 1) * LANES] = o.astype(BF16)

        _loop(n_chunks, attend)

    def dft_channels(c):
        r0 = pl.multiple_of(c * rc, rc)
        for g in range(N_FGROUPS):
            a = jnp.dot(f_scr[pl.ds(r0, rc), g * LANES:(g + 1) * LANES], fcs_ref[...],
                        preferred_element_type=F32)
            stk[pl.ds(r0, rc), g * LANES:(g + 1) * LANES] = a[:, :LANES].astype(BF16)
            stk[pl.ds(pl.multiple_of(n + r0, rc), rc), g * LANES:(g + 1) * LANES] = (
                a[:, LANES:].astype(BF16))

    _loop(n_chunks, dft_channels)

    fscale = 1.0 / math.sqrt(n * LANES)

    def dft_positions(c):
        r0 = pl.multiple_of(c * rc, rc)
        y = jnp.dot(dftn_ref[pl.ds(r0, rc), :], stk[...], preferred_element_type=F32) * fscale
        mix[pl.ds(r0, rc), qk_w:] = y.astype(BF16)

    _loop(n_chunks, dft_positions)

    klane = lax.broadcasted_iota(jnp.int32, (rc, LANES), 1)
    neg_inf = jnp.float32(-jnp.inf)

    def tail(c):
        r0 = pl.multiple_of(c * rc, rc)
        x = x_ref[0, pl.ds(r0, rc), :]
        mixed = jnp.dot(mix[pl.ds(r0, rc), :], wout_ref[...], preferred_element_type=F32)
        x1 = x + gate1 * mixed
        x1_ref[pl.ds(r0, rc), :] = x1
        h2 = (x1 * _rsqrt_mean_sq(x1) * g2_ref[...]) * (1.0 + scale2) + shift2
        tpr = d // LANES
        for j in range(tpr):
            h2_ref[pl.ds(r0 * tpr + j, rc, stride=tpr), :] = h2[:, j * LANES:(j + 1) * LANES]
        hi = h2.astype(BF16)
        lo = (h2 - hi.astype(F32)).astype(BF16)
        logits = (jnp.dot(hi, rwh_ref[...], preferred_element_type=F32)
                  + jnp.dot(lo, rwh_ref[...], preferred_element_type=F32)
                  + jnp.dot(hi, rwl_ref[...], preferred_element_type=F32))
        l = jnp.where(klane < n_experts, logits + rb_ref[...], neg_inf)
        vals, ids = [], []
        for _ in range(TOP_K):
            m = jnp.max(l, axis=-1, keepdims=True)
            cand = jnp.where(l == m, klane, LANES).astype(F32)
            i = jnp.min(cand, axis=-1, keepdims=True).astype(jnp.int32)
            vals.append(m)
            ids.append(i)
            l = jnp.where(klane == i, neg_inf, l)
        es = [jnp.exp(v - vals[0]) for v in vals]
        inv = 1.0 / functools.reduce(lambda a, b: a + b, es)
        idx_out = jnp.zeros((rc, LANES), jnp.int32)
        wts_out = jnp.zeros((rc, LANES), F32)
        for k in range(TOP_K):
            idx_out = jnp.where(klane == k, ids[k], idx_out)
            wts_out = jnp.where(klane == k, es[k] * inv, wts_out)
        idx_ref[pl.ds(r0, rc), :] = idx_out
        wts_ref[pl.ds(r0, rc), :] = wts_out
        hits = functools.reduce(lambda a, b: a + b,
                                [jnp.where(klane == i, 1.0, 0.0) for i in ids])
        cnt_ref[...] += jnp.sum(hits, axis=0, keepdims=True)

    _loop(n_chunks, tail)


def _const_spec(shape):
    nd = len(shape)
    return pl.BlockSpec(shape, lambda b: (0,) * nd, pipeline_mode=pl.Buffered(1))


def _layer(x, mod, g1, win, lamv, subg, fcs, dftn, wout, g2, rwh, rwl, rb, *, n_experts,
           mod_row0, mod_row_step, h2_rows, h2_block0, zero_blocks=0, cache=None, h2_buf=None):
    bsz, n, d = x.shape
    heads = win.shape[1] // (4 * LANES)
    rope = cache is not None
    n_cache = cache[0].shape[2] if rope else 0
    nk = n_cache + n
    live = lambda b: jnp.minimum(b, bsz - 1)
    in_specs = [
        pl.BlockSpec((1, n, d), lambda b: (live(b), 0, 0)),
        pl.BlockSpec((1, 1, mod.shape[-1]),
                     lambda b: (mod_row0 + mod_row_step * live(b), 0, 0)),
        _const_spec(g1.shape), _const_spec(win.shape), _const_spec(lamv.shape),
        _const_spec(subg.shape), _const_spec(fcs.shape), _const_spec(dftn.shape),
        _const_spec(wout.shape), _const_spec(g2.shape), _const_spec(rwh.shape),
        _const_spec(rwl.shape), _const_spec(rb.shape),
    ]
    args = [x, mod, g1, win, lamv, subg, fcs, dftn, wout, g2, rwh, rwl, rb]
    out_specs = [
        pl.BlockSpec((n, d), lambda b: (live(b), 0)),
        pl.BlockSpec((n * (d // LANES), LANES), lambda b: (h2_block0 + b, 0)),
        pl.BlockSpec((n, LANES), lambda b: (live(b), 0)),
        pl.BlockSpec((n, LANES), lambda b: (live(b), 0)),
        pl.BlockSpec((1, LANES), lambda b: (0, 0)),
    ]
    out_shape = [
        jax.ShapeDtypeStruct((bsz * n, d), F32),
        jax.ShapeDtypeStruct((h2_rows * (d // LANES), LANES), F32),
        jax.ShapeDtypeStruct((bsz * n, LANES), jnp.int32),
        jax.ShapeDtypeStruct((bsz * n, LANES), F32),
        jax.ShapeDtypeStruct((1, LANES), F32),
    ]
    aliases = {}
    if rope:
        ck, cv, cos, sin = cache
        in_specs += [
            pl.BlockSpec((1, heads, n_cache, LANES), lambda b: (live(b), 0, 0, 0)),
            pl.BlockSpec((1, heads, n_cache, LANES), lambda b: (live(b), 0, 0, 0)),
            _const_spec(cos.shape), _const_spec(sin.shape),
            pl.BlockSpec(memory_space=pl.ANY),
        ]
        args += [ck, cv, cos, sin, h2_buf]
        aliases = {len(args) - 1: 1}
    else:
        kv_spec = pl.BlockSpec((1, 1, heads, n, LANES), lambda b: (live(b), 0, 0, 0, 0))
        out_specs += [kv_spec, kv_spec]
        kv_shape = jax.ShapeDtypeStruct((bsz, 1, heads, n, LANES), F32)
        out_shape += [kv_shape, kv_shape]
    scratch = [
        pltpu.VMEM((n, heads * LANES), BF16),
        pltpu.VMEM((n, heads * LANES), BF16),
        pltpu.VMEM((heads, nk, LANES), BF16),
        pltpu.VMEM((heads, nk, LANES), BF16),
        pltpu.VMEM((n, N_FGROUPS * LANES), BF16),
        pltpu.VMEM((2 * n, N_FGROUPS * LANES), BF16),
        pltpu.VMEM((n, d), BF16),
    ]
    kern = functools.partial(_layer_kernel, n_live=bsz if zero_blocks else None,
                             n_inputs=len(args), n=n, n_cache=n_cache, heads=heads,
                             n_experts=n_experts, rope=rope, emit_kv=not rope)
    return pl.pallas_call(
        kern,
        grid=(bsz + zero_blocks,),
        in_specs=in_specs,
        out_specs=out_specs,
        out_shape=out_shape,
        scratch_shapes=scratch,
        input_output_aliases=aliases,
        compiler_params=pltpu.CompilerParams(dimension_semantics=("arbitrary",),
                                             vmem_limit_bytes=VMEM_LIMIT),
        name="layer_latent" if rope else "layer_context",
    )(*args)


def _moe_kernel(texp_ref, next_ref, nvalid_ref, tok_ref, tok_next_ref, dst_ref, h2_hbm, wgu_hbm,
                bgu_ref, wd_hbm, bd_ref, y_hbm, xbuf, ybuf, wgu_f32, wd_f32, wgu_bf, wd_bf, gsem,
                ssem, wsem, run_ref, *, tm):
    i = pl.program_id(0)
    nv = nvalid_ref[0]
    d_ff, d = wd_bf.shape
    tpr = d // LANES

    def weight_copies(e, s):
        return (pltpu.make_async_copy(wgu_hbm.at[e], wgu_f32.at[s], wsem.at[s, 0]),
                pltpu.make_async_copy(wd_hbm.at[e], wd_f32.at[s], wsem.at[s, 1]))

    def token_rows(t):
        return pl.ds(pl.multiple_of(t * tpr, tpr), tpr)

    def gather_row(idx_ref, slot, r):
        return pltpu.make_async_copy(h2_hbm.at[token_rows(idx_ref[0, 0, r]), :],
                                     xbuf.at[slot, pl.ds(r * tpr, tpr), :], gsem.at[slot])

    def scatter_row(slot, r):
        return pltpu.make_async_copy(ybuf.at[slot, pl.ds(r * tpr, tpr), :],
                                     y_hbm.at[token_rows(dst_ref[0, 0, r]), :], ssem.at[slot])

    @pl.when(i < nv)
    def _():
        slot = i % 2

        @pl.when(i == 0)
        def _():
            run_ref[0] = 0
            for c in weight_copies(texp_ref[0], 0):
                c.start(priority=1)
            for r in range(tm):
                gather_row(tok_ref, 0, r).start()
            ybuf[...] = jnp.zeros(ybuf.shape, F32)
            spare0 = y_hbm.shape[0] - 2 * tm * tpr
            fills = [pltpu.make_async_copy(
                ybuf.at[s], y_hbm.at[pl.ds(spare0 + s * tm * tpr, tm * tpr), :], ssem.at[s])
                for s in range(2)]
            for f in fills:
                f.start()
            for f in fills:
                f.wait()

        @pl.when(i + 1 < nv)
        def _():
            for r in range(tm):
                gather_row(tok_next_ref, 1 - slot, r).start()

        @pl.when(jnp.logical_or(i == 0, texp_ref[i] != texp_ref[jnp.maximum(i - 1, 0)]))
        def _():
            run = run_ref[0]
            ws = run % 2
            for c in weight_copies(texp_ref[i], ws):
                c.wait()

            @pl.when(next_ref[i] >= 0)
            def _():
                for c in weight_copies(next_ref[i], 1 - ws):
                    c.start(priority=1)

            wgu_bf[...] = wgu_f32[ws].astype(BF16)
            wd_bf[...] = wd_f32[ws].astype(BF16)
            run_ref[0] = run + 1

        for r in range(tm):
            gather_row(tok_ref, slot, r).wait()

        @pl.when(i >= 2)
        def _():
            for r in range(tm):
                scatter_row(slot, r).wait()

        x = jnp.concatenate(
            [xbuf[slot, pl.ds(j, tm, stride=tpr), :].astype(BF16) for j in range(tpr)], axis=1)
        gu = jnp.dot(x, wgu_bf[...], preferred_element_type=F32) + bgu_ref[0]
        glu = jnp.minimum(gu[:, :d_ff], SWIGLU_LIMIT)
        lin = jnp.clip(gu[:, d_ff:], -SWIGLU_LIMIT, SWIGLU_LIMIT)
        act = glu * _sigmoid(SWIGLU_ALPHA * glu) * (lin + 1.0)
        y = jnp.dot(act.astype(BF16), wd_bf[...], preferred_element_type=F32) + bd_ref[0]
        for j in range(tpr):
            ybuf[slot, pl.ds(j, tm, stride=tpr), :] = y[:, j * LANES:(j + 1) * LANES]

        for r in range(tm):
            scatter_row(slot, r).start()

        @pl.when(i == nv - 1)
        def _():
            for r in range(tm):
                scatter_row(slot, r).wait()

            @pl.when(i >= 1)
            def _():
                for r in range(tm):
                    scatter_row(1 - slot, r).wait()


def _moe(h2, tile_expert, next_expert, n_valid, src_tok, dest, wgu, bgu, wd, bd):
    n_exp, d, two_f = wgu.shape
    tpr = d // LANES
    t_rows = h2.shape[0] // tpr
    d_ff = two_f // 2
    n_tiles = tile_expert.shape[0]
    tm = src_tok.shape[-1]
    last = n_tiles - 1
    smem_tile = functools.partial(pl.BlockSpec, (1, 1, tm), memory_space=pltpu.SMEM)
    grid_spec = pltpu.PrefetchScalarGridSpec(
        num_scalar_prefetch=3,
        grid=(n_tiles,),
        in_specs=[
            smem_tile(lambda i, te, nx, nv: (i, 0, 0)),
            smem_tile(lambda i, te, nx, nv: (jnp.minimum(i + 1, last), 0, 0)),
            smem_tile(lambda i, te, nx, nv: (i, 0, 0)),
            pl.BlockSpec(memory_space=pl.ANY),
            pl.BlockSpec(memory_space=pl.ANY),
            pl.BlockSpec((1, 1, two_f), lambda i, te, nx, nv: (te[i], 0, 0)),
            pl.BlockSpec(memory_space=pl.ANY),
            pl.BlockSpec((1, 1, d), lambda i, te, nx, nv: (te[i], 0, 0)),
        ],
        out_specs=pl.BlockSpec(memory_space=pl.ANY),
        scratch_shapes=[
            pltpu.VMEM((2, tm * tpr, LANES), F32),
            pltpu.VMEM((2, tm * tpr, LANES), F32),
            pltpu.VMEM((2, d, two_f), F32),
            pltpu.VMEM((2, d_ff, d), F32),
            pltpu.VMEM((d, two_f), BF16),
            pltpu.VMEM((d_ff, d), BF16),
            pltpu.SemaphoreType.DMA((2,)),
            pltpu.SemaphoreType.DMA((2,)),
            pltpu.SemaphoreType.DMA((2, 2)),
            pltpu.SMEM((1,), jnp.int32),
        ],
    )
    return pl.pallas_call(
        functools.partial(_moe_kernel, tm=tm),
        grid_spec=grid_spec,
        out_shape=jax.ShapeDtypeStruct(((TOP_K * t_rows + 2 * tm) * tpr, LANES), F32),
        compiler_params=pltpu.CompilerParams(dimension_semantics=("arbitrary",),
                                             vmem_limit_bytes=VMEM_LIMIT),
        name="routed_moe",
    )(tile_expert, next_expert, n_valid, src_tok, src_tok, dest, h2, wgu,
      bgu.reshape(n_exp, 1, two_f), wd, bd.reshape(n_exp, 1, d))


def _route(idx, counts, tm, n_tiles):
    t_rows, top_k = idx.shape
    n_exp = counts.shape[0]
    n_pairs = t_rows * top_k
    pad_bit = 16
    assert n_pairs <= 1 << pad_bit and tm <= 1 << pad_bit and n_tiles * tm == n_pairs + n_exp * tm
    pair_ids = np.arange(n_pairs, dtype=np.int32)
    real_keys = (idx.T.reshape(-1) << (pad_bit + 1)) | pair_ids
    pad_e = np.repeat(np.arange(n_exp, dtype=np.int32), tm)
    pad_j = np.tile(np.arange(tm, dtype=np.int32), n_exp)
    n_pad = (-counts) % tm
    unused = n_exp << (pad_bit + 1)
    pad_keys = jnp.where(pad_j < jnp.repeat(n_pad, tm),
                         (pad_e << (pad_bit + 1)) | (1 << pad_bit) | pad_j, unused)
    keys = jnp.sort(jnp.concatenate([real_keys, pad_keys]))
    is_real = jnp.logical_and((keys >> pad_bit) & 1 == 0, keys < unused)
    pair = keys & ((1 << pad_bit) - 1)
    slot = sum((pair >= k * t_rows).astype(jnp.int32) for k in range(1, top_k))
    pos = np.arange(n_tiles * tm, dtype=np.int32)
    spare = n_pairs + pos % (2 * tm)
    dest = jnp.where(is_real, pair, spare)
    src_tok = jnp.where(is_real, pair - slot * t_rows, 0)
    n_valid = jnp.sum(counts + n_pad) // tm
    last_e = jnp.max(jnp.where(counts > 0, jnp.arange(n_exp, dtype=jnp.int32), 0))
    tile_expert = jnp.where(np.arange(n_tiles) < n_valid, keys[::tm] >> (pad_bit + 1), last_e)
    experts = jnp.arange(n_exp, dtype=jnp.int32)
    owner = jnp.where(counts > 0, experts, n_exp)
    following = jnp.concatenate([lax.cummin(owner, reverse=True)[1:],
                                 jnp.full((1,), n_exp, jnp.int32)])
    following = jnp.where(following >= n_exp, -1, following)
    next_expert = jnp.sum(jnp.where(tile_expert[:, None] == experts[None, :], following[None, :], 0),
                          axis=1)
    return (tile_expert.astype(jnp.int32), next_expert.astype(jnp.int32),
            n_valid.reshape(1).astype(jnp.int32), src_tok.reshape(n_tiles, 1, tm),
            dest.reshape(n_tiles, 1, tm))


def _combine_kernel(x1_ref, y0_ref, y1_ref, y2_ref, y3_ref, wts_ref, gate_ref, fg_ref, o_ref):
    w = wts_ref[...]
    tc, d = x1_ref.shape
    tpr = d // LANES
    cols = []
    for j in range(tpr):
        acc = w[:, 0:1] * y0_ref[pl.ds(j, tc, stride=tpr), :]
        for k, y_ref in enumerate((y1_ref, y2_ref, y3_ref), start=1):
            acc = acc + w[:, k:k + 1] * y_ref[pl.ds(j, tc, stride=tpr), :]
        cols.append(acc)
    x = x1_ref[...] + gate_ref[0] * jnp.concatenate(cols, axis=1)
    o_ref[...] = x * _rsqrt_mean_sq(x) * fg_ref[...]


def _combine(x1, y, wts, gate2, final_g, *, t_all, row0, rows_per_gate):
    rows, d = x1.shape
    tc = COMBINE_TILE
    y_specs = [
        pl.BlockSpec((tc * (d // LANES), LANES), functools.partial(
            lambda i, k: ((k * t_all + row0) // tc + i, 0), k=k))
        for k in range(TOP_K)
    ]
    return pl.pallas_call(
        _combine_kernel,
        grid=(rows // tc,),
        in_specs=[pl.BlockSpec((tc, d), lambda i: (i, 0))] + y_specs + [
            pl.BlockSpec((tc, LANES), lambda i: (i, 0)),
            pl.BlockSpec((1, 1, d), lambda i: ((i * tc) // rows_per_gate, 0, 0)),
            pl.BlockSpec((1, d), lambda i: (0, 0)),
        ],
        out_specs=pl.BlockSpec((tc, d), lambda i: (i, 0)),
        out_shape=jax.ShapeDtypeStruct((rows, d), F32),
        compiler_params=pltpu.CompilerParams(dimension_semantics=("arbitrary",)),
        name="combine",
    )(x1, y, y, y, y, wts, gate2, final_g)


def _dft_tables(n):
    def angles(m):
        k = np.arange(m, dtype=np.int64)
        return (2.0 * np.pi / m) * ((k[:, None] * k[None, :]) % m)
    an = angles(n)
    ac = angles(LANES)
    dftn = np.concatenate([np.cos(an), -np.sin(an)], axis=1).astype(np.float32)
    fcs = np.concatenate([np.cos(ac), np.sin(ac)], axis=1).astype(np.float32)
    return jnp.asarray(dftn).astype(BF16), jnp.asarray(fcs).astype(BF16)


def _rope_tables(n, qk_dim):
    quarter = qk_dim // 4
    tok = np.arange(n)
    pos = np.stack([tok // GRID_W, tok % GRID_W], axis=-1).astype(np.float64)
    freqs = ROPE_THETA ** (-np.arange(quarter, dtype=np.float64) / quarter)
    ang = (pos[:, :, None] * freqs).reshape(n, 2 * quarter)
    cos, sin = np.cos(ang), np.sin(ang)
    row_c, col_c = cos[:, :quarter], cos[:, quarter:]
    row_s, col_s = sin[:, :quarter], sin[:, quarter:]
    cos_map = np.concatenate([row_c, row_c, col_c, col_c], axis=-1)
    sin_map = np.concatenate([-row_s, row_s, -col_s, col_s], axis=-1)
    reps = LANES // qk_dim
    return (np.tile(cos_map, (1, reps)).astype(np.float32),
            np.tile(sin_map, (1, reps)).astype(np.float32))


def kernel(x_prompt, x_sample, cache_k, cache_v, c, c_ctx, w_mod, b_mod, norm1_g, w_in, lambda_q1,
           lambda_k1, lambda_q2, lambda_k2, subln_g, w_out, norm2_g, router_w, router_b, w_gate_up,
           b_gate_up, w_down, b_down, final_g):
    bsz, seq, d = x_prompt.shape
    dec_b, dec_seq, _ = x_sample.shape
    heads, past, qk_dim = cache_k.shape[2], cache_k.shape[3], cache_k.shape[5]
    n_exp = router_w.shape[-1]
    t_ctx, t_den = bsz * seq, dec_b * dec_seq
    t_all = t_ctx + t_den
    assert t_ctx % dec_seq == 0 and 2 * qk_dim == LANES and dec_seq % GRID_W == 0
    assert (t_all * TOP_K) % MOE_TILE == 0 and t_all % COMBINE_TILE == 0

    cvec = jnp.concatenate([c_ctx[None, :], c, jnp.zeros((8 - 1 - dec_b, d), F32)], axis=0)
    mod = _modulation(cvec, w_mod[0], b_mod[0])[:, None, :]

    win = w_in[0].astype(BF16)
    wout = w_out[0].astype(BF16)
    rw = jnp.pad(router_w[0], ((0, 0), (0, LANES - n_exp)))
    rwh = rw.astype(BF16)
    rwl = (rw - rwh.astype(F32)).astype(BF16)
    rb = jnp.pad(router_b[0], (0, LANES - n_exp)).reshape(1, LANES)
    lamv = jnp.stack([lambda_q1[0], lambda_k1[0], lambda_q2[0], lambda_k2[0]], axis=0)
    g1 = norm1_g[0].reshape(1, d)
    g2 = norm2_g[0].reshape(1, d)
    subg = subln_g[0].reshape(1, LANES)
    dft_ctx, fcs = _dft_tables(seq)
    dft_den, _ = _dft_tables(dec_seq)
    cos, sin = _rope_tables(dec_seq, qk_dim)

    shared = (g1, win, lamv, subg, fcs)
    tail = (wout, g2, rwh, rwl, rb)
    x1_ctx, h2_all, idx_ctx, wts_ctx, cnt_ctx, new_k, new_v = _layer(
        x_prompt, mod, *shared, dft_ctx, *tail, n_experts=n_exp, mod_row0=0, mod_row_step=0,
        h2_rows=t_all, h2_block0=0, zero_blocks=t_den // seq)
    ck = cache_k[:, 0].reshape(dec_b, heads, past, LANES)
    cv = cache_v[:, 0]
    x1_den, h2_all, idx_den, wts_den, cnt_den = _layer(
        x_sample, mod, *shared, dft_den, *tail, n_experts=n_exp, mod_row0=1, mod_row_step=1,
        h2_rows=t_all, h2_block0=t_ctx // dec_seq, cache=(ck, cv, cos, sin), h2_buf=h2_all)

    idx = jnp.concatenate([idx_ctx[:, :TOP_K], idx_den[:, :TOP_K]], axis=0)
    n_tiles = (t_all * TOP_K) // MOE_TILE + n_exp
    counts = (cnt_ctx + cnt_den)[0, :n_exp].astype(jnp.int32)
    tile_expert, next_expert, n_valid, src_tok, dest = _route(idx, counts, MOE_TILE, n_tiles)
    y = _moe(h2_all, tile_expert, next_expert, n_valid, src_tok, dest, w_gate_up[0], b_gate_up[0],
             w_down[0], b_down[0])

    gate2 = mod[:, :, 5 * d:]
    fg = final_g.reshape(1, d)
    y_prompt = _combine(x1_ctx, y, wts_ctx, gate2[0:1], fg, t_all=t_all, row0=0,
                        rows_per_gate=t_ctx)
    y_sample = _combine(x1_den, y, wts_den, gate2[1:1 + dec_b], fg, t_all=t_all, row0=t_ctx,
                        rows_per_gate=dec_seq)
    return (y_prompt.reshape(bsz, seq, d), y_sample.reshape(dec_b, dec_seq, d),
            new_k.reshape(bsz, 1, heads, seq, 2, qk_dim), new_v)
```

```python
import functools
import math

import numpy as np
import jax
import jax.numpy as jnp
from jax import lax
from jax.experimental import pallas as pl
from jax.experimental.pallas import tpu as pltpu

F32 = jnp.float32
BF16 = jnp.bfloat16

GRID_W = 64
N_FGROUPS = 4
TOP_K = 4
SWIGLU_LIMIT = 7.0
SWIGLU_ALPHA = 1.702
ROPE_THETA = 10000.0
NORM_EPS = 1e-6
LAMBDA_INIT = 0.8 - 0.6 * math.exp(-0.3 * 0)

LANES = 128
ROW_CHUNK = 256
MOE_TILE = 256
COMBINE_TILE = 256
VMEM_LIMIT = 56 * 1024 * 1024


def _rsqrt_mean_sq(x):
    return lax.rsqrt(jnp.mean(x * x, axis=-1, keepdims=True) + NORM_EPS)


def _sigmoid(z):
    return 1.0 / (1.0 + jnp.exp(-z))


def _mod_kernel(c_ref, w_ref, b_ref, o_ref):
    c = c_ref[...]
    s = c * _sigmoid(c)
    o_ref[...] = jnp.dot(s.astype(BF16), w_ref[...].astype(BF16),
                         preferred_element_type=F32) + b_ref[...]


def _modulation(cvec, w_mod, b_mod):
    rows, d = cvec.shape
    n_out = w_mod.shape[1]
    return pl.pallas_call(
        _mod_kernel,
        grid=(n_out // d,),
        in_specs=[
            pl.BlockSpec((rows, d), lambda j: (0, 0)),
            pl.BlockSpec((d, d), lambda j: (0, j)),
            pl.BlockSpec((1, d), lambda j: (0, j)),
        ],
        out_specs=pl.BlockSpec((rows, d), lambda j: (0, j)),
        out_shape=jax.ShapeDtypeStruct((rows, n_out), F32),
        name="modulation",
    )(cvec, w_mod, b_mod.reshape(1, n_out))


def _loop(n, body):
    if n == 1:
        body(0)
    else:
        def step(i, carry):
            body(i)
            return carry
        lax.fori_loop(0, n, step, 0)


def _layer_kernel(*refs, n_live, n_inputs, **static):
    if n_live is None:
        _layer_body(*refs, **static)
        return
    b = pl.program_id(0)
    h2_ref = refs[n_inputs + 1]
    pl.when(b < n_live)(functools.partial(_layer_body, *refs, **static))

    @pl.when(b >= n_live)
    def _():
        h2_ref[...] = jnp.zeros(h2_ref.shape, F32)


def _layer_body(*refs, n, n_cache, heads, n_experts, rope, emit_kv):
    it = iter(refs)
    x_ref = next(it); mod_ref = next(it); g1_ref = next(it); win_ref = next(it)
    lamv_ref = next(it); subg_ref = next(it); fcs_ref = next(it); dftn_ref = next(it)
    wout_ref = next(it); g2_ref = next(it); rwh_ref = next(it); rwl_ref = next(it); rb_ref = next(it)
    if rope:
        ck_ref = next(it); cv_ref = next(it); cos_ref = next(it); sin_ref = next(it)
        next(it)
    x1_ref = next(it); h2_ref = next(it); idx_ref = next(it); wts_ref = next(it); cnt_ref = next(it)
    if emit_kv:
        newk_ref = next(it); newv_ref = next(it)
    q1_scr = next(it); q2_scr = next(it); kall = next(it); vall = next(it)
    f_scr = next(it); stk = next(it); mix = next(it)

    d = x_ref.shape[-1]
    qk_w = heads * LANES

    @pl.when(pl.program_id(0) == 0)
    def _():
        cnt_ref[...] = jnp.zeros(cnt_ref.shape, F32)
    rc = min(ROW_CHUNK, n)
    n_chunks = n // rc

    def mod_row(j):
        return mod_ref[0, :, j * d:(j + 1) * d]

    shift1, scale1, gate1 = mod_row(0), mod_row(1), mod_row(2)
    shift2, scale2, gate2 = mod_row(3), mod_row(4), mod_row(5)
    del gate2

    lv = lamv_ref[...]
    lam = (jnp.exp(jnp.sum(lv[0:1] * lv[1:2], axis=-1, keepdims=True))
           - jnp.exp(jnp.sum(lv[2:3] * lv[3:4], axis=-1, keepdims=True)) + LAMBDA_INIT)

    if rope:
        for hd in range(heads):
            kall[hd, 0:n_cache, :] = ck_ref[0, hd].astype(BF16)
            vall[hd, 0:n_cache, :] = cv_ref[0, hd].astype(BF16)

    lane = lax.broadcasted_iota(jnp.int32, (rc, LANES), 1)
    first_map = lane < (LANES // 2)

    first_of_pair = jnp.bitwise_and(lane, 31) < 16

    def rotate(t, cos, sin):
        partner = jnp.where(first_of_pair, pltpu.roll(t, LANES - 16, 1), pltpu.roll(t, 16, 1))
        return t * cos + partner * sin

    def project(c):
        r0 = pl.multiple_of(c * rc, rc)
        x = x_ref[0, pl.ds(r0, rc), :]
        h = (x * _rsqrt_mean_sq(x) * g1_ref[...]) * (1.0 + scale1) + shift1
        p = jnp.dot(h.astype(BF16), win_ref[...], preferred_element_type=F32)
        if rope:
            cos = cos_ref[pl.ds(r0, rc), :]
            sin = sin_ref[pl.ds(r0, rc), :]
        for hd in range(heads):
            qh = p[:, hd * LANES:(hd + 1) * LANES]
            kh = p[:, qk_w + hd * LANES:qk_w + (hd + 1) * LANES]
            vh = p[:, 2 * qk_w + hd * LANES:2 * qk_w + (hd + 1) * LANES]
            if rope:
                qh = rotate(qh, cos, sin)
                kh = rotate(kh, cos, sin)
            if emit_kv:
                newk_ref[0, 0, hd, pl.ds(r0, rc), :] = kh
                newv_ref[0, 0, hd, pl.ds(r0, rc), :] = vh
            qs = qh * (LANES // 2) ** -0.5
            q1_scr[pl.ds(r0, rc), hd * LANES:(hd + 1) * LANES] = jnp.where(first_map, qs, 0.0).astype(BF16)
            q2_scr[pl.ds(r0, rc), hd * LANES:(hd + 1) * LANES] = jnp.where(first_map, 0.0, qs).astype(BF16)
            kall[hd, pl.ds(n_cache + r0, rc), :] = kh.astype(BF16)
            vall[hd, pl.ds(n_cache + r0, rc), :] = vh.astype(BF16)
        f_scr[pl.ds(r0, rc), :] = p[:, 3 * qk_w:].astype(BF16)

    _loop(n_chunks, project)

    contract_last = (((1,), (1,)), ((), ()))

    def softmax(s):
        e = jnp.exp(s - jnp.max(s, axis=-1, keepdims=True))
        return e * (1.0 / jnp.sum(e, axis=-1, keepdims=True))

    for hd in range(heads):
        def attend(c, hd=hd):
            r0 = pl.multiple_of(c * rc, rc)
            kh = kall[hd]
            s1 = lax.dot_general(q1_scr[pl.ds(r0, rc), hd * LANES:(hd + 1) * LANES], kh,
                                 contract_last, preferred_element_type=F32)
            s2 = lax.dot_general(q2_scr[pl.ds(r0, rc), hd * LANES:(hd + 1) * LANES], kh,
                                 contract_last, preferred_element_type=F32)
            a = softmax(s1) - lam * softmax(s2)
            o = jnp.dot(a.astype(BF16), vall[hd], preferred_element_type=F32)
            o = o * _rsqrt_mean_sq(o) * subg_ref[...] * (1.0 - LAMBDA_INIT)
            mix[pl.ds(r0, rc), hd * LANES:(hd + 1) * LANES] = o.astype(BF16)

        _loop(n_chunks, attend)

    def dft_channels(c):
        r0 = pl.multiple_of(c * rc, rc)
        for g in range(N_FGROUPS):
            a = jnp.dot(f_scr[pl.ds(r0, rc), g * LANES:(g + 1) * LANES], fcs_ref[...],
                        preferred_element_type=F32)
            stk[pl.ds(r0, rc), g * LANES:(g + 1) * LANES] = a[:, :LANES].astype(BF16)
            stk[pl.ds(pl.multiple_of(n + r0, rc), rc), g * LANES:(g + 1) * LANES] = (
                a[:, LANES:].astype(BF16))

    _loop(n_chunks, dft_channels)

    fscale = 1.0 / math.sqrt(n * LANES)

    def dft_positions(c):
        r0 = pl.multiple_of(c * rc, rc)
        y = jnp.dot(dftn_ref[pl.ds(r0, rc), :], stk[...], preferred_element_type=F32) * fscale
        mix[pl.ds(r0, rc), qk_w:] = y.astype(BF16)

    _loop(n_chunks, dft_positions)

    klane = lax.broadcasted_iota(jnp.int32, (rc, LANES), 1)
    neg_inf = jnp.float32(-jnp.inf)

    def tail(c):
        r0 = pl.multiple_of(c * rc, rc)
        x = x_ref[0, pl.ds(r0, rc), :]
        mixed = jnp.dot(mix[pl.ds(r0, rc), :], wout_ref[...], preferred_element_type=F32)
        x1 = x + gate1 * mixed
        x1_ref[pl.ds(r0, rc), :] = x1
        h2 = (x1 * _rsqrt_mean_sq(x1) * g2_ref[...]) * (1.0 + scale2) + shift2
        tpr = d // LANES
        for j in range(tpr):
            h2_ref[pl.ds(r0 * tpr + j, rc, stride=tpr), :] = h2[:, j * LANES:(j + 1) * LANES]
        hi = h2.astype(BF16)
        lo = (h2 - hi.astype(F32)).astype(BF16)
        logits = (jnp.dot(hi, rwh_ref[...], preferred_element_type=F32)
                  + jnp.dot(lo, rwh_ref[...], preferred_element_type=F32)
                  + jnp.dot(hi, rwl_ref[...], preferred_element_type=F32))
        l = jnp.where(klane < n_experts, logits + rb_ref[...], neg_inf)
        vals, ids = [], []
        for _ in range(TOP_K):
            m = jnp.max(l, axis=-1, keepdims=True)
            cand = jnp.where(l == m, klane, LANES).astype(F32)
            i = jnp.min(cand, axis=-1, keepdims=True).astype(jnp.int32)
            vals.append(m)
            ids.append(i)
            l = jnp.where(klane == i, neg_inf, l)
        es = [jnp.exp(v - vals[0]) for v in vals]
        inv = 1.0 / functools.reduce(lambda a, b: a + b, es)
        idx_out = jnp.zeros((rc, LANES), jnp.int32)
        wts_out = jnp.zeros((rc, LANES), F32)
        for k in range(TOP_K):
            idx_out = jnp.where(klane == k, ids[k], idx_out)
            wts_out = jnp.where(klane == k, es[k] * inv, wts_out)
        idx_ref[pl.ds(r0, rc), :] = idx_out
        wts_ref[pl.ds(r0, rc), :] = wts_out
        hits = functools.reduce(lambda a, b: a + b,
                                [jnp.where(klane == i, 1.0, 0.0) for i in ids])
        cnt_ref[...] += jnp.sum(hits, axis=0, keepdims=True)

    _loop(n_chunks, tail)


def _const_spec(shape):
    nd = len(shape)
    return pl.BlockSpec(shape, lambda b: (0,) * nd, pipeline_mode=pl.Buffered(1))


def _layer(x, mod, g1, win, lamv, subg, fcs, dftn, wout, g2, rwh, rwl, rb, *, n_experts,
           mod_row0, mod_row_step, h2_rows, h2_block0, zero_blocks=0, cache=None, h2_buf=None):
    bsz, n, d = x.shape
    heads = win.shape[1] // (4 * LANES)
    rope = cache is not None
    n_cache = cache[0].shape[2] if rope else 0
    nk = n_cache + n
    live = lambda b: jnp.minimum(b, bsz - 1)
    in_specs = [
        pl.BlockSpec((1, n, d), lambda b: (live(b), 0, 0)),
        pl.BlockSpec((1, 1, mod.shape[-1]),
                     lambda b: (mod_row0 + mod_row_step * live(b), 0, 0)),
        _const_spec(g1.shape), _const_spec(win.shape), _const_spec(lamv.shape),
        _const_spec(subg.shape), _const_spec(fcs.shape), _const_spec(dftn.shape),
        _const_spec(wout.shape), _const_spec(g2.shape), _const_spec(rwh.shape),
        _const_spec(rwl.shape), _const_spec(rb.shape),
    ]
    args = [x, mod, g1, win, lamv, subg, fcs, dftn, wout, g2, rwh, rwl, rb]
    out_specs = [
        pl.BlockSpec((n, d), lambda b: (live(b), 0)),
        pl.BlockSpec((n * (d // LANES), LANES), lambda b: (h2_block0 + b, 0)),
        pl.BlockSpec((n, LANES), lambda b: (live(b), 0)),
        pl.BlockSpec((n, LANES), lambda b: (live(b), 0)),
        pl.BlockSpec((1, LANES), lambda b: (0, 0)),
    ]
    out_shape = [
        jax.ShapeDtypeStruct((bsz * n, d), F32),
        jax.ShapeDtypeStruct((h2_rows * (d // LANES), LANES), F32),
        jax.ShapeDtypeStruct((bsz * n, LANES), jnp.int32),
        jax.ShapeDtypeStruct((bsz * n, LANES), F32),
        jax.ShapeDtypeStruct((1, LANES), F32),
    ]
    aliases = {}
    if rope:
        ck, cv, cos, sin = cache
        in_specs += [
            pl.BlockSpec((1, heads, n_cache, LANES), lambda b: (live(b), 0, 0, 0)),
            pl.BlockSpec((1, heads, n_cache, LANES), lambda b: (live(b), 0, 0, 0)),
            _const_spec(cos.shape), _const_spec(sin.shape),
            pl.BlockSpec(memory_space=pl.ANY),
        ]
        args += [ck, cv, cos, sin, h2_buf]
        aliases = {len(args) - 1: 1}
    else:
        kv_spec = pl.BlockSpec((1, 1, heads, n, LANES), lambda b: (live(b), 0, 0, 0, 0))
        out_specs += [kv_spec, kv_spec]
        kv_shape = jax.ShapeDtypeStruct((bsz, 1, heads, n, LANES), F32)
        out_shape += [kv_shape, kv_shape]
    scratch = [
        pltpu.VMEM((n, heads * LANES), BF16),
        pltpu.VMEM((n, heads * LANES), BF16),
        pltpu.VMEM((heads, nk, LANES), BF16),
        pltpu.VMEM((heads, nk, LANES), BF16),
        pltpu.VMEM((n, N_FGROUPS * LANES), BF16),
        pltpu.VMEM((2 * n, N_FGROUPS * LANES), BF16),
        pltpu.VMEM((n, d), BF16),
    ]
    kern = functools.partial(_layer_kernel, n_live=bsz if zero_blocks else None,
                             n_inputs=len(args), n=n, n_cache=n_cache, heads=heads,
                             n_experts=n_experts, rope=rope, emit_kv=not rope)
    return pl.pallas_call(
        kern,
        grid=(bsz + zero_blocks,),
        in_specs=in_specs,
        out_specs=out_specs,
        out_shape=out_shape,
        scratch_shapes=scratch,
        input_output_aliases=aliases,
        compiler_params=pltpu.CompilerParams(dimension_semantics=("arbitrary",),
                                             vmem_limit_bytes=VMEM_LIMIT),
        name="layer_latent" if rope else "layer_context",
    )(*args)


def _moe_kernel(texp_ref, next_ref, nvalid_ref, tok_ref, tok_next_ref, dst_ref, dst_prev_ref, h2_hbm,
                wgu_hbm, bgu_ref, wd_hbm, bd_ref, y_hbm, xbuf0, xbuf1, ybuf0, ybuf1, wgu_f32, wd_f32,
                wgu_bf, wd_bf, gsem, ssem, wsem, run_ref, *, tm):
    i = pl.program_id(0)
    nv = nvalid_ref[0]
    d_ff, d = wd_bf.shape
    tpr = d // LANES

    def weight_copies(e, s):
        return (pltpu.make_async_copy(wgu_hbm.at[e], wgu_f32.at[s], wsem.at[s, 0]),
                pltpu.make_async_copy(wd_hbm.at[e], wd_f32.at[s], wsem.at[s, 1]))

    def token_rows(t):
        return pl.ds(pl.multiple_of(t * tpr, tpr), tpr)

    xbuf, ybuf = (xbuf0, xbuf1), (ybuf0, ybuf1)

    def gather_row(idx_ref, slot, r):
        return pltpu.make_async_copy(h2_hbm.at[token_rows(idx_ref[0, 0, r]), :],
                                     xbuf[slot].at[pl.ds(r * tpr, tpr), :], gsem.at[slot])

    def scatter_row(idx_ref, slot, r):
        return pltpu.make_async_copy(ybuf[slot].at[pl.ds(r * tpr, tpr), :],
                                     y_hbm.at[token_rows(idx_ref[0, 0, r]), :], ssem.at[slot])

    def step(slot):
        other = 1 - slot
        for r in range(tm):
            gather_row(tok_ref, slot, r).wait()

        @pl.when(i >= 1)
        def _():
            for r in range(tm):
                scatter_row(dst_ref, slot, r).wait()

        for r in range(tm):
            gather_row(tok_next_ref, other, r).start()
        for r in range(tm):
            scatter_row(dst_prev_ref, other, r).start()

        x = jnp.concatenate(
            [xbuf[slot][pl.ds(j, tm, stride=tpr), :].astype(BF16) for j in range(tpr)], axis=1)
        gu = jnp.dot(x, wgu_bf[...], preferred_element_type=F32) + bgu_ref[0]
        glu = jnp.minimum(gu[:, :d_ff], SWIGLU_LIMIT)
        lin = jnp.clip(gu[:, d_ff:], -SWIGLU_LIMIT, SWIGLU_LIMIT)
        act = glu * _sigmoid(SWIGLU_ALPHA * glu) * (lin + 1.0)
        y = jnp.dot(act.astype(BF16), wd_bf[...], preferred_element_type=F32) + bd_ref[0]
        for j in range(tpr):
            ybuf[slot][pl.ds(j, tm, stride=tpr), :] = y[:, j * LANES:(j + 1) * LANES]

        @pl.when(i == nv - 1)
        def _():
            for r in range(tm):
                scatter_row(dst_ref, slot, r).start()
            for r in range(tm):
                scatter_row(dst_ref, slot, r).wait()
            for r in range(tm):
                scatter_row(dst_prev_ref, other, r).wait()
            for r in range(tm):
                gather_row(tok_next_ref, other, r).wait()

    @pl.when(i < nv)
    def _():
        @pl.when(i == 0)
        def _():
            run_ref[0] = 0
            for c in weight_copies(texp_ref[0], 0):
                c.start(priority=1)
            for r in range(tm):
                gather_row(tok_ref, 0, r).start()
            spare0 = y_hbm.shape[0] - 2 * tm * tpr
            for s in range(2):
                ybuf[s][...] = jnp.zeros(ybuf[s].shape, F32)
            fills = [pltpu.make_async_copy(
                ybuf[s], y_hbm.at[pl.ds(spare0 + s * tm * tpr, tm * tpr), :], ssem.at[s])
                for s in range(2)]
            for f in fills:
                f.start()
            for f in fills:
                f.wait()

        @pl.when(jnp.logical_or(i == 0, texp_ref[i] != texp_ref[jnp.maximum(i - 1, 0)]))
        def _():
            run = run_ref[0]
            ws = run % 2
            for c in weight_copies(texp_ref[i], ws):
                c.wait()

            @pl.when(next_ref[i] >= 0)
            def _():
                for c in weight_copies(next_ref[i], 1 - ws):
                    c.start(priority=1)

            wgu_bf[...] = wgu_f32[ws].astype(BF16)
            wd_bf[...] = wd_f32[ws].astype(BF16)
            run_ref[0] = run + 1

        for s in range(2):
            pl.when(i % 2 == s)(functools.partial(step, s))


def _moe(h2, tile_expert, next_expert, n_valid, src_tok, dest, wgu, bgu, wd, bd):
    n_exp, d, two_f = wgu.shape
    tpr = d // LANES
    t_rows = h2.shape[0] // tpr
    d_ff = two_f // 2
    n_tiles = tile_expert.shape[0]
    tm = src_tok.shape[-1]
    last = n_tiles - 1
    smem_tile = functools.partial(pl.BlockSpec, (1, 1, tm), memory_space=pltpu.SMEM)
    grid_spec = pltpu.PrefetchScalarGridSpec(
        num_scalar_prefetch=3,
        grid=(n_tiles,),
        in_specs=[
            smem_tile(lambda i, te, nx, nv: (i, 0, 0)),
            smem_tile(lambda i, te, nx, nv: (jnp.minimum(i + 1, last), 0, 0)),
            smem_tile(lambda i, te, nx, nv: (i + 1, 0, 0)),
            smem_tile(lambda i, te, nx, nv: (i, 0, 0)),
            pl.BlockSpec(memory_space=pl.ANY),
            pl.BlockSpec(memory_space=pl.ANY),
            pl.BlockSpec((1, 1, two_f), lambda i, te, nx, nv: (te[i], 0, 0)),
            pl.BlockSpec(memory_space=pl.ANY),
            pl.BlockSpec((1, 1, d), lambda i, te, nx, nv: (te[i], 0, 0)),
        ],
        out_specs=pl.BlockSpec(memory_space=pl.ANY),
        scratch_shapes=[
            pltpu.VMEM((tm * tpr, LANES), F32),
            pltpu.VMEM((tm * tpr, LANES), F32),
            pltpu.VMEM((tm * tpr, LANES), F32),
            pltpu.VMEM((tm * tpr, LANES), F32),
            pltpu.VMEM((2, d, two_f), F32),
            pltpu.VMEM((2, d_ff, d), F32),
            pltpu.VMEM((d, two_f), BF16),
            pltpu.VMEM((d_ff, d), BF16),
            pltpu.SemaphoreType.DMA((2,)),
            pltpu.SemaphoreType.DMA((2,)),
            pltpu.SemaphoreType.DMA((2, 2)),
            pltpu.SMEM((1,), jnp.int32),
        ],
    )
    return pl.pallas_call(
        functools.partial(_moe_kernel, tm=tm),
        grid_spec=grid_spec,
        out_shape=jax.ShapeDtypeStruct(((TOP_K * t_rows + 2 * tm) * tpr, LANES), F32),
        compiler_params=pltpu.CompilerParams(dimension_semantics=("arbitrary",),
                                             vmem_limit_bytes=VMEM_LIMIT),
        name="routed_moe",
    )(tile_expert, next_expert, n_valid, src_tok, src_tok, dest, dest, h2, wgu,
      bgu.reshape(n_exp, 1, two_f), wd, bd.reshape(n_exp, 1, d))


def _route(idx, counts, tm, n_tiles):
    t_rows, top_k = idx.shape
    n_exp = counts.shape[0]
    n_pairs = t_rows * top_k
    pad_bit = 16
    assert n_pairs <= 1 << pad_bit and tm <= 1 << pad_bit and n_tiles * tm == n_pairs + n_exp * tm
    pair_ids = np.arange(n_pairs, dtype=np.int32)
    real_keys = (idx.T.reshape(-1) << (pad_bit + 1)) | pair_ids
    pad_e = np.repeat(np.arange(n_exp, dtype=np.int32), tm)
    pad_j = np.tile(np.arange(tm, dtype=np.int32), n_exp)
    n_pad = (-counts) % tm
    unused = n_exp << (pad_bit + 1)
    pad_keys = jnp.where(pad_j < jnp.repeat(n_pad, tm),
                         (pad_e << (pad_bit + 1)) | (1 << pad_bit) | pad_j, unused)
    keys = jnp.sort(jnp.concatenate([real_keys, pad_keys]))
    is_real = jnp.logical_and((keys >> pad_bit) & 1 == 0, keys < unused)
    pair = keys & ((1 << pad_bit) - 1)
    slot = sum((pair >= k * t_rows).astype(jnp.int32) for k in range(1, top_k))
    pos = np.arange(n_tiles * tm, dtype=np.int32)
    spare = n_pairs + pos % (2 * tm)
    dest = jnp.concatenate([spare[:tm], jnp.where(is_real, pair, spare)])
    src_tok = jnp.where(is_real, pair - slot * t_rows, 0)
    n_valid = jnp.sum(counts + n_pad) // tm
    last_e = jnp.max(jnp.where(counts > 0, jnp.arange(n_exp, dtype=jnp.int32), 0))
    tile_expert = jnp.where(np.arange(n_tiles) < n_valid, keys[::tm] >> (pad_bit + 1), last_e)
    experts = jnp.arange(n_exp, dtype=jnp.int32)
    owner = jnp.where(counts > 0, experts, n_exp)
    following = jnp.concatenate([lax.cummin(owner, reverse=True)[1:],
                                 jnp.full((1,), n_exp, jnp.int32)])
    following = jnp.where(following >= n_exp, -1, following)
    next_expert = jnp.sum(jnp.where(tile_expert[:, None] == experts[None, :], following[None, :], 0),
                          axis=1)
    return (tile_expert.astype(jnp.int32), next_expert.astype(jnp.int32),
            n_valid.reshape(1).astype(jnp.int32), src_tok.reshape(n_tiles, 1, tm),
            dest.reshape(n_tiles + 1, 1, tm))


def _combine_kernel(x1_ref, y0_ref, y1_ref, y2_ref, y3_ref, wts_ref, gate_ref, fg_ref, o_ref):
    w = wts_ref[...]
    tc, d = x1_ref.shape
    tpr = d // LANES
    cols = []
    for j in range(tpr):
        acc = w[:, 0:1] * y0_ref[pl.ds(j, tc, stride=tpr), :]
        for k, y_ref in enumerate((y1_ref, y2_ref, y3_ref), start=1):
            acc = acc + w[:, k:k + 1] * y_ref[pl.ds(j, tc, stride=tpr), :]
        cols.append(acc)
    x = x1_ref[...] + gate_ref[0] * jnp.concatenate(cols, axis=1)
    o_ref[...] = x * _rsqrt_mean_sq(x) * fg_ref[...]


def _combine(x1, y, wts, gate2, final_g, *, t_all, row0, rows_per_gate):
    rows, d = x1.shape
    tc = COMBINE_TILE
    y_specs = [
        pl.BlockSpec((tc * (d // LANES), LANES), functools.partial(
            lambda i, k: ((k * t_all + row0) // tc + i, 0), k=k))
        for k in range(TOP_K)
    ]
    return pl.pallas_call(
        _combine_kernel,
        grid=(rows // tc,),
        in_specs=[pl.BlockSpec((tc, d), lambda i: (i, 0))] + y_specs + [
            pl.BlockSpec((tc, LANES), lambda i: (i, 0)),
            pl.BlockSpec((1, 1, d), lambda i: ((i * tc) // rows_per_gate, 0, 0)),
            pl.BlockSpec((1, d), lambda i: (0, 0)),
        ],
        out_specs=pl.BlockSpec((tc, d), lambda i: (i, 0)),
        out_shape=jax.ShapeDtypeStruct((rows, d), F32),
        compiler_params=pltpu.CompilerParams(dimension_semantics=("arbitrary",)),
        name="combine",
    )(x1, y, y, y, y, wts, gate2, final_g)


def _dft_tables(n):
    def angles(m):
        k = np.arange(m, dtype=np.int64)
        return (2.0 * np.pi / m) * ((k[:, None] * k[None, :]) % m)
    an = angles(n)
    ac = angles(LANES)
    dftn = np.concatenate([np.cos(an), -np.sin(an)], axis=1).astype(np.float32)
    fcs = np.concatenate([np.cos(ac), np.sin(ac)], axis=1).astype(np.float32)
    return jnp.asarray(dftn).astype(BF16), jnp.asarray(fcs).astype(BF16)


def _rope_tables(n, qk_dim):
    quarter = qk_dim // 4
    tok = np.arange(n)
    pos = np.stack([tok // GRID_W, tok % GRID_W], axis=-1).astype(np.float64)
    freqs = ROPE_THETA ** (-np.arange(quarter, dtype=np.float64) / quarter)
    ang = (pos[:, :, None] * freqs).reshape(n, 2 * quarter)
    cos, sin = np.cos(ang), np.sin(ang)
    row_c, col_c = cos[:, :quarter], cos[:, quarter:]
    row_s, col_s = sin[:, :quarter], sin[:, quarter:]
    cos_map = np.concatenate([row_c, row_c, col_c, col_c], axis=-1)
    sin_map = np.concatenate([-row_s, row_s, -col_s, col_s], axis=-1)
    reps = LANES // qk_dim
    return (np.tile(cos_map, (1, reps)).astype(np.float32),
            np.tile(sin_map, (1, reps)).astype(np.float32))


def kernel(x_prompt, x_sample, cache_k, cache_v, c, c_ctx, w_mod, b_mod, norm1_g, w_in, lambda_q1,
           lambda_k1, lambda_q2, lambda_k2, subln_g, w_out, norm2_g, router_w, router_b, w_gate_up,
           b_gate_up, w_down, b_down, final_g):
    bsz, seq, d = x_prompt.shape
    dec_b, dec_seq, _ = x_sample.shape
    heads, past, qk_dim = cache_k.shape[2], cache_k.shape[3], cache_k.shape[5]
    n_exp = router_w.shape[-1]
    t_ctx, t_den = bsz * seq, dec_b * dec_seq
    t_all = t_ctx + t_den
    assert t_ctx % dec_seq == 0 and 2 * qk_dim == LANES and dec_seq % GRID_W == 0
    assert (t_all * TOP_K) % MOE_TILE == 0 and t_all % COMBINE_TILE == 0

    cvec = jnp.concatenate([c_ctx[None, :], c, jnp.zeros((8 - 1 - dec_b, d), F32)], axis=0)
    mod = _modulation(cvec, w_mod[0], b_mod[0])[:, None, :]

    win = w_in[0].astype(BF16)
    wout = w_out[0].astype(BF16)
    rw = jnp.pad(router_w[0], ((0, 0), (0, LANES - n_exp)))
    rwh = rw.astype(BF16)
    rwl = (rw - rwh.astype(F32)).astype(BF16)
    rb = jnp.pad(router_b[0], (0, LANES - n_exp)).reshape(1, LANES)
    lamv = jnp.stack([lambda_q1[0], lambda_k1[0], lambda_q2[0], lambda_k2[0]], axis=0)
    g1 = norm1_g[0].reshape(1, d)
    g2 = norm2_g[0].reshape(1, d)
    subg = subln_g[0].reshape(1, LANES)
    dft_ctx, fcs = _dft_tables(seq)
    dft_den, _ = _dft_tables(dec_seq)
    cos, sin = _rope_tables(dec_seq, qk_dim)

    shared = (g1, win, lamv, subg, fcs)
    tail = (wout, g2, rwh, rwl, rb)
    x1_ctx, h2_all, idx_ctx, wts_ctx, cnt_ctx, new_k, new_v = _layer(
        x_prompt, mod, *shared, dft_ctx, *tail, n_experts=n_exp, mod_row0=0, mod_row_step=0,
        h2_rows=t_all, h2_block0=0, zero_blocks=t_den // seq)
    ck = cache_k[:, 0].reshape(dec_b, heads, past, LANES)
    cv = cache_v[:, 0]
    x1_den, h2_all, idx_den, wts_den, cnt_den = _layer(
        x_sample, mod, *shared, dft_den, *tail, n_experts=n_exp, mod_row0=1, mod_row_step=1,
        h2_rows=t_all, h2_block0=t_ctx // dec_seq, cache=(ck, cv, cos, sin), h2_buf=h2_all)

    idx = jnp.concatenate([idx_ctx[:, :TOP_K], idx_den[:, :TOP_K]], axis=0)
    n_tiles = (t_all * TOP_K) // MOE_TILE + n_exp
    counts = (cnt_ctx + cnt_den)[0, :n_exp].astype(jnp.int32)
    tile_expert, next_expert, n_valid, src_tok, dest = _route(idx, counts, MOE_TILE, n_tiles)
    y = _moe(h2_all, tile_expert, next_expert, n_valid, src_tok, dest, w_gate_up[0], b_gate_up[0],
             w_down[0], b_down[0])

    gate2 = mod[:, :, 5 * d:]
    fg = final_g.reshape(1, d)
    y_prompt = _combine(x1_ctx, y, wts_ctx, gate2[0:1], fg, t_all=t_all, row0=0,
                        rows_per_gate=t_ctx)
    y_sample = _combine(x1_den, y, wts_den, gate2[1:1 + dec_b], fg, t_all=t_all, row0=t_ctx,
                        rows_per_gate=dec_seq)
    return (y_prompt.reshape(bsz, seq, d), y_sample.reshape(dec_b, dec_seq, d),
            new_k.reshape(bsz, 1, heads, seq, 2, qk_dim), new_v)
```

```python
import functools
import math

import numpy as np
import jax
import jax.numpy as jnp
from jax import lax
from jax.experimental import pallas as pl
from jax.experimental.pallas import tpu as pltpu

F32 = jnp.float32
BF16 = jnp.bfloat16

GRID_W = 64
N_FGROUPS = 4
TOP_K = 4
SWIGLU_LIMIT = 7.0
SWIGLU_ALPHA = 1.702
ROPE_THETA = 10000.0
NORM_EPS = 1e-6
LAMBDA_INIT = 0.8 - 0.6 * math.exp(-0.3 * 0)

LANES = 128
ROW_CHUNK = 256
MOE_TILE = 256
MOE_RING = 3
COMBINE_TILE = 256
VMEM_LIMIT = 56 * 1024 * 1024


def _rsqrt_mean_sq(x):
    return lax.rsqrt(jnp.mean(x * x, axis=-1, keepdims=True) + NORM_EPS)


def _sigmoid(z):
    return 1.0 / (1.0 + jnp.exp(-z))


def _mod_kernel(c_ref, w_ref, b_ref, o_ref):
    c = c_ref[...]
    s = c * _sigmoid(c)
    o_ref[...] = jnp.dot(s.astype(BF16), w_ref[...].astype(BF16),
                         preferred_element_type=F32) + b_ref[...]


def _modulation(cvec, w_mod, b_mod):
    rows, d = cvec.shape
    n_out = w_mod.shape[1]
    return pl.pallas_call(
        _mod_kernel,
        grid=(n_out // d,),
        in_specs=[
            pl.BlockSpec((rows, d), lambda j: (0, 0)),
            pl.BlockSpec((d, d), lambda j: (0, j)),
            pl.BlockSpec((1, d), lambda j: (0, j)),
        ],
        out_specs=pl.BlockSpec((rows, d), lambda j: (0, j)),
        out_shape=jax.ShapeDtypeStruct((rows, n_out), F32),
        name="modulation",
    )(cvec, w_mod, b_mod.reshape(1, n_out))


def _loop(n, body):
    if n == 1:
        body(0)
    else:
        def step(i, carry):
            body(i)
            return carry
        lax.fori_loop(0, n, step, 0)


def _layer_kernel(*refs, n_live, n_inputs, **static):
    if n_live is None:
        _layer_body(*refs, **static)
        return
    b = pl.program_id(0)
    h2_ref = refs[n_inputs + 1]
    pl.when(b < n_live)(functools.partial(_layer_body, *refs, **static))

    @pl.when(b >= n_live)
    def _():
        h2_ref[...] = jnp.zeros(h2_ref.shape, F32)


def _layer_body(*refs, n, n_cache, heads, n_experts, rope, emit_kv):
    it = iter(refs)
    x_ref = next(it); mod_ref = next(it); g1_ref = next(it); win_ref = next(it)
    lamv_ref = next(it); subg_ref = next(it); fcs_ref = next(it); dftn_ref = next(it)
    wout_ref = next(it); g2_ref = next(it); rwh_ref = next(it); rwl_ref = next(it); rb_ref = next(it)
    if rope:
        ck_ref = next(it); cv_ref = next(it); cos_ref = next(it); sin_ref = next(it)
        next(it)
    x1_ref = next(it); h2_ref = next(it); idx_ref = next(it); wts_ref = next(it); cnt_ref = next(it)
    if emit_kv:
        newk_ref = next(it); newv_ref = next(it)
    q1_scr = next(it); q2_scr = next(it); kall = next(it); vall = next(it)
    f_scr = next(it); stk = next(it); mix = next(it)

    d = x_ref.shape[-1]
    qk_w = heads * LANES

    @pl.when(pl.program_id(0) == 0)
    def _():
        cnt_ref[...] = jnp.zeros(cnt_ref.shape, F32)
    rc = min(ROW_CHUNK, n)
    n_chunks = n // rc

    def mod_row(j):
        return mod_ref[0, :, j * d:(j + 1) * d]

    shift1, scale1, gate1 = mod_row(0), mod_row(1), mod_row(2)
    shift2, scale2, gate2 = mod_row(3), mod_row(4), mod_row(5)
    del gate2

    lv = lamv_ref[...]
    lam = (jnp.exp(jnp.sum(lv[0:1] * lv[1:2], axis=-1, keepdims=True))
           - jnp.exp(jnp.sum(lv[2:3] * lv[3:4], axis=-1, keepdims=True)) + LAMBDA_INIT)

    if rope:
        for hd in range(heads):
            kall[hd, 0:n_cache, :] = ck_ref[0, hd].astype(BF16)
            vall[hd, 0:n_cache, :] = cv_ref[0, hd].astype(BF16)

    lane = lax.broadcasted_iota(jnp.int32, (rc, LANES), 1)
    first_map = lane < (LANES // 2)

    first_of_pair = jnp.bitwise_and(lane, 31) < 16

    def rotate(t, cos, sin):
        partner = jnp.where(first_of_pair, pltpu.roll(t, LANES - 16, 1), pltpu.roll(t, 16, 1))
        return t * cos + partner * sin

    def project(c):
        r0 = pl.multiple_of(c * rc, rc)
        x = x_ref[0, pl.ds(r0, rc), :]
        h = (x * _rsqrt_mean_sq(x) * g1_ref[...]) * (1.0 + scale1) + shift1
        p = jnp.dot(h.astype(BF16), win_ref[...], preferred_element_type=F32)
        if rope:
            cos = cos_ref[pl.ds(r0, rc), :]
            sin = sin_ref[pl.ds(r0, rc), :]
        for hd in range(heads):
            qh = p[:, hd * LANES:(hd + 1) * LANES]
            kh = p[:, qk_w + hd * LANES:qk_w + (hd + 1) * LANES]
            vh = p[:, 2 * qk_w + hd * LANES:2 * qk_w + (hd + 1) * LANES]
            if rope:
                qh = rotate(qh, cos, sin)
                kh = rotate(kh, cos, sin)
            if emit_kv:
                newk_ref[0, 0, hd, pl.ds(r0, rc), :] = kh
                newv_ref[0, 0, hd, pl.ds(r0, rc), :] = vh
            qs = qh * (LANES // 2) ** -0.5
            q1_scr[pl.ds(r0, rc), hd * LANES:(hd + 1) * LANES] = jnp.where(first_map, qs, 0.0).astype(BF16)
            q2_scr[pl.ds(r0, rc), hd * LANES:(hd + 1) * LANES] = jnp.where(first_map, 0.0, qs).astype(BF16)
            kall[hd, pl.ds(n_cache + r0, rc), :] = kh.astype(BF16)
            vall[hd, pl.ds(n_cache + r0, rc), :] = vh.astype(BF16)
        f_scr[pl.ds(r0, rc), :] = p[:, 3 * qk_w:].astype(BF16)

    _loop(n_chunks, project)

    contract_last = (((1,), (1,)), ((), ()))

    def softmax(s):
        e = jnp.exp(s - jnp.max(s, axis=-1, keepdims=True))
        return e * (1.0 / jnp.sum(e, axis=-1, keepdims=True))

    for hd in range(heads):
        def attend(c, hd=hd):
            r0 = pl.multiple_of(c * rc, rc)
            kh = kall[hd]
            s1 = lax.dot_general(q1_scr[pl.ds(r0, rc), hd * LANES:(hd + 1) * LANES], kh,
                                 contract_last, preferred_element_type=F32)
            s2 = lax.dot_general(q2_scr[pl.ds(r0, rc), hd * LANES:(hd + 1) * LANES], kh,
                                 contract_last, preferred_element_type=F32)
            a = softmax(s1) - lam * softmax(s2)
            o = jnp.dot(a.astype(BF16), vall[hd], preferred_element_type=F32)
            o = o * _rsqrt_mean_sq(o) * subg_ref[...] * (1.0 - LAMBDA_INIT)
            mix[pl.ds(r0, rc), hd * LANES:(hd + 1) * LANES] = o.astype(BF16)

        _loop(n_chunks, attend)

    def dft_channels(c):
        r0 = pl.multiple_of(c * rc, rc)
        for g in range(N_FGROUPS):
            a = jnp.dot(f_scr[pl.ds(r0, rc), g * LANES:(g + 1) * LANES], fcs_ref[...],
                        preferred_element_type=F32)
            stk[pl.ds(r0, rc), g * LANES:(g + 1) * LANES] = a[:, :LANES].astype(BF16)
            stk[pl.ds(pl.multiple_of(n + r0, rc), rc), g * LANES:(g + 1) * LANES] = (
                a[:, LANES:].astype(BF16))

    _loop(n_chunks, dft_channels)

    fscale = 1.0 / math.sqrt(n * LANES)

    def dft_positions(c):
        r0 = pl.multiple_of(c * rc, rc)
        y = jnp.dot(dftn_ref[pl.ds(r0, rc), :], stk[...], preferred_element_type=F32) * fscale
        mix[pl.ds(r0, rc), qk_w:] = y.astype(BF16)

    _loop(n_chunks, dft_positions)

    klane = lax.broadcasted_iota(jnp.int32, (rc, LANES), 1)
    neg_inf = jnp.float32(-jnp.inf)

    def tail(c):
        r0 = pl.multiple_of(c * rc, rc)
        x = x_ref[0, pl.ds(r0, rc), :]
        mixed = jnp.dot(mix[pl.ds(r0, rc), :], wout_ref[...], preferred_element_type=F32)
        x1 = x + gate1 * mixed
        x1_ref[pl.ds(r0, rc), :] = x1
        h2 = (x1 * _rsqrt_mean_sq(x1) * g2_ref[...]) * (1.0 + scale2) + shift2
        tpr = d // LANES
        for j in range(tpr):
            h2_ref[pl.ds(r0 * tpr + j, rc, stride=tpr), :] = h2[:, j * LANES:(j + 1) * LANES]
        hi = h2.astype(BF16)
        lo = (h2 - hi.astype(F32)).astype(BF16)
        logits = (jnp.dot(hi, rwh_ref[...], preferred_element_type=F32)
                  + jnp.dot(lo, rwh_ref[...], preferred_element_type=F32)
                  + jnp.dot(hi, rwl_ref[...], preferred_element_type=F32))
        l = jnp.where(klane < n_experts, logits + rb_ref[...], neg_inf)
        vals, ids = [], []
        for _ in range(TOP_K):
            m = jnp.max(l, axis=-1, keepdims=True)
            cand = jnp.where(l == m, klane, LANES).astype(F32)
            i = jnp.min(cand, axis=-1, keepdims=True).astype(jnp.int32)
            vals.append(m)
            ids.append(i)
            l = jnp.where(klane == i, neg_inf, l)
        es = [jnp.exp(v - vals[0]) for v in vals]
        inv = 1.0 / functools.reduce(lambda a, b: a + b, es)
        idx_out = jnp.zeros((rc, LANES), jnp.int32)
        wts_out = jnp.zeros((rc, LANES), F32)
        for k in range(TOP_K):
            idx_out = jnp.where(klane == k, ids[k], idx_out)
            wts_out = jnp.where(klane == k, es[k] * inv, wts_out)
        idx_ref[pl.ds(r0, rc), :] = idx_out
        wts_ref[pl.ds(r0, rc), :] = wts_out
        hits = functools.reduce(lambda a, b: a + b,
                                [jnp.where(klane == i, 1.0, 0.0) for i in ids])
        cnt_ref[...] += jnp.sum(hits, axis=0, keepdims=True)

    _loop(n_chunks, tail)


def _const_spec(shape):
    nd = len(shape)
    return pl.BlockSpec(shape, lambda b: (0,) * nd, pipeline_mode=pl.Buffered(1))


def _layer(x, mod, g1, win, lamv, subg, fcs, dftn, wout, g2, rwh, rwl, rb, *, n_experts,
           mod_row0, mod_row_step, h2_rows, h2_block0, zero_blocks=0, cache=None, h2_buf=None):
    bsz, n, d = x.shape
    heads = win.shape[1] // (4 * LANES)
    rope = cache is not None
    n_cache = cache[0].shape[2] if rope else 0
    nk = n_cache + n
    live = lambda b: jnp.minimum(b, bsz - 1)
    in_specs = [
        pl.BlockSpec((1, n, d), lambda b: (live(b), 0, 0)),
        pl.BlockSpec((1, 1, mod.shape[-1]),
                     lambda b: (mod_row0 + mod_row_step * live(b), 0, 0)),
        _const_spec(g1.shape), _const_spec(win.shape), _const_spec(lamv.shape),
        _const_spec(subg.shape), _const_spec(fcs.shape), _const_spec(dftn.shape),
        _const_spec(wout.shape), _const_spec(g2.shape), _const_spec(rwh.shape),
        _const_spec(rwl.shape), _const_spec(rb.shape),
    ]
    args = [x, mod, g1, win, lamv, subg, fcs, dftn, wout, g2, rwh, rwl, rb]
    out_specs = [
        pl.BlockSpec((n, d), lambda b: (live(b), 0)),
        pl.BlockSpec((n * (d // LANES), LANES), lambda b: (h2_block0 + b, 0)),
        pl.BlockSpec((n, LANES), lambda b: (live(b), 0)),
        pl.BlockSpec((n, LANES), lambda b: (live(b), 0)),
        pl.BlockSpec((1, LANES), lambda b: (0, 0)),
    ]
    out_shape = [
        jax.ShapeDtypeStruct((bsz * n, d), F32),
        jax.ShapeDtypeStruct((h2_rows * (d // LANES), LANES), F32),
        jax.ShapeDtypeStruct((bsz * n, LANES), jnp.int32),
        jax.ShapeDtypeStruct((bsz * n, LANES), F32),
        jax.ShapeDtypeStruct((1, LANES), F32),
    ]
    aliases = {}
    if rope:
        ck, cv, cos, sin = cache
        in_specs += [
            pl.BlockSpec((1, heads, n_cache, LANES), lambda b: (live(b), 0, 0, 0)),
            pl.BlockSpec((1, heads, n_cache, LANES), lambda b: (live(b), 0, 0, 0)),
            _const_spec(cos.shape), _const_spec(sin.shape),
            pl.BlockSpec(memory_space=pl.ANY),
        ]
        args += [ck, cv, cos, sin, h2_buf]
        aliases = {len(args) - 1: 1}
    else:
        kv_spec = pl.BlockSpec((1, 1, heads, n, LANES), lambda b: (live(b), 0, 0, 0, 0))
        out_specs += [kv_spec, kv_spec]
        kv_shape = jax.ShapeDtypeStruct((bsz, 1, heads, n, LANES), F32)
        out_shape += [kv_shape, kv_shape]
    scratch = [
        pltpu.VMEM((n, heads * LANES), BF16),
        pltpu.VMEM((n, heads * LANES), BF16),
        pltpu.VMEM((heads, nk, LANES), BF16),
        pltpu.VMEM((heads, nk, LANES), BF16),
        pltpu.VMEM((n, N_FGROUPS * LANES), BF16),
        pltpu.VMEM((2 * n, N_FGROUPS * LANES), BF16),
        pltpu.VMEM((n, d), BF16),
    ]
    kern = functools.partial(_layer_kernel, n_live=bsz if zero_blocks else None,
                             n_inputs=len(args), n=n, n_cache=n_cache, heads=heads,
                             n_experts=n_experts, rope=rope, emit_kv=not rope)
    return pl.pallas_call(
        kern,
        grid=(bsz + zero_blocks,),
        in_specs=in_specs,
        out_specs=out_specs,
        out_shape=out_shape,
        scratch_shapes=scratch,
        input_output_aliases=aliases,
        compiler_params=pltpu.CompilerParams(dimension_semantics=("arbitrary",),
                                             vmem_limit_bytes=VMEM_LIMIT),
        name="layer_latent" if rope else "layer_context",
    )(*args)


def _moe_kernel(texp_ref, next_ref, nvalid_ref, tok_ref, tok_next_ref, tok_next2_ref, dst_ref,
                dst_prev_ref, h2_hbm, wgu_hbm, bgu_ref, wd_hbm, bd_ref, y_hbm, xbuf0, xbuf1, xbuf2,
                ybuf0, ybuf1, ybuf2, wgu_f32, wd_f32, wgu_bf, wd_bf, gsem, ssem, wsem, run_ref, *,
                tm):
    i = pl.program_id(0)
    nv = nvalid_ref[0]
    d_ff, d = wd_bf.shape
    tpr = d // LANES

    def weight_copies(e, s):
        return (pltpu.make_async_copy(wgu_hbm.at[e], wgu_f32.at[s], wsem.at[s, 0]),
                pltpu.make_async_copy(wd_hbm.at[e], wd_f32.at[s], wsem.at[s, 1]))

    def token_rows(t):
        return pl.ds(pl.multiple_of(t * tpr, tpr), tpr)

    xbuf, ybuf = (xbuf0, xbuf1, xbuf2), (ybuf0, ybuf1, ybuf2)

    def gather_row(idx_ref, slot, r):
        return pltpu.make_async_copy(h2_hbm.at[token_rows(idx_ref[0, 0, r]), :],
                                     xbuf[slot].at[pl.ds(r * tpr, tpr), :], gsem.at[slot])

    def scatter_row(idx_ref, slot, r):
        return pltpu.make_async_copy(ybuf[slot].at[pl.ds(r * tpr, tpr), :],
                                     y_hbm.at[token_rows(idx_ref[0, 0, r]), :], ssem.at[slot])

    def step(slot):
        ahead, behind = (slot + 1) % MOE_RING, (slot + 2) % MOE_RING
        for r in range(tm):
            gather_row(tok_ref, slot, r).wait()

        @pl.when(i >= 2)
        def _():
            for r in range(tm):
                scatter_row(dst_ref, slot, r).wait()

        for r in range(tm):
            gather_row(tok_next2_ref, behind, r).start()
        for r in range(tm):
            scatter_row(dst_prev_ref, behind, r).start()

        x = jnp.concatenate(
            [xbuf[slot][pl.ds(j, tm, stride=tpr), :].astype(BF16) for j in range(tpr)], axis=1)
        gu = jnp.dot(x, wgu_bf[...], preferred_element_type=F32) + bgu_ref[0]
        glu = jnp.minimum(gu[:, :d_ff], SWIGLU_LIMIT)
        lin = jnp.clip(gu[:, d_ff:], -SWIGLU_LIMIT, SWIGLU_LIMIT)
        act = glu * _sigmoid(SWIGLU_ALPHA * glu) * (lin + 1.0)
        y = jnp.dot(act.astype(BF16), wd_bf[...], preferred_element_type=F32) + bd_ref[0]
        for j in range(tpr):
            ybuf[slot][pl.ds(j, tm, stride=tpr), :] = y[:, j * LANES:(j + 1) * LANES]

        @pl.when(i == nv - 1)
        def _():
            for r in range(tm):
                scatter_row(dst_ref, slot, r).start()
            for r in range(tm):
                scatter_row(dst_ref, slot, r).wait()
            for r in range(tm):
                scatter_row(dst_prev_ref, behind, r).wait()

            @pl.when(i >= 1)
            def _():
                for r in range(tm):
                    scatter_row(dst_prev_ref, ahead, r).wait()

            for r in range(tm):
                gather_row(tok_next_ref, ahead, r).wait()
            for r in range(tm):
                gather_row(tok_next2_ref, behind, r).wait()

    @pl.when(i < nv)
    def _():
        @pl.when(i == 0)
        def _():
            run_ref[0] = 0
            for c in weight_copies(texp_ref[0], 0):
                c.start(priority=1)
            for r in range(tm):
                gather_row(tok_ref, 0, r).start()
            for r in range(tm):
                gather_row(tok_next_ref, 1, r).start()
            spare0 = y_hbm.shape[0] - 2 * tm * tpr
            for s in range(MOE_RING):
                ybuf[s][...] = jnp.zeros(ybuf[s].shape, F32)
            fills = [pltpu.make_async_copy(
                ybuf[s], y_hbm.at[pl.ds(spare0 + s * tm * tpr, tm * tpr), :], ssem.at[s])
                for s in range(2)]
            for f in fills:
                f.start()
            for f in fills:
                f.wait()

        @pl.when(jnp.logical_or(i == 0, texp_ref[i] != texp_ref[jnp.maximum(i - 1, 0)]))
        def _():
            run = run_ref[0]
            ws = run % 2
            for c in weight_copies(texp_ref[i], ws):
                c.wait()

            @pl.when(next_ref[i] >= 0)
            def _():
                for c in weight_copies(next_ref[i], 1 - ws):
                    c.start(priority=1)

            wgu_bf[...] = wgu_f32[ws].astype(BF16)
            wd_bf[...] = wd_f32[ws].astype(BF16)
            run_ref[0] = run + 1

        for s in range(MOE_RING):
            pl.when(i % MOE_RING == s)(functools.partial(step, s))


def _moe(h2, tile_expert, next_expert, n_valid, src_tok, dest, wgu, bgu, wd, bd):
    n_exp, d, two_f = wgu.shape
    tpr = d // LANES
    t_rows = h2.shape[0] // tpr
    d_ff = two_f // 2
    n_tiles = tile_expert.shape[0]
    tm = src_tok.shape[-1]
    last = n_tiles - 1
    smem_tile = functools.partial(pl.BlockSpec, (1, 1, tm), memory_space=pltpu.SMEM)
    grid_spec = pltpu.PrefetchScalarGridSpec(
        num_scalar_prefetch=3,
        grid=(n_tiles,),
        in_specs=[
            smem_tile(lambda i, te, nx, nv: (i, 0, 0)),
            smem_tile(lambda i, te, nx, nv: (jnp.minimum(i + 1, last), 0, 0)),
            smem_tile(lambda i, te, nx, nv: (jnp.minimum(i + 2, last), 0, 0)),
            smem_tile(lambda i, te, nx, nv: (i + 1, 0, 0)),
            smem_tile(lambda i, te, nx, nv: (i, 0, 0)),
            pl.BlockSpec(memory_space=pl.ANY),
            pl.BlockSpec(memory_space=pl.ANY),
            pl.BlockSpec((1, 1, two_f), lambda i, te, nx, nv: (te[i], 0, 0)),
            pl.BlockSpec(memory_space=pl.ANY),
            pl.BlockSpec((1, 1, d), lambda i, te, nx, nv: (te[i], 0, 0)),
        ],
        out_specs=pl.BlockSpec(memory_space=pl.ANY),
        scratch_shapes=[
            *([pltpu.VMEM((tm * tpr, LANES), F32)] * (2 * MOE_RING)),
            pltpu.VMEM((2, d, two_f), F32),
            pltpu.VMEM((2, d_ff, d), F32),
            pltpu.VMEM((d, two_f), BF16),
            pltpu.VMEM((d_ff, d), BF16),
            pltpu.SemaphoreType.DMA((MOE_RING,)),
            pltpu.SemaphoreType.DMA((MOE_RING,)),
            pltpu.SemaphoreType.DMA((2, 2)),
            pltpu.SMEM((1,), jnp.int32),
        ],
    )
    return pl.pallas_call(
        functools.partial(_moe_kernel, tm=tm),
        grid_spec=grid_spec,
        out_shape=jax.ShapeDtypeStruct(((TOP_K * t_rows + 2 * tm) * tpr, LANES), F32),
        compiler_params=pltpu.CompilerParams(dimension_semantics=("arbitrary",),
                                             vmem_limit_bytes=VMEM_LIMIT),
        name="routed_moe",
    )(tile_expert, next_expert, n_valid, src_tok, src_tok, src_tok, dest, dest, h2, wgu,
      bgu.reshape(n_exp, 1, two_f), wd, bd.reshape(n_exp, 1, d))


def _route(idx, counts, tm, n_tiles):
    t_rows, top_k = idx.shape
    n_exp = counts.shape[0]
    n_pairs = t_rows * top_k
    pad_bit = 16
    assert n_pairs <= 1 << pad_bit and tm <= 1 << pad_bit and n_tiles * tm == n_pairs + n_exp * tm
    pair_ids = np.arange(n_pairs, dtype=np.int32)
    real_keys = (idx.T.reshape(-1) << (pad_bit + 1)) | pair_ids
    pad_e = np.repeat(np.arange(n_exp, dtype=np.int32), tm)
    pad_j = np.tile(np.arange(tm, dtype=np.int32), n_exp)
    n_pad = (-counts) % tm
    unused = n_exp << (pad_bit + 1)
    pad_keys = jnp.where(pad_j < jnp.repeat(n_pad, tm),
                         (pad_e << (pad_bit + 1)) | (1 << pad_bit) | pad_j, unused)
    keys = jnp.sort(jnp.concatenate([real_keys, pad_keys]))
    is_real = jnp.logical_and((keys >> pad_bit) & 1 == 0, keys < unused)
    pair = keys & ((1 << pad_bit) - 1)
    slot = sum((pair >= k * t_rows).astype(jnp.int32) for k in range(1, top_k))
    pos = np.arange(n_tiles * tm, dtype=np.int32)
    spare = n_pairs + pos % (2 * tm)
    dest = jnp.concatenate([spare[:tm], jnp.where(is_real, pair, spare)])
    src_tok = jnp.where(is_real, pair - slot * t_rows, 0)
    n_valid = jnp.sum(counts + n_pad) // tm
    last_e = jnp.max(jnp.where(counts > 0, jnp.arange(n_exp, dtype=jnp.int32), 0))
    tile_expert = jnp.where(np.arange(n_tiles) < n_valid, keys[::tm] >> (pad_bit + 1), last_e)
    experts = jnp.arange(n_exp, dtype=jnp.int32)
    owner = jnp.where(counts > 0, experts, n_exp)
    following = jnp.concatenate([lax.cummin(owner, reverse=True)[1:],
                                 jnp.full((1,), n_exp, jnp.int32)])
    following = jnp.where(following >= n_exp, -1, following)
    next_expert = jnp.sum(jnp.where(tile_expert[:, None] == experts[None, :], following[None, :], 0),
                          axis=1)
    return (tile_expert.astype(jnp.int32), next_expert.astype(jnp.int32),
            n_valid.reshape(1).astype(jnp.int32), src_tok.reshape(n_tiles, 1, tm),
            dest.reshape(n_tiles + 1, 1, tm))


def _combine_kernel(x1_ref, y0_ref, y1_ref, y2_ref, y3_ref, wts_ref, gate_ref, fg_ref, o_ref):
    w = wts_ref[...]
    tc, d = x1_ref.shape
    tpr = d // LANES
    cols = []
    for j in range(tpr):
        acc = w[:, 0:1] * y0_ref[pl.ds(j, tc, stride=tpr), :]
        for k, y_ref in enumerate((y1_ref, y2_ref, y3_ref), start=1):
            acc = acc + w[:, k:k + 1] * y_ref[pl.ds(j, tc, stride=tpr), :]
        cols.append(acc)
    x = x1_ref[...] + gate_ref[0] * jnp.concatenate(cols, axis=1)
    o_ref[...] = x * _rsqrt_mean_sq(x) * fg_ref[...]


def _combine(x1, y, wts, gate2, final_g, *, t_all, row0, rows_per_gate):
    rows, d = x1.shape
    tc = COMBINE_TILE
    y_specs = [
        pl.BlockSpec((tc * (d // LANES), LANES), functools.partial(
            lambda i, k: ((k * t_all + row0) // tc + i, 0), k=k))
        for k in range(TOP_K)
    ]
    return pl.pallas_call(
        _combine_kernel,
        grid=(rows // tc,),
        in_specs=[pl.BlockSpec((tc, d), lambda i: (i, 0))] + y_specs + [
            pl.BlockSpec((tc, LANES), lambda i: (i, 0)),
            pl.BlockSpec((1, 1, d), lambda i: ((i * tc) // rows_per_gate, 0, 0)),
            pl.BlockSpec((1, d), lambda i: (0, 0)),
        ],
        out_specs=pl.BlockSpec((tc, d), lambda i: (i, 0)),
        out_shape=jax.ShapeDtypeStruct((rows, d), F32),
        compiler_params=pltpu.CompilerParams(dimension_semantics=("arbitrary",)),
        name="combine",
    )(x1, y, y, y, y, wts, gate2, final_g)


def _dft_tables(n):
    def angles(m):
        k = np.arange(m, dtype=np.int64)
        return (2.0 * np.pi / m) * ((k[:, None] * k[None, :]) % m)
    an = angles(n)
    ac = angles(LANES)
    dftn = np.concatenate([np.cos(an), -np.sin(an)], axis=1).astype(np.float32)
    fcs = np.concatenate([np.cos(ac), np.sin(ac)], axis=1).astype(np.float32)
    return jnp.asarray(dftn).astype(BF16), jnp.asarray(fcs).astype(BF16)


def _rope_tables(n, qk_dim):
    quarter = qk_dim // 4
    tok = np.arange(n)
    pos = np.stack([tok // GRID_W, tok % GRID_W], axis=-1).astype(np.float64)
    freqs = ROPE_THETA ** (-np.arange(quarter, dtype=np.float64) / quarter)
    ang = (pos[:, :, None] * freqs).reshape(n, 2 * quarter)
    cos, sin = np.cos(ang), np.sin(ang)
    row_c, col_c = cos[:, :quarter], cos[:, quarter:]
    row_s, col_s = sin[:, :quarter], sin[:, quarter:]
    cos_map = np.concatenate([row_c, row_c, col_c, col_c], axis=-1)
    sin_map = np.concatenate([-row_s, row_s, -col_s, col_s], axis=-1)
    reps = LANES // qk_dim
    return (np.tile(cos_map, (1, reps)).astype(np.float32),
            np.tile(sin_map, (1, reps)).astype(np.float32))


def kernel(x_prompt, x_sample, cache_k, cache_v, c, c_ctx, w_mod, b_mod, norm1_g, w_in, lambda_q1,
           lambda_k1, lambda_q2, lambda_k2, subln_g, w_out, norm2_g, router_w, router_b, w_gate_up,
           b_gate_up, w_down, b_down, final_g):
    bsz, seq, d = x_prompt.shape
    dec_b, dec_seq, _ = x_sample.shape
    heads, past, qk_dim = cache_k.shape[2], cache_k.shape[3], cache_k.shape[5]
    n_exp = router_w.shape[-1]
    t_ctx, t_den = bsz * seq, dec_b * dec_seq
    t_all = t_ctx + t_den
    assert t_ctx % dec_seq == 0 and 2 * qk_dim == LANES and dec_seq % GRID_W == 0
    assert (t_all * TOP_K) % MOE_TILE == 0 and t_all % COMBINE_TILE == 0

    cvec = jnp.concatenate([c_ctx[None, :], c, jnp.zeros((8 - 1 - dec_b, d), F32)], axis=0)
    mod = _modulation(cvec, w_mod[0], b_mod[0])[:, None, :]

    win = w_in[0].astype(BF16)
    wout = w_out[0].astype(BF16)
    rw = jnp.pad(router_w[0], ((0, 0), (0, LANES - n_exp)))
    rwh = rw.astype(BF16)
    rwl = (rw - rwh.astype(F32)).astype(BF16)
    rb = jnp.pad(router_b[0], (0, LANES - n_exp)).reshape(1, LANES)
    lamv = jnp.stack([lambda_q1[0], lambda_k1[0], lambda_q2[0], lambda_k2[0]], axis=0)
    g1 = norm1_g[0].reshape(1, d)
    g2 = norm2_g[0].reshape(1, d)
    subg = subln_g[0].reshape(1, LANES)
    dft_ctx, fcs = _dft_tables(seq)
    dft_den, _ = _dft_tables(dec_seq)
    cos, sin = _rope_tables(dec_seq, qk_dim)

    shared = (g1, win, lamv, subg, fcs)
    tail = (wout, g2, rwh, rwl, rb)
    x1_ctx, h2_all, idx_ctx, wts_ctx, cnt_ctx, new_k, new_v = _layer(
        x_prompt, mod, *shared, dft_ctx, *tail, n_experts=n_exp, mod_row0=0, mod_row_step=0,
        h2_rows=t_all, h2_block0=0, zero_blocks=t_den // seq)
    ck = cache_k[:, 0].reshape(dec_b, heads, past, LANES)
    cv = cache_v[:, 0]
    x1_den, h2_all, idx_den, wts_den, cnt_den = _layer(
        x_sample, mod, *shared, dft_den, *tail, n_experts=n_exp, mod_row0=1, mod_row_step=1,
        h2_rows=t_all, h2_block0=t_ctx // dec_seq, cache=(ck, cv, cos, sin), h2_buf=h2_all)

    idx = jnp.concatenate([idx_ctx[:, :TOP_K], idx_den[:, :TOP_K]], axis=0)
    n_tiles = (t_all * TOP_K) // MOE_TILE + n_exp
    counts = (cnt_ctx + cnt_den)[0, :n_exp].astype(jnp.int32)
    tile_expert, next_expert, n_valid, src_tok, dest = _route(idx, counts, MOE_TILE, n_tiles)
    y = _moe(h2_all, tile_expert, next_expert, n_valid, src_tok, dest, w_gate_up[0], b_gate_up[0],
             w_down[0], b_down[0])

    gate2 = mod[:, :, 5 * d:]
    fg = final_g.reshape(1, d)
    y_prompt = _combine(x1_ctx, y, wts_ctx, gate2[0:1], fg, t_all=t_all, row0=0,
                        rows_per_gate=t_ctx)
    y_sample = _combine(x1_den, y, wts_den, gate2[1:1 + dec_b], fg, t_all=t_all, row0=t_ctx,
                        rows_per_gate=dec_seq)
    return (y_prompt.reshape(bsz, seq, d), y_sample.reshape(dec_b, dec_seq, d),
            new_k.reshape(bsz, 1, heads, seq, 2, qk_dim), new_v)
```

```python
import functools
import math

import numpy as np
import jax
import jax.numpy as jnp
from jax import lax
from jax.experimental import pallas as pl
from jax.experimental.pallas import tpu as pltpu

F32 = jnp.float32
BF16 = jnp.bfloat16

GRID_W = 64
N_FGROUPS = 4
TOP_K = 4
SWIGLU_LIMIT = 7.0
SWIGLU_ALPHA = 1.702
ROPE_THETA = 10000.0
NORM_EPS = 1e-6
LAMBDA_INIT = 0.8 - 0.6 * math.exp(-0.3 * 0)

LANES = 128
ROW_CHUNK = 256
MOE_TILE = 256
MOE_RING = 3
COMBINE_TILE = 256
VMEM_LIMIT = 56 * 1024 * 1024


def _rsqrt_mean_sq(x):
    return lax.rsqrt(jnp.mean(x * x, axis=-1, keepdims=True) + NORM_EPS)


def _sigmoid(z):
    return 1.0 / (1.0 + jnp.exp(-z))


def _mod_kernel(c_ref, w_ref, b_ref, o_ref):
    c = c_ref[...]
    s = c * _sigmoid(c)
    o_ref[...] = jnp.dot(s.astype(BF16), w_ref[...].astype(BF16),
                         preferred_element_type=F32) + b_ref[...]


def _modulation(cvec, w_mod, b_mod):
    rows, d = cvec.shape
    n_out = w_mod.shape[1]
    return pl.pallas_call(
        _mod_kernel,
        grid=(n_out // d,),
        in_specs=[
            pl.BlockSpec((rows, d), lambda j: (0, 0)),
            pl.BlockSpec((d, d), lambda j: (0, j)),
            pl.BlockSpec((1, d), lambda j: (0, j)),
        ],
        out_specs=pl.BlockSpec((rows, d), lambda j: (0, j)),
        out_shape=jax.ShapeDtypeStruct((rows, n_out), F32),
        name="modulation",
    )(cvec, w_mod, b_mod.reshape(1, n_out))


def _loop(n, body):
    if n == 1:
        body(0)
    else:
        def step(i, carry):
            body(i)
            return carry
        lax.fori_loop(0, n, step, 0)


def _layer_kernel(*refs, n_live, n_inputs, **static):
    if n_live is None:
        _layer_body(*refs, **static)
        return
    b = pl.program_id(0)
    h2_ref = refs[n_inputs + 1]
    pl.when(b < n_live)(functools.partial(_layer_body, *refs, **static))

    @pl.when(b >= n_live)
    def _():
        h2_ref[...] = jnp.zeros(h2_ref.shape, F32)


def _layer_body(*refs, n, n_cache, heads, n_experts, rope, emit_kv):
    it = iter(refs)
    x_ref = next(it); mod_ref = next(it); g1_ref = next(it); win_ref = next(it)
    lamv_ref = next(it); subg_ref = next(it); fcs_ref = next(it); dftn_ref = next(it)
    wout_ref = next(it); g2_ref = next(it); rwh_ref = next(it); rwl_ref = next(it); rb_ref = next(it)
    if rope:
        ck_ref = next(it); cv_ref = next(it); cos_ref = next(it); sin_ref = next(it)
        next(it)
    x1_ref = next(it); h2_ref = next(it); idx_ref = next(it); wts_ref = next(it); cnt_ref = next(it)
    if emit_kv:
        newk_ref = next(it); newv_ref = next(it)
    q1_scr = next(it); q2_scr = next(it); kall = next(it); vall = next(it)
    f_scr = next(it); stk = next(it); mix = next(it)

    d = x_ref.shape[-1]
    qk_w = heads * LANES

    @pl.when(pl.program_id(0) == 0)
    def _():
        cnt_ref[...] = jnp.zeros(cnt_ref.shape, F32)
    rc = min(ROW_CHUNK, n)
    n_chunks = n // rc

    def mod_row(j):
        return mod_ref[0, :, j * d:(j + 1) * d]

    shift1, scale1, gate1 = mod_row(0), mod_row(1), mod_row(2)
    shift2, scale2, gate2 = mod_row(3), mod_row(4), mod_row(5)
    del gate2

    lv = lamv_ref[...]
    lam = (jnp.exp(jnp.sum(lv[0:1] * lv[1:2], axis=-1, keepdims=True))
           - jnp.exp(jnp.sum(lv[2:3] * lv[3:4], axis=-1, keepdims=True)) + LAMBDA_INIT)

    if rope:
        for hd in range(heads):
            kall[hd, 0:n_cache, :] = ck_ref[0, hd].astype(BF16)
            vall[hd, 0:n_cache, :] = cv_ref[0, hd].astype(BF16)

    lane = lax.broadcasted_iota(jnp.int32, (rc, LANES), 1)
    first_map = lane < (LANES // 2)

    first_of_pair = jnp.bitwise_and(lane, 31) < 16

    def rotate(t, cos, sin):
        partner = jnp.where(first_of_pair, pltpu.roll(t, LANES - 16, 1), pltpu.roll(t, 16, 1))
        return t * cos + partner * sin

    def project(c):
        r0 = pl.multiple_of(c * rc, rc)
        x = x_ref[0, pl.ds(r0, rc), :]
        h = (x * _rsqrt_mean_sq(x) * g1_ref[...]) * (1.0 + scale1) + shift1
        p = jnp.dot(h.astype(BF16), win_ref[...], preferred_element_type=F32)
        if rope:
            cos = cos_ref[pl.ds(r0, rc), :]
            sin = sin_ref[pl.ds(r0, rc), :]
        for hd in range(heads):
            qh = p[:, hd * LANES:(hd + 1) * LANES]
            kh = p[:, qk_w + hd * LANES:qk_w + (hd + 1) * LANES]
            vh = p[:, 2 * qk_w + hd * LANES:2 * qk_w + (hd + 1) * LANES]
            if rope:
                qh = rotate(qh, cos, sin)
                kh = rotate(kh, cos, sin)
            if emit_kv:
                newk_ref[0, 0, hd, pl.ds(r0, rc), :] = kh
                newv_ref[0, 0, hd, pl.ds(r0, rc), :] = vh
            qs = qh * (LANES // 2) ** -0.5
            q1_scr[pl.ds(r0, rc), hd * LANES:(hd + 1) * LANES] = jnp.where(first_map, qs, 0.0).astype(BF16)
            q2_scr[pl.ds(r0, rc), hd * LANES:(hd + 1) * LANES] = jnp.where(first_map, 0.0, qs).astype(BF16)
            kall[hd, pl.ds(n_cache + r0, rc), :] = kh.astype(BF16)
            vall[hd, pl.ds(n_cache + r0, rc), :] = vh.astype(BF16)
        f_scr[pl.ds(r0, rc), :] = p[:, 3 * qk_w:].astype(BF16)

    _loop(n_chunks, project)

    contract_last = (((1,), (1,)), ((), ()))

    def softmax(s):
        e = jnp.exp(s - jnp.max(s, axis=-1, keepdims=True))
        return e * (1.0 / jnp.sum(e, axis=-1, keepdims=True))

    for hd in range(heads):
        def attend(c, hd=hd):
            r0 = pl.multiple_of(c * rc, rc)
            kh = kall[hd]
            s1 = lax.dot_general(q1_scr[pl.ds(r0, rc), hd * LANES:(hd + 1) * LANES], kh,
                                 contract_last, preferred_element_type=F32)
            s2 = lax.dot_general(q2_scr[pl.ds(r0, rc), hd * LANES:(hd + 1) * LANES], kh,
                                 contract_last, preferred_element_type=F32)
            a = softmax(s1) - lam * softmax(s2)
            o = jnp.dot(a.astype(BF16), vall[hd], preferred_element_type=F32)
            o = o * _rsqrt_mean_sq(o) * subg_ref[...] * (1.0 - LAMBDA_INIT)
            mix[pl.ds(r0, rc), hd * LANES:(hd + 1) * LANES] = o.astype(BF16)

        _loop(n_chunks, attend)

    def dft_channels(c):
        r0 = pl.multiple_of(c * rc, rc)
        for g in range(N_FGROUPS):
            a = jnp.dot(f_scr[pl.ds(r0, rc), g * LANES:(g + 1) * LANES], fcs_ref[...],
                        preferred_element_type=F32)
            stk[pl.ds(r0, rc), g * LANES:(g + 1) * LANES] = a[:, :LANES].astype(BF16)
            stk[pl.ds(pl.multiple_of(n + r0, rc), rc), g * LANES:(g + 1) * LANES] = (
                a[:, LANES:].astype(BF16))

    _loop(n_chunks, dft_channels)

    fscale = 1.0 / math.sqrt(n * LANES)

    def dft_positions(c):
        r0 = pl.multiple_of(c * rc, rc)
        y = jnp.dot(dftn_ref[pl.ds(r0, rc), :], stk[...], preferred_element_type=F32) * fscale
        mix[pl.ds(r0, rc), qk_w:] = y.astype(BF16)

    _loop(n_chunks, dft_positions)

    klane = lax.broadcasted_iota(jnp.int32, (rc, LANES), 1)
    neg_inf = jnp.float32(-jnp.inf)

    def tail(c):
        r0 = pl.multiple_of(c * rc, rc)
        x = x_ref[0, pl.ds(r0, rc), :]
        mixed = jnp.dot(mix[pl.ds(r0, rc), :], wout_ref[...], preferred_element_type=F32)
        x1 = x + gate1 * mixed
        x1_ref[pl.ds(r0, rc), :] = x1
        h2 = (x1 * _rsqrt_mean_sq(x1) * g2_ref[...]) * (1.0 + scale2) + shift2
        tpr = d // LANES
        for j in range(tpr):
            h2_ref[pl.ds(r0 * tpr + j, rc, stride=tpr), :] = h2[:, j * LANES:(j + 1) * LANES]
        hi = h2.astype(BF16)
        lo = (h2 - hi.astype(F32)).astype(BF16)
        logits = (jnp.dot(hi, rwh_ref[...], preferred_element_type=F32)
                  + jnp.dot(lo, rwh_ref[...], preferred_element_type=F32)
                  + jnp.dot(hi, rwl_ref[...], preferred_element_type=F32))
        l = jnp.where(klane < n_experts, logits + rb_ref[...], neg_inf)
        vals, ids = [], []
        for _ in range(TOP_K):
            m = jnp.max(l, axis=-1, keepdims=True)
            cand = jnp.where(l == m, klane, LANES).astype(F32)
            i = jnp.min(cand, axis=-1, keepdims=True).astype(jnp.int32)
            vals.append(m)
            ids.append(i)
            l = jnp.where(klane == i, neg_inf, l)
        es = [jnp.exp(v - vals[0]) for v in vals]
        inv = 1.0 / functools.reduce(lambda a, b: a + b, es)
        idx_out = jnp.zeros((rc, LANES), jnp.int32)
        wts_out = jnp.zeros((rc, LANES), F32)
        for k in range(TOP_K):
            idx_out = jnp.where(klane == k, ids[k], idx_out)
            wts_out = jnp.where(klane == k, es[k] * inv, wts_out)
        idx_ref[pl.ds(r0, rc), :] = idx_out
        wts_ref[pl.ds(r0, rc), :] = wts_out
        hits = functools.reduce(lambda a, b: a + b,
                                [jnp.where(klane == i, 1.0, 0.0) for i in ids])
        cnt_ref[...] += jnp.sum(hits, axis=0, keepdims=True)

    _loop(n_chunks, tail)


def _const_spec(shape):
    nd = len(shape)
    return pl.BlockSpec(shape, lambda b: (0,) * nd, pipeline_mode=pl.Buffered(1))


def _layer(x, mod, g1, win, lamv, subg, fcs, dftn, wout, g2, rwh, rwl, rb, *, n_experts,
           mod_row0, mod_row_step, h2_rows, h2_block0, zero_blocks=0, cache=None, h2_buf=None):
    bsz, n, d = x.shape
    heads = win.shape[1] // (4 * LANES)
    rope = cache is not None
    n_cache = cache[0].shape[2] if rope else 0
    nk = n_cache + n
    live = lambda b: jnp.minimum(b, bsz - 1)
    in_specs = [
        pl.BlockSpec((1, n, d), lambda b: (live(b), 0, 0)),
        pl.BlockSpec((1, 1, mod.shape[-1]),
                     lambda b: (mod_row0 + mod_row_step * live(b), 0, 0)),
        _const_spec(g1.shape), _const_spec(win.shape), _const_spec(lamv.shape),
        _const_spec(subg.shape), _const_spec(fcs.shape), _const_spec(dftn.shape),
        _const_spec(wout.shape), _const_spec(g2.shape), _const_spec(rwh.shape),
        _const_spec(rwl.shape), _const_spec(rb.shape),
    ]
    args = [x, mod, g1, win, lamv, subg, fcs, dftn, wout, g2, rwh, rwl, rb]
    out_specs = [
        pl.BlockSpec((n, d), lambda b: (live(b), 0)),
        pl.BlockSpec((n * (d // LANES), LANES), lambda b: (h2_block0 + b, 0)),
        pl.BlockSpec((n, LANES), lambda b: (live(b), 0)),
        pl.BlockSpec((n, LANES), lambda b: (live(b), 0)),
        pl.BlockSpec((1, LANES), lambda b: (0, 0)),
    ]
    out_shape = [
        jax.ShapeDtypeStruct((bsz * n, d), F32),
        jax.ShapeDtypeStruct((h2_rows * (d // LANES), LANES), F32),
        jax.ShapeDtypeStruct((bsz * n, LANES), jnp.int32),
        jax.ShapeDtypeStruct((bsz * n, LANES), F32),
        jax.ShapeDtypeStruct((1, LANES), F32),
    ]
    aliases = {}
    if rope:
        ck, cv, cos, sin = cache
        in_specs += [
            pl.BlockSpec((1, heads, n_cache, LANES), lambda b: (live(b), 0, 0, 0)),
            pl.BlockSpec((1, heads, n_cache, LANES), lambda b: (live(b), 0, 0, 0)),
            _const_spec(cos.shape), _const_spec(sin.shape),
            pl.BlockSpec(memory_space=pl.ANY),
        ]
        args += [ck, cv, cos, sin, h2_buf]
        aliases = {len(args) - 1: 1}
    else:
        kv_spec = pl.BlockSpec((1, 1, heads, n, LANES), lambda b: (live(b), 0, 0, 0, 0))
        out_specs += [kv_spec, kv_spec]
        kv_shape = jax.ShapeDtypeStruct((bsz, 1, heads, n, LANES), F32)
        out_shape += [kv_shape, kv_shape]
    scratch = [
        pltpu.VMEM((n, heads * LANES), BF16),
        pltpu.VMEM((n, heads * LANES), BF16),
        pltpu.VMEM((heads, nk, LANES), BF16),
        pltpu.VMEM((heads, nk, LANES), BF16),
        pltpu.VMEM((n, N_FGROUPS * LANES), BF16),
        pltpu.VMEM((2 * n, N_FGROUPS * LANES), BF16),
        pltpu.VMEM((n, d), BF16),
    ]
    kern = functools.partial(_layer_kernel, n_live=bsz if zero_blocks else None,
                             n_inputs=len(args), n=n, n_cache=n_cache, heads=heads,
                             n_experts=n_experts, rope=rope, emit_kv=not rope)
    return pl.pallas_call(
        kern,
        grid=(bsz + zero_blocks,),
        in_specs=in_specs,
        out_specs=out_specs,
        out_shape=out_shape,
        scratch_shapes=scratch,
        input_output_aliases=aliases,
        compiler_params=pltpu.CompilerParams(dimension_semantics=("arbitrary",),
                                             vmem_limit_bytes=VMEM_LIMIT),
        name="layer_latent" if rope else "layer_context",
    )(*args)


def _moe_kernel(texp_ref, next_ref, nvalid_ref, tok_ref, tok_next_ref, tok_next2_ref, dst_ref,
                dst_prev_ref, h2_hbm, wgu_hbm, bgu_ref, wd_hbm, bd_ref, y_hbm, xbuf0, xbuf1, xbuf2,
                ybuf0, ybuf1, ybuf2, wgu_f32, wd_f32, wgu_bf, wd_bf, gsem, ssem, wsem, run_ref, *,
                tm):
    i = pl.program_id(0)
    nv = nvalid_ref[0]
    d_ff, d = wd_bf.shape
    tpr = d // LANES

    def weight_copies(e, s):
        return (pltpu.make_async_copy(wgu_hbm.at[e], wgu_f32.at[s], wsem.at[s, 0]),
                pltpu.make_async_copy(wd_hbm.at[e], wd_f32.at[s], wsem.at[s, 1]))

    def token_rows(t):
        return pl.ds(pl.multiple_of(t * tpr, tpr), tpr)

    xbuf, ybuf = (xbuf0, xbuf1, xbuf2), (ybuf0, ybuf1, ybuf2)

    def gather_row(idx_ref, slot, r):
        return pltpu.make_async_copy(h2_hbm.at[token_rows(idx_ref[0, 0, r]), :],
                                     xbuf[slot].at[pl.ds(r * tpr, tpr), :], gsem.at[slot])

    def scatter_row(idx_ref, slot, r):
        return pltpu.make_async_copy(ybuf[slot].at[pl.ds(r * tpr, tpr), :],
                                     y_hbm.at[token_rows(idx_ref[0, 0, r]), :], ssem.at[slot])

    def step(slot):
        ahead, behind = (slot + 1) % MOE_RING, (slot + 2) % MOE_RING
        for r in range(tm):
            gather_row(tok_ref, slot, r).wait()

        @pl.when(i >= 2)
        def _():
            for r in range(tm):
                scatter_row(dst_ref, slot, r).wait()

        for r in range(tm):
            gather_row(tok_next2_ref, behind, r).start()
        for r in range(tm):
            scatter_row(dst_prev_ref, behind, r).start(priority=r % 2)

        x = jnp.concatenate(
            [xbuf[slot][pl.ds(j, tm, stride=tpr), :].astype(BF16) for j in range(tpr)], axis=1)
        gu = jnp.dot(x, wgu_bf[...], preferred_element_type=F32) + bgu_ref[0]
        glu = jnp.minimum(gu[:, :d_ff], SWIGLU_LIMIT)
        lin = jnp.clip(gu[:, d_ff:], -SWIGLU_LIMIT, SWIGLU_LIMIT)
        act = glu * _sigmoid(SWIGLU_ALPHA * glu) * (lin + 1.0)
        y = jnp.dot(act.astype(BF16), wd_bf[...], preferred_element_type=F32) + bd_ref[0]
        for j in range(tpr):
            ybuf[slot][pl.ds(j, tm, stride=tpr), :] = y[:, j * LANES:(j + 1) * LANES]

        @pl.when(i == nv - 1)
        def _():
            for r in range(tm):
                scatter_row(dst_ref, slot, r).start()
            for r in range(tm):
                scatter_row(dst_ref, slot, r).wait()
            for r in range(tm):
                scatter_row(dst_prev_ref, behind, r).wait()

            @pl.when(i >= 1)
            def _():
                for r in range(tm):
                    scatter_row(dst_prev_ref, ahead, r).wait()

            for r in range(tm):
                gather_row(tok_next_ref, ahead, r).wait()
            for r in range(tm):
                gather_row(tok_next2_ref, behind, r).wait()

    @pl.when(i < nv)
    def _():
        @pl.when(i == 0)
        def _():
            run_ref[0] = 0
            for c in weight_copies(texp_ref[0], 0):
                c.start(priority=1)
            for r in range(tm):
                gather_row(tok_ref, 0, r).start()
            for r in range(tm):
                gather_row(tok_next_ref, 1, r).start()
            spare0 = y_hbm.shape[0] - 2 * tm * tpr
            for s in range(MOE_RING):
                ybuf[s][...] = jnp.zeros(ybuf[s].shape, F32)
            fills = [pltpu.make_async_copy(
                ybuf[s], y_hbm.at[pl.ds(spare0 + s * tm * tpr, tm * tpr), :], ssem.at[s])
                for s in range(2)]
            for f in fills:
                f.start()
            for f in fills:
                f.wait()

        @pl.when(jnp.logical_or(i == 0, texp_ref[i] != texp_ref[jnp.maximum(i - 1, 0)]))
        def _():
            run = run_ref[0]
            ws = run % 2
            for c in weight_copies(texp_ref[i], ws):
                c.wait()

            @pl.when(next_ref[i] >= 0)
            def _():
                for c in weight_copies(next_ref[i], 1 - ws):
                    c.start(priority=1)

            wgu_bf[...] = wgu_f32[ws].astype(BF16)
            wd_bf[...] = wd_f32[ws].astype(BF16)
            run_ref[0] = run + 1

        for s in range(MOE_RING):
            pl.when(i % MOE_RING == s)(functools.partial(step, s))


def _moe(h2, tile_expert, next_expert, n_valid, src_tok, dest, wgu, bgu, wd, bd):
    n_exp, d, two_f = wgu.shape
    tpr = d // LANES
    t_rows = h2.shape[0] // tpr
    d_ff = two_f // 2
    n_tiles = tile_expert.shape[0]
    tm = src_tok.shape[-1]
    last = n_tiles - 1
    smem_tile = functools.partial(pl.BlockSpec, (1, 1, tm), memory_space=pltpu.SMEM)
    grid_spec = pltpu.PrefetchScalarGridSpec(
        num_scalar_prefetch=3,
        grid=(n_tiles,),
        in_specs=[
            smem_tile(lambda i, te, nx, nv: (i, 0, 0)),
            smem_tile(lambda i, te, nx, nv: (jnp.minimum(i + 1, last), 0, 0)),
            smem_tile(lambda i, te, nx, nv: (jnp.minimum(i + 2, last), 0, 0)),
            smem_tile(lambda i, te, nx, nv: (i + 1, 0, 0)),
            smem_tile(lambda i, te, nx, nv: (i, 0, 0)),
            pl.BlockSpec(memory_space=pl.ANY),
            pl.BlockSpec(memory_space=pl.ANY),
            pl.BlockSpec((1, 1, two_f), lambda i, te, nx, nv: (te[i], 0, 0)),
            pl.BlockSpec(memory_space=pl.ANY),
            pl.BlockSpec((1, 1, d), lambda i, te, nx, nv: (te[i], 0, 0)),
        ],
        out_specs=pl.BlockSpec(memory_space=pl.ANY),
        scratch_shapes=[
            *([pltpu.VMEM((tm * tpr, LANES), F32)] * (2 * MOE_RING)),
            pltpu.VMEM((2, d, two_f), F32),
            pltpu.VMEM((2, d_ff, d), F32),
            pltpu.VMEM((d, two_f), BF16),
            pltpu.VMEM((d_ff, d), BF16),
            pltpu.SemaphoreType.DMA((MOE_RING,)),
            pltpu.SemaphoreType.DMA((MOE_RING,)),
            pltpu.SemaphoreType.DMA((2, 2)),
            pltpu.SMEM((1,), jnp.int32),
        ],
    )
    return pl.pallas_call(
        functools.partial(_moe_kernel, tm=tm),
        grid_spec=grid_spec,
        out_shape=jax.ShapeDtypeStruct(((TOP_K * t_rows + 2 * tm) * tpr, LANES), F32),
        compiler_params=pltpu.CompilerParams(dimension_semantics=("arbitrary",),
                                             vmem_limit_bytes=VMEM_LIMIT),
        name="routed_moe",
    )(tile_expert, next_expert, n_valid, src_tok, src_tok, src_tok, dest, dest, h2, wgu,
      bgu.reshape(n_exp, 1, two_f), wd, bd.reshape(n_exp, 1, d))


def _route(idx, counts, tm, n_tiles):
    t_rows, top_k = idx.shape
    n_exp = counts.shape[0]
    n_pairs = t_rows * top_k
    pad_bit = 16
    assert n_pairs <= 1 << pad_bit and tm <= 1 << pad_bit and n_tiles * tm == n_pairs + n_exp * tm
    pair_ids = np.arange(n_pairs, dtype=np.int32)
    real_keys = (idx.T.reshape(-1) << (pad_bit + 1)) | pair_ids
    pad_e = np.repeat(np.arange(n_exp, dtype=np.int32), tm)
    pad_j = np.tile(np.arange(tm, dtype=np.int32), n_exp)
    n_pad = (-counts) % tm
    unused = n_exp << (pad_bit + 1)
    pad_keys = jnp.where(pad_j < jnp.repeat(n_pad, tm),
                         (pad_e << (pad_bit + 1)) | (1 << pad_bit) | pad_j, unused)
    keys = jnp.sort(jnp.concatenate([real_keys, pad_keys]))
    is_real = jnp.logical_and((keys >> pad_bit) & 1 == 0, keys < unused)
    pair = keys & ((1 << pad_bit) - 1)
    slot = sum((pair >= k * t_rows).astype(jnp.int32) for k in range(1, top_k))
    pos = np.arange(n_tiles * tm, dtype=np.int32)
    spare = n_pairs + pos % (2 * tm)
    dest = jnp.concatenate([spare[:tm], jnp.where(is_real, pair, spare)])
    src_tok = jnp.where(is_real, pair - slot * t_rows, 0)
    n_valid = jnp.sum(counts + n_pad) // tm
    last_e = jnp.max(jnp.where(counts > 0, jnp.arange(n_exp, dtype=jnp.int32), 0))
    tile_expert = jnp.where(np.arange(n_tiles) < n_valid, keys[::tm] >> (pad_bit + 1), last_e)
    experts = jnp.arange(n_exp, dtype=jnp.int32)
    owner = jnp.where(counts > 0, experts, n_exp)
    following = jnp.concatenate([lax.cummin(owner, reverse=True)[1:],
                                 jnp.full((1,), n_exp, jnp.int32)])
    following = jnp.where(following >= n_exp, -1, following)
    next_expert = jnp.sum(jnp.where(tile_expert[:, None] == experts[None, :], following[None, :], 0),
                          axis=1)
    return (tile_expert.astype(jnp.int32), next_expert.astype(jnp.int32),
            n_valid.reshape(1).astype(jnp.int32), src_tok.reshape(n_tiles, 1, tm),
            dest.reshape(n_tiles + 1, 1, tm))


def _combine_kernel(x1_ref, y0_ref, y1_ref, y2_ref, y3_ref, wts_ref, gate_ref, fg_ref, o_ref):
    w = wts_ref[...]
    tc, d = x1_ref.shape
    tpr = d // LANES
    cols = []
    for j in range(tpr):
        acc = w[:, 0:1] * y0_ref[pl.ds(j, tc, stride=tpr), :]
        for k, y_ref in enumerate((y1_ref, y2_ref, y3_ref), start=1):
            acc = acc + w[:, k:k + 1] * y_ref[pl.ds(j, tc, stride=tpr), :]
        cols.append(acc)
    x = x1_ref[...] + gate_ref[0] * jnp.concatenate(cols, axis=1)
    o_ref[...] = x * _rsqrt_mean_sq(x) * fg_ref[...]


def _combine(x1, y, wts, gate2, final_g, *, t_all, row0, rows_per_gate):
    rows, d = x1.shape
    tc = COMBINE_TILE
    y_specs = [
        pl.BlockSpec((tc * (d // LANES), LANES), functools.partial(
            lambda i, k: ((k * t_all + row0) // tc + i, 0), k=k))
        for k in range(TOP_K)
    ]
    return pl.pallas_call(
        _combine_kernel,
        grid=(rows // tc,),
        in_specs=[pl.BlockSpec((tc, d), lambda i: (i, 0))] + y_specs + [
            pl.BlockSpec((tc, LANES), lambda i: (i, 0)),
            pl.BlockSpec((1, 1, d), lambda i: ((i * tc) // rows_per_gate, 0, 0)),
            pl.BlockSpec((1, d), lambda i: (0, 0)),
        ],
        out_specs=pl.BlockSpec((tc, d), lambda i: (i, 0)),
        out_shape=jax.ShapeDtypeStruct((rows, d), F32),
        compiler_params=pltpu.CompilerParams(dimension_semantics=("arbitrary",)),
        name="combine",
    )(x1, y, y, y, y, wts, gate2, final_g)


def _dft_tables(n):
    def angles(m):
        k = np.arange(m, dtype=np.int64)
        return (2.0 * np.pi / m) * ((k[:, None] * k[None, :]) % m)
    an = angles(n)
    ac = angles(LANES)
    dftn = np.concatenate([np.cos(an), -np.sin(an)], axis=1).astype(np.float32)
    fcs = np.concatenate([np.cos(ac), np.sin(ac)], axis=1).astype(np.float32)
    return jnp.asarray(dftn).astype(BF16), jnp.asarray(fcs).astype(BF16)


def _rope_tables(n, qk_dim):
    quarter = qk_dim // 4
    tok = np.arange(n)
    pos = np.stack([tok // GRID_W, tok % GRID_W], axis=-1).astype(np.float64)
    freqs = ROPE_THETA ** (-np.arange(quarter, dtype=np.float64) / quarter)
    ang = (pos[:, :, None] * freqs).reshape(n, 2 * quarter)
    cos, sin = np.cos(ang), np.sin(ang)
    row_c, col_c = cos[:, :quarter], cos[:, quarter:]
    row_s, col_s = sin[:, :quarter], sin[:, quarter:]
    cos_map = np.concatenate([row_c, row_c, col_c, col_c], axis=-1)
    sin_map = np.concatenate([-row_s, row_s, -col_s, col_s], axis=-1)
    reps = LANES // qk_dim
    return (np.tile(cos_map, (1, reps)).astype(np.float32),
            np.tile(sin_map, (1, reps)).astype(np.float32))


def kernel(x_prompt, x_sample, cache_k, cache_v, c, c_ctx, w_mod, b_mod, norm1_g, w_in, lambda_q1,
           lambda_k1, lambda_q2, lambda_k2, subln_g, w_out, norm2_g, router_w, router_b, w_gate_up,
           b_gate_up, w_down, b_down, final_g):
    bsz, seq, d = x_prompt.shape
    dec_b, dec_seq, _ = x_sample.shape
    heads, past, qk_dim = cache_k.shape[2], cache_k.shape[3], cache_k.shape[5]
    n_exp = router_w.shape[-1]
    t_ctx, t_den = bsz * seq, dec_b * dec_seq
    t_all = t_ctx + t_den
    assert t_ctx % dec_seq == 0 and 2 * qk_dim == LANES and dec_seq % GRID_W == 0
    assert (t_all * TOP_K) % MOE_TILE == 0 and t_all % COMBINE_TILE == 0

    cvec = jnp.concatenate([c_ctx[None, :], c, jnp.zeros((8 - 1 - dec_b, d), F32)], axis=0)
    mod = _modulation(cvec, w_mod[0], b_mod[0])[:, None, :]

    win = w_in[0].astype(BF16)
    wout = w_out[0].astype(BF16)
    rw = jnp.pad(router_w[0], ((0, 0), (0, LANES - n_exp)))
    rwh = rw.astype(BF16)
    rwl = (rw - rwh.astype(F32)).astype(BF16)
    rb = jnp.pad(router_b[0], (0, LANES - n_exp)).reshape(1, LANES)
    lamv = jnp.stack([lambda_q1[0], lambda_k1[0], lambda_q2[0], lambda_k2[0]], axis=0)
    g1 = norm1_g[0].reshape(1, d)
    g2 = norm2_g[0].reshape(1, d)
    subg = subln_g[0].reshape(1, LANES)
    dft_ctx, fcs = _dft_tables(seq)
    dft_den, _ = _dft_tables(dec_seq)
    cos, sin = _rope_tables(dec_seq, qk_dim)

    shared = (g1, win, lamv, subg, fcs)
    tail = (wout, g2, rwh, rwl, rb)
    x1_ctx, h2_all, idx_ctx, wts_ctx, cnt_ctx, new_k, new_v = _layer(
        x_prompt, mod, *shared, dft_ctx, *tail, n_experts=n_exp, mod_row0=0, mod_row_step=0,
        h2_rows=t_all, h2_block0=0, zero_blocks=t_den // seq)
    ck = cache_k[:, 0].reshape(dec_b, heads, past, LANES)
    cv = cache_v[:, 0]
    x1_den, h2_all, idx_den, wts_den, cnt_den = _layer(
        x_sample, mod, *shared, dft_den, *tail, n_experts=n_exp, mod_row0=1, mod_row_step=1,
        h2_rows=t_all, h2_block0=t_ctx // dec_seq, cache=(ck, cv, cos, sin), h2_buf=h2_all)

    idx = jnp.concatenate([idx_ctx[:, :TOP_K], idx_den[:, :TOP_K]], axis=0)
    n_tiles = (t_all * TOP_K) // MOE_TILE + n_exp
    counts = (cnt_ctx + cnt_den)[0, :n_exp].astype(jnp.int32)
    tile_expert, next_expert, n_valid, src_tok, dest = _route(idx, counts, MOE_TILE, n_tiles)
    y = _moe(h2_all, tile_expert, next_expert, n_valid, src_tok, dest, w_gate_up[0], b_gate_up[0],
             w_down[0], b_down[0])

    gate2 = mod[:, :, 5 * d:]
    fg = final_g.reshape(1, d)
    y_prompt = _combine(x1_ctx, y, wts_ctx, gate2[0:1], fg, t_all=t_all, row0=0,
                        rows_per_gate=t_ctx)
    y_sample = _combine(x1_den, y, wts_den, gate2[1:1 + dec_b], fg, t_all=t_all, row0=t_ctx,
                        rows_per_gate=dec_seq)
    return (y_prompt.reshape(bsz, seq, d), y_sample.reshape(dec_b, dec_seq, d),
            new_k.reshape(bsz, 1, heads, seq, 2, qk_dim), new_v)
```

```python
import functools
import math

import numpy as np
import jax
import jax.numpy as jnp
from jax import lax
from jax.experimental import pallas as pl
from jax.experimental.pallas import tpu as pltpu

F32 = jnp.float32
BF16 = jnp.bfloat16

GRID_W = 64
N_FGROUPS = 4
TOP_K = 4
SWIGLU_LIMIT = 7.0
SWIGLU_ALPHA = 1.702
ROPE_THETA = 10000.0
NORM_EPS = 1e-6
LAMBDA_INIT = 0.8 - 0.6 * math.exp(-0.3 * 0)

LANES = 128
ROW_CHUNK = 256
CTX_SEQS = 2
MOE_TILE = 256
MOE_RING = 3
COMBINE_TILE = 256
VMEM_LIMIT = 56 * 1024 * 1024


def _rsqrt_mean_sq(x):
    return lax.rsqrt(jnp.mean(x * x, axis=-1, keepdims=True) + NORM_EPS)


def _sigmoid(z):
    return 1.0 / (1.0 + jnp.exp(-z))


def _mod_kernel(c_ref, w_ref, b_ref, o_ref):
    c = c_ref[...]
    s = c * _sigmoid(c)
    o_ref[...] = jnp.dot(s.astype(BF16), w_ref[...].astype(BF16),
                         preferred_element_type=F32) + b_ref[...]


def _modulation(cvec, w_mod, b_mod):
    rows, d = cvec.shape
    n_out = w_mod.shape[1]
    return pl.pallas_call(
        _mod_kernel,
        grid=(n_out // d,),
        in_specs=[
            pl.BlockSpec((rows, d), lambda j: (0, 0)),
            pl.BlockSpec((d, d), lambda j: (0, j)),
            pl.BlockSpec((1, d), lambda j: (0, j)),
        ],
        out_specs=pl.BlockSpec((rows, d), lambda j: (0, j)),
        out_shape=jax.ShapeDtypeStruct((rows, n_out), F32),
        name="modulation",
    )(cvec, w_mod, b_mod.reshape(1, n_out))


def _loop(n, body):
    if n == 1:
        body(0)
    else:
        def step(i, carry):
            body(i)
            return carry
        lax.fori_loop(0, n, step, 0)


def _layer_kernel(*refs, n_live, n_inputs, **static):
    if n_live is None:
        _layer_body(*refs, **static)
        return
    b = pl.program_id(0)
    h2_ref = refs[n_inputs + 1]
    pl.when(b < n_live)(functools.partial(_layer_body, *refs, **static))

    @pl.when(b >= n_live)
    def _():
        h2_ref[...] = jnp.zeros(h2_ref.shape, F32)


def _layer_body(*refs, n, seqs, n_cache, heads, n_experts, rope, emit_kv):
    it = iter(refs)
    x_ref = next(it); mod_ref = next(it); g1_ref = next(it); win_ref = next(it)
    lamv_ref = next(it); subg_ref = next(it); fcs_ref = next(it); dftn_ref = next(it)
    wout_ref = next(it); g2_ref = next(it); rwh_ref = next(it); rwl_ref = next(it); rb_ref = next(it)
    if rope:
        ck_ref = next(it); cv_ref = next(it); cos_ref = next(it); sin_ref = next(it)
        next(it)
    x1_ref = next(it); h2_ref = next(it); idx_ref = next(it); wts_ref = next(it); cnt_ref = next(it)
    if emit_kv:
        newk_ref = next(it); newv_ref = next(it)
    q1_scr = next(it); q2_scr = next(it); kall = next(it); vall = next(it)
    f_scr = next(it); stk = next(it); mix = next(it)

    d = x_ref.shape[-1]
    qk_w = heads * LANES

    @pl.when(pl.program_id(0) == 0)
    def _():
        cnt_ref[...] = jnp.zeros(cnt_ref.shape, F32)
    rc = min(ROW_CHUNK, n)
    n_chunks = n // rc

    def mod_row(j):
        return mod_ref[0, :, j * d:(j + 1) * d]

    shift1, scale1, gate1 = mod_row(0), mod_row(1), mod_row(2)
    shift2, scale2, gate2 = mod_row(3), mod_row(4), mod_row(5)
    del gate2

    lv = lamv_ref[...]
    lam = (jnp.exp(jnp.sum(lv[0:1] * lv[1:2], axis=-1, keepdims=True))
           - jnp.exp(jnp.sum(lv[2:3] * lv[3:4], axis=-1, keepdims=True)) + LAMBDA_INIT)

    if rope:
        for hd in range(heads):
            kall[0, hd, 0:n_cache, :] = ck_ref[0, hd].astype(BF16)
            vall[0, hd, 0:n_cache, :] = cv_ref[0, hd].astype(BF16)

    lane = lax.broadcasted_iota(jnp.int32, (rc, LANES), 1)
    first_map = lane < (LANES // 2)

    first_of_pair = jnp.bitwise_and(lane, 31) < 16

    def rotate(t, cos, sin):
        partner = jnp.where(first_of_pair, pltpu.roll(t, LANES - 16, 1), pltpu.roll(t, 16, 1))
        return t * cos + partner * sin

    for seq in range(seqs):
        _layer_sequence(seq, locals())


def _layer_sequence(seq, env):
    (x_ref, g1_ref, win_ref, subg_ref, fcs_ref, dftn_ref, wout_ref, g2_ref, rwh_ref, rwl_ref, rb_ref,
     x1_ref, h2_ref, idx_ref, wts_ref, cnt_ref, n, n_cache, heads, n_experts, rope, emit_kv, d,
     qk_w, rc, n_chunks, shift1, scale1, gate1, shift2, scale2, lam, first_map, rotate) = (
        env[k] for k in (
            "x_ref g1_ref win_ref subg_ref fcs_ref dftn_ref wout_ref g2_ref rwh_ref rwl_ref rb_ref "
            "x1_ref h2_ref idx_ref wts_ref cnt_ref n n_cache heads n_experts rope emit_kv d "
            "qk_w rc n_chunks shift1 scale1 gate1 shift2 scale2 lam first_map rotate").split())
    cos_ref, sin_ref = env.get("cos_ref"), env.get("sin_ref")
    newk_ref, newv_ref = env.get("newk_ref"), env.get("newv_ref")
    xs = x_ref.at[seq]
    q1s, q2s, ks, vs = (env[k].at[seq] for k in ("q1_scr", "q2_scr", "kall", "vall"))
    fs, stks, mixs = (env[k].at[seq] for k in ("f_scr", "stk", "mix"))
    row0 = seq * n

    def project(c):
        r0 = pl.multiple_of(c * rc, rc)
        x = xs[pl.ds(r0, rc), :]
        h = (x * _rsqrt_mean_sq(x) * g1_ref[...]) * (1.0 + scale1) + shift1
        p = jnp.dot(h.astype(BF16), win_ref[...], preferred_element_type=F32)
        if rope:
            cos = cos_ref[pl.ds(r0, rc), :]
            sin = sin_ref[pl.ds(r0, rc), :]
        for hd in range(heads):
            qh = p[:, hd * LANES:(hd + 1) * LANES]
            kh = p[:, qk_w + hd * LANES:qk_w + (hd + 1) * LANES]
            vh = p[:, 2 * qk_w + hd * LANES:2 * qk_w + (hd + 1) * LANES]
            if rope:
                qh = rotate(qh, cos, sin)
                kh = rotate(kh, cos, sin)
            if emit_kv:
                newk_ref[seq, 0, hd, pl.ds(r0, rc), :] = kh
                newv_ref[seq, 0, hd, pl.ds(r0, rc), :] = vh
            qs = qh * (LANES // 2) ** -0.5
            head = slice(hd * LANES, (hd + 1) * LANES)
            q1s[pl.ds(r0, rc), head] = jnp.where(first_map, qs, 0.0).astype(BF16)
            q2s[pl.ds(r0, rc), head] = jnp.where(first_map, 0.0, qs).astype(BF16)
            ks[hd, pl.ds(n_cache + r0, rc), :] = kh.astype(BF16)
            vs[hd, pl.ds(n_cache + r0, rc), :] = vh.astype(BF16)
        fs[pl.ds(r0, rc), :] = p[:, 3 * qk_w:].astype(BF16)

    _loop(n_chunks, project)

    contract_last = (((1,), (1,)), ((), ()))

    def softmax(s):
        e = jnp.exp(s - jnp.max(s, axis=-1, keepdims=True))
        return e * (1.0 / jnp.sum(e, axis=-1, keepdims=True))

    def attend(c):
        r0 = pl.multiple_of(c * rc, rc)
        for hd in range(heads):
            kh = ks[hd]
            s1 = lax.dot_general(q1s[pl.ds(r0, rc), hd * LANES:(hd + 1) * LANES], kh,
                                 contract_last, preferred_element_type=F32)
            s2 = lax.dot_general(q2s[pl.ds(r0, rc), hd * LANES:(hd + 1) * LANES], kh,
                                 contract_last, preferred_element_type=F32)
            a = softmax(s1) - lam * softmax(s2)
            o = jnp.dot(a.astype(BF16), vs[hd], preferred_element_type=F32)
            o = o * _rsqrt_mean_sq(o) * subg_ref[...] * (1.0 - LAMBDA_INIT)
            mixs[pl.ds(r0, rc), hd * LANES:(hd + 1) * LANES] = o.astype(BF16)

    _loop(n_chunks, attend)

    def dft_channels(c):
        r0 = pl.multiple_of(c * rc, rc)
        for g in range(N_FGROUPS):
            a = jnp.dot(fs[pl.ds(r0, rc), g * LANES:(g + 1) * LANES], fcs_ref[...],
                        preferred_element_type=F32)
            stks[pl.ds(r0, rc), g * LANES:(g + 1) * LANES] = a[:, :LANES].astype(BF16)
            stks[pl.ds(pl.multiple_of(n + r0, rc), rc), g * LANES:(g + 1) * LANES] = (
                a[:, LANES:].astype(BF16))

    _loop(n_chunks, dft_channels)

    fscale = 1.0 / math.sqrt(n * LANES)

    def dft_positions(c):
        r0 = pl.multiple_of(c * rc, rc)
        y = jnp.dot(dftn_ref[pl.ds(r0, rc), :], stks[...], preferred_element_type=F32) * fscale
        mixs[pl.ds(r0, rc), qk_w:] = y.astype(BF16)

    _loop(n_chunks, dft_positions)

    klane = lax.broadcasted_iota(jnp.int32, (rc, LANES), 1)
    neg_inf = jnp.float32(-jnp.inf)

    def tail(c):
        r0 = pl.multiple_of(c * rc, rc)
        x = xs[pl.ds(r0, rc), :]
        mixed = jnp.dot(mixs[pl.ds(r0, rc), :], wout_ref[...], preferred_element_type=F32)
        x1 = x + gate1 * mixed
        x1_ref[pl.ds(row0 + r0, rc), :] = x1
        h2 = (x1 * _rsqrt_mean_sq(x1) * g2_ref[...]) * (1.0 + scale2) + shift2
        tpr = d // LANES
        for j in range(tpr):
            h2_ref[pl.ds((row0 + r0) * tpr + j, rc, stride=tpr), :] = (
                h2[:, j * LANES:(j + 1) * LANES])
        hi = h2.astype(BF16)
        lo = (h2 - hi.astype(F32)).astype(BF16)
        logits = (jnp.dot(hi, rwh_ref[...], preferred_element_type=F32)
                  + jnp.dot(lo, rwh_ref[...], preferred_element_type=F32)
                  + jnp.dot(hi, rwl_ref[...], preferred_element_type=F32))
        l = jnp.where(klane < n_experts, logits + rb_ref[...], neg_inf)
        vals, ids = [], []
        for _ in range(TOP_K):
            m = jnp.max(l, axis=-1, keepdims=True)
            cand = jnp.where(l == m, klane, LANES).astype(F32)
            i = jnp.min(cand, axis=-1, keepdims=True).astype(jnp.int32)
            vals.append(m)
            ids.append(i)
            l = jnp.where(klane == i, neg_inf, l)
        es = [jnp.exp(v - vals[0]) for v in vals]
        inv = 1.0 / functools.reduce(lambda a, b: a + b, es)
        idx_out = jnp.zeros((rc, LANES), jnp.int32)
        wts_out = jnp.zeros((rc, LANES), F32)
        for k in range(TOP_K):
            idx_out = jnp.where(klane == k, ids[k], idx_out)
            wts_out = jnp.where(klane == k, es[k] * inv, wts_out)
        idx_ref[pl.ds(row0 + r0, rc), :] = idx_out
        wts_ref[pl.ds(row0 + r0, rc), :] = wts_out
        hits = functools.reduce(lambda a, b: a + b,
                                [jnp.where(klane == i, 1.0, 0.0) for i in ids])
        cnt_ref[...] += jnp.sum(hits, axis=0, keepdims=True)

    _loop(n_chunks, tail)


def _const_spec(shape):
    nd = len(shape)
    return pl.BlockSpec(shape, lambda b: (0,) * nd, pipeline_mode=pl.Buffered(1))


def _layer(x, mod, g1, win, lamv, subg, fcs, dftn, wout, g2, rwh, rwl, rb, *, n_experts,
           mod_row0, mod_row_step, h2_rows, h2_block0, seqs=1, zero_blocks=0, cache=None,
           h2_buf=None):
    n_seq, n, d = x.shape
    bsz = n_seq // seqs
    heads = win.shape[1] // (4 * LANES)
    rope = cache is not None
    n_cache = cache[0].shape[2] if rope else 0
    nk = n_cache + n
    live = lambda b: jnp.minimum(b, bsz - 1)
    in_specs = [
        pl.BlockSpec((seqs, n, d), lambda b: (live(b), 0, 0)),
        pl.BlockSpec((1, 1, mod.shape[-1]),
                     lambda b: (mod_row0 + mod_row_step * live(b), 0, 0)),
        _const_spec(g1.shape), _const_spec(win.shape), _const_spec(lamv.shape),
        _const_spec(subg.shape), _const_spec(fcs.shape), _const_spec(dftn.shape),
        _const_spec(wout.shape), _const_spec(g2.shape), _const_spec(rwh.shape),
        _const_spec(rwl.shape), _const_spec(rb.shape),
    ]
    args = [x, mod, g1, win, lamv, subg, fcs, dftn, wout, g2, rwh, rwl, rb]
    out_specs = [
        pl.BlockSpec((seqs * n, d), lambda b: (live(b), 0)),
        pl.BlockSpec((seqs * n * (d // LANES), LANES), lambda b: (h2_block0 + b, 0)),
        pl.BlockSpec((seqs * n, LANES), lambda b: (live(b), 0)),
        pl.BlockSpec((seqs * n, LANES), lambda b: (live(b), 0)),
        pl.BlockSpec((1, LANES), lambda b: (0, 0)),
    ]
    out_shape = [
        jax.ShapeDtypeStruct((n_seq * n, d), F32),
        jax.ShapeDtypeStruct((h2_rows * (d // LANES), LANES), F32),
        jax.ShapeDtypeStruct((n_seq * n, LANES), jnp.int32),
        jax.ShapeDtypeStruct((n_seq * n, LANES), F32),
        jax.ShapeDtypeStruct((1, LANES), F32),
    ]
    aliases = {}
    if rope:
        ck, cv, cos, sin = cache
        in_specs += [
            pl.BlockSpec((1, heads, n_cache, LANES), lambda b: (live(b), 0, 0, 0)),
            pl.BlockSpec((1, heads, n_cache, LANES), lambda b: (live(b), 0, 0, 0)),
            _const_spec(cos.shape), _const_spec(sin.shape),
            pl.BlockSpec(memory_space=pl.ANY),
        ]
        args += [ck, cv, cos, sin, h2_buf]
        aliases = {len(args) - 1: 1}
    else:
        kv_spec = pl.BlockSpec((seqs, 1, heads, n, LANES), lambda b: (live(b), 0, 0, 0, 0))
        out_specs += [kv_spec, kv_spec]
        kv_shape = jax.ShapeDtypeStruct((n_seq, 1, heads, n, LANES), F32)
        out_shape += [kv_shape, kv_shape]
    scratch = [
        pltpu.VMEM((seqs, n, heads * LANES), BF16),
        pltpu.VMEM((seqs, n, heads * LANES), BF16),
        pltpu.VMEM((seqs, heads, nk, LANES), BF16),
        pltpu.VMEM((seqs, heads, nk, LANES), BF16),
        pltpu.VMEM((seqs, n, N_FGROUPS * LANES), BF16),
        pltpu.VMEM((seqs, 2 * n, N_FGROUPS * LANES), BF16),
        pltpu.VMEM((seqs, n, d), BF16),
    ]
    kern = functools.partial(_layer_kernel, n_live=bsz if zero_blocks else None,
                             n_inputs=len(args), n=n, seqs=seqs, n_cache=n_cache, heads=heads,
                             n_experts=n_experts, rope=rope, emit_kv=not rope)
    return pl.pallas_call(
        kern,
        grid=(bsz + zero_blocks,),
        in_specs=in_specs,
        out_specs=out_specs,
        out_shape=out_shape,
        scratch_shapes=scratch,
        input_output_aliases=aliases,
        compiler_params=pltpu.CompilerParams(dimension_semantics=("arbitrary",),
                                             vmem_limit_bytes=VMEM_LIMIT),
        name="layer_latent" if rope else "layer_context",
    )(*args)


def _moe_kernel(texp_ref, next_ref, nvalid_ref, tok_ref, tok_next_ref, tok_next2_ref, dst_ref,
                dst_prev_ref, h2_hbm, wgu_hbm, bgu_ref, wd_hbm, bd_ref, y_hbm, xbuf0, xbuf1, xbuf2,
                ybuf0, ybuf1, ybuf2, wgu_f32, wd_f32, wgu_bf, wd_bf, gsem, ssem, wsem, run_ref, *,
                tm):
    i = pl.program_id(0)
    nv = nvalid_ref[0]
    d_ff, d = wd_bf.shape
    tpr = d // LANES

    def weight_copies(e, s):
        return (pltpu.make_async_copy(wgu_hbm.at[e], wgu_f32.at[s], wsem.at[s, 0]),
                pltpu.make_async_copy(wd_hbm.at[e], wd_f32.at[s], wsem.at[s, 1]))

    def token_rows(t):
        return pl.ds(pl.multiple_of(t * tpr, tpr), tpr)

    xbuf, ybuf = (xbuf0, xbuf1, xbuf2), (ybuf0, ybuf1, ybuf2)

    def gather_row(idx_ref, slot, r):
        return pltpu.make_async_copy(h2_hbm.at[token_rows(idx_ref[0, 0, r]), :],
                                     xbuf[slot].at[pl.ds(r * tpr, tpr), :], gsem.at[slot])

    def scatter_row(idx_ref, slot, r):
        return pltpu.make_async_copy(ybuf[slot].at[pl.ds(r * tpr, tpr), :],
                                     y_hbm.at[token_rows(idx_ref[0, 0, r]), :], ssem.at[slot])

    def step(slot):
        ahead, behind = (slot + 1) % MOE_RING, (slot + 2) % MOE_RING
        for r in range(tm):
            gather_row(tok_ref, slot, r).wait()

        @pl.when(i >= 2)
        def _():
            for r in range(tm):
                scatter_row(dst_ref, slot, r).wait()

        for r in range(tm):
            gather_row(tok_next2_ref, behind, r).start()
        for r in range(tm):
            scatter_row(dst_prev_ref, behind, r).start()

        x = jnp.concatenate(
            [xbuf[slot][pl.ds(j, tm, stride=tpr), :].astype(BF16) for j in range(tpr)], axis=1)
        gu = jnp.dot(x, wgu_bf[...], preferred_element_type=F32) + bgu_ref[0]
        glu = jnp.minimum(gu[:, :d_ff], SWIGLU_LIMIT)
        lin = jnp.clip(gu[:, d_ff:], -SWIGLU_LIMIT, SWIGLU_LIMIT)
        act = glu * _sigmoid(SWIGLU_ALPHA * glu) * (lin + 1.0)
        y = jnp.dot(act.astype(BF16), wd_bf[...], preferred_element_type=F32) + bd_ref[0]
        for j in range(tpr):
            ybuf[slot][pl.ds(j, tm, stride=tpr), :] = y[:, j * LANES:(j + 1) * LANES]

        @pl.when(i == nv - 1)
        def _():
            for r in range(tm):
                scatter_row(dst_ref, slot, r).start()
            for r in range(tm):
                scatter_row(dst_ref, slot, r).wait()
            for r in range(tm):
                scatter_row(dst_prev_ref, behind, r).wait()

            @pl.when(i >= 1)
            def _():
                for r in range(tm):
                    scatter_row(dst_prev_ref, ahead, r).wait()

            for r in range(tm):
                gather_row(tok_next_ref, ahead, r).wait()
            for r in range(tm):
                gather_row(tok_next2_ref, behind, r).wait()

    @pl.when(i < nv)
    def _():
        @pl.when(i == 0)
        def _():
            run_ref[0] = 0
            for c in weight_copies(texp_ref[0], 0):
                c.start(priority=1)
            for r in range(tm):
                gather_row(tok_ref, 0, r).start()
            for r in range(tm):
                gather_row(tok_next_ref, 1, r).start()
            spare0 = y_hbm.shape[0] - 2 * tm * tpr
            for s in range(MOE_RING):
                ybuf[s][...] = jnp.zeros(ybuf[s].shape, F32)
            fills = [pltpu.make_async_copy(
                ybuf[s], y_hbm.at[pl.ds(spare0 + s * tm * tpr, tm * tpr), :], ssem.at[s])
                for s in range(2)]
            for f in fills:
                f.start()
            for f in fills:
                f.wait()

        @pl.when(jnp.logical_or(i == 0, texp_ref[i] != texp_ref[jnp.maximum(i - 1, 0)]))
        def _():
            run = run_ref[0]
            ws = run % 2
            for c in weight_copies(texp_ref[i], ws):
                c.wait()

            @pl.when(next_ref[i] >= 0)
            def _():
                for c in weight_copies(next_ref[i], 1 - ws):
                    c.start(priority=1)

            wgu_bf[...] = wgu_f32[ws].astype(BF16)
            wd_bf[...] = wd_f32[ws].astype(BF16)
            run_ref[0] = run + 1

        for s in range(MOE_RING):
            pl.when(i % MOE_RING == s)(functools.partial(step, s))


def _moe(h2, tile_expert, next_expert, n_valid, src_tok, dest, wgu, bgu, wd, bd):
    n_exp, d, two_f = wgu.shape
    tpr = d // LANES
    t_rows = h2.shape[0] // tpr
    d_ff = two_f // 2
    n_tiles = tile_expert.shape[0]
    tm = src_tok.shape[-1]
    last = n_tiles - 1
    smem_tile = functools.partial(pl.BlockSpec, (1, 1, tm), memory_space=pltpu.SMEM)
    grid_spec = pltpu.PrefetchScalarGridSpec(
        num_scalar_prefetch=3,
        grid=(n_tiles,),
        in_specs=[
            smem_tile(lambda i, te, nx, nv: (i, 0, 0)),
            smem_tile(lambda i, te, nx, nv: (jnp.minimum(i + 1, last), 0, 0)),
            smem_tile(lambda i, te, nx, nv: (jnp.minimum(i + 2, last), 0, 0)),
            smem_tile(lambda i, te, nx, nv: (i + 1, 0, 0)),
            smem_tile(lambda i, te, nx, nv: (i, 0, 0)),
            pl.BlockSpec(memory_space=pl.ANY),
            pl.BlockSpec(memory_space=pl.ANY),
            pl.BlockSpec((1, 1, two_f), lambda i, te, nx, nv: (te[i], 0, 0)),
            pl.BlockSpec(memory_space=pl.ANY),
            pl.BlockSpec((1, 1, d), lambda i, te, nx, nv: (te[i], 0, 0)),
        ],
        out_specs=pl.BlockSpec(memory_space=pl.ANY),
        scratch_shapes=[
            *([pltpu.VMEM((tm * tpr, LANES), F32)] * (2 * MOE_RING)),
            pltpu.VMEM((2, d, two_f), F32),
            pltpu.VMEM((2, d_ff, d), F32),
            pltpu.VMEM((d, two_f), BF16),
            pltpu.VMEM((d_ff, d), BF16),
            pltpu.SemaphoreType.DMA((MOE_RING,)),
            pltpu.SemaphoreType.DMA((MOE_RING,)),
            pltpu.SemaphoreType.DMA((2, 2)),
            pltpu.SMEM((1,), jnp.int32),
        ],
    )
    return pl.pallas_call(
        functools.partial(_moe_kernel, tm=tm),
        grid_spec=grid_spec,
        out_shape=jax.ShapeDtypeStruct(((TOP_K * t_rows + 2 * tm) * tpr, LANES), F32),
        compiler_params=pltpu.CompilerParams(dimension_semantics=("arbitrary",),
                                             vmem_limit_bytes=VMEM_LIMIT),
        name="routed_moe",
    )(tile_expert, next_expert, n_valid, src_tok, src_tok, src_tok, dest, dest, h2, wgu,
      bgu.reshape(n_exp, 1, two_f), wd, bd.reshape(n_exp, 1, d))


def _route(idx, counts, tm, n_tiles):
    t_rows, top_k = idx.shape
    n_exp = counts.shape[0]
    n_pairs = t_rows * top_k
    pad_bit = 16
    assert n_pairs <= 1 << pad_bit and tm <= 1 << pad_bit and n_tiles * tm == n_pairs + n_exp * tm
    pair_ids = np.arange(n_pairs, dtype=np.int32)
    real_keys = (idx.T.reshape(-1) << (pad_bit + 1)) | pair_ids
    pad_e = np.repeat(np.arange(n_exp, dtype=np.int32), tm)
    pad_j = np.tile(np.arange(tm, dtype=np.int32), n_exp)
    n_pad = (-counts) % tm
    unused = n_exp << (pad_bit + 1)
    pad_keys = jnp.where(pad_j < jnp.repeat(n_pad, tm),
                         (pad_e << (pad_bit + 1)) | (1 << pad_bit) | pad_j, unused)
    keys = jnp.sort(jnp.concatenate([real_keys, pad_keys]))
    is_real = jnp.logical_and((keys >> pad_bit) & 1 == 0, keys < unused)
    pair = keys & ((1 << pad_bit) - 1)
    slot = sum((pair >= k * t_rows).astype(jnp.int32) for k in range(1, top_k))
    pos = np.arange(n_tiles * tm, dtype=np.int32)
    spare = n_pairs + pos % (2 * tm)
    dest = jnp.concatenate([spare[:tm], jnp.where(is_real, pair, spare)])
    src_tok = jnp.where(is_real, pair - slot * t_rows, 0)
    n_valid = jnp.sum(counts + n_pad) // tm
    last_e = jnp.max(jnp.where(counts > 0, jnp.arange(n_exp, dtype=jnp.int32), 0))
    tile_expert = jnp.where(np.arange(n_tiles) < n_valid, keys[::tm] >> (pad_bit + 1), last_e)
    experts = jnp.arange(n_exp, dtype=jnp.int32)
    owner = jnp.where(counts > 0, experts, n_exp)
    following = jnp.concatenate([lax.cummin(owner, reverse=True)[1:],
                                 jnp.full((1,), n_exp, jnp.int32)])
    following = jnp.where(following >= n_exp, -1, following)
    next_expert = jnp.sum(jnp.where(tile_expert[:, None] == experts[None, :], following[None, :], 0),
                          axis=1)
    return (tile_expert.astype(jnp.int32), next_expert.astype(jnp.int32),
            n_valid.reshape(1).astype(jnp.int32), src_tok.reshape(n_tiles, 1, tm),
            dest.reshape(n_tiles + 1, 1, tm))


def _combine_kernel(x1_ref, y0_ref, y1_ref, y2_ref, y3_ref, wts_ref, gate_ref, fg_ref, o_ref):
    w = wts_ref[...]
    tc, d = x1_ref.shape
    tpr = d // LANES
    cols = []
    for j in range(tpr):
        acc = w[:, 0:1] * y0_ref[pl.ds(j, tc, stride=tpr), :]
        for k, y_ref in enumerate((y1_ref, y2_ref, y3_ref), start=1):
            acc = acc + w[:, k:k + 1] * y_ref[pl.ds(j, tc, stride=tpr), :]
        cols.append(acc)
    x = x1_ref[...] + gate_ref[0] * jnp.concatenate(cols, axis=1)
    o_ref[...] = x * _rsqrt_mean_sq(x) * fg_ref[...]


def _combine(x1, y, wts, gate2, final_g, *, t_all, row0, rows_per_gate):
    rows, d = x1.shape
    tc = COMBINE_TILE
    y_specs = [
        pl.BlockSpec((tc * (d // LANES), LANES), functools.partial(
            lambda i, k: ((k * t_all + row0) // tc + i, 0), k=k))
        for k in range(TOP_K)
    ]
    return pl.pallas_call(
        _combine_kernel,
        grid=(rows // tc,),
        in_specs=[pl.BlockSpec((tc, d), lambda i: (i, 0))] + y_specs + [
            pl.BlockSpec((tc, LANES), lambda i: (i, 0)),
            pl.BlockSpec((1, 1, d), lambda i: ((i * tc) // rows_per_gate, 0, 0)),
            pl.BlockSpec((1, d), lambda i: (0, 0)),
        ],
        out_specs=pl.BlockSpec((tc, d), lambda i: (i, 0)),
        out_shape=jax.ShapeDtypeStruct((rows, d), F32),
        compiler_params=pltpu.CompilerParams(dimension_semantics=("arbitrary",)),
        name="combine",
    )(x1, y, y, y, y, wts, gate2, final_g)


def _dft_tables(n):
    def angles(m):
        k = np.arange(m, dtype=np.int64)
        return (2.0 * np.pi / m) * ((k[:, None] * k[None, :]) % m)
    an = angles(n)
    ac = angles(LANES)
    dftn = np.concatenate([np.cos(an), -np.sin(an)], axis=1).astype(np.float32)
    fcs = np.concatenate([np.cos(ac), np.sin(ac)], axis=1).astype(np.float32)
    return jnp.asarray(dftn).astype(BF16), jnp.asarray(fcs).astype(BF16)


def _rope_tables(n, qk_dim):
    quarter = qk_dim // 4
    tok = np.arange(n)
    pos = np.stack([tok // GRID_W, tok % GRID_W], axis=-1).astype(np.float64)
    freqs = ROPE_THETA ** (-np.arange(quarter, dtype=np.float64) / quarter)
    ang = (pos[:, :, None] * freqs).reshape(n, 2 * quarter)
    cos, sin = np.cos(ang), np.sin(ang)
    row_c, col_c = cos[:, :quarter], cos[:, quarter:]
    row_s, col_s = sin[:, :quarter], sin[:, quarter:]
    cos_map = np.concatenate([row_c, row_c, col_c, col_c], axis=-1)
    sin_map = np.concatenate([-row_s, row_s, -col_s, col_s], axis=-1)
    reps = LANES // qk_dim
    return (np.tile(cos_map, (1, reps)).astype(np.float32),
            np.tile(sin_map, (1, reps)).astype(np.float32))


def kernel(x_prompt, x_sample, cache_k, cache_v, c, c_ctx, w_mod, b_mod, norm1_g, w_in, lambda_q1,
           lambda_k1, lambda_q2, lambda_k2, subln_g, w_out, norm2_g, router_w, router_b, w_gate_up,
           b_gate_up, w_down, b_down, final_g):
    bsz, seq, d = x_prompt.shape
    dec_b, dec_seq, _ = x_sample.shape
    heads, past, qk_dim = cache_k.shape[2], cache_k.shape[3], cache_k.shape[5]
    n_exp = router_w.shape[-1]
    t_ctx, t_den = bsz * seq, dec_b * dec_seq
    t_all = t_ctx + t_den
    assert bsz % CTX_SEQS == 0 and t_den % (CTX_SEQS * seq) == 0 and t_ctx % dec_seq == 0
    assert 2 * qk_dim == LANES and dec_seq % GRID_W == 0
    assert (t_all * TOP_K) % MOE_TILE == 0 and t_all % COMBINE_TILE == 0

    cvec = jnp.concatenate([c_ctx[None, :], c, jnp.zeros((8 - 1 - dec_b, d), F32)], axis=0)
    mod = _modulation(cvec, w_mod[0], b_mod[0])[:, None, :]

    win = w_in[0].astype(BF16)
    wout = w_out[0].astype(BF16)
    rw = jnp.pad(router_w[0], ((0, 0), (0, LANES - n_exp)))
    rwh = rw.astype(BF16)
    rwl = (rw - rwh.astype(F32)).astype(BF16)
    rb = jnp.pad(router_b[0], (0, LANES - n_exp)).reshape(1, LANES)
    lamv = jnp.stack([lambda_q1[0], lambda_k1[0], lambda_q2[0], lambda_k2[0]], axis=0)
    g1 = norm1_g[0].reshape(1, d)
    g2 = norm2_g[0].reshape(1, d)
    subg = subln_g[0].reshape(1, LANES)
    dft_ctx, fcs = _dft_tables(seq)
    dft_den, _ = _dft_tables(dec_seq)
    cos, sin = _rope_tables(dec_seq, qk_dim)

    shared = (g1, win, lamv, subg, fcs)
    tail = (wout, g2, rwh, rwl, rb)
    x1_ctx, h2_all, idx_ctx, wts_ctx, cnt_ctx, new_k, new_v = _layer(
        x_prompt, mod, *shared, dft_ctx, *tail, n_experts=n_exp, mod_row0=0, mod_row_step=0,
        h2_rows=t_all, h2_block0=0, seqs=CTX_SEQS, zero_blocks=t_den // (CTX_SEQS * seq))
    ck = cache_k[:, 0].reshape(dec_b, heads, past, LANES)
    cv = cache_v[:, 0]
    x1_den, h2_all, idx_den, wts_den, cnt_den = _layer(
        x_sample, mod, *shared, dft_den, *tail, n_experts=n_exp, mod_row0=1, mod_row_step=1,
        h2_rows=t_all, h2_block0=t_ctx // dec_seq, cache=(ck, cv, cos, sin), h2_buf=h2_all)

    idx = jnp.concatenate([idx_ctx[:, :TOP_K], idx_den[:, :TOP_K]], axis=0)
    n_tiles = (t_all * TOP_K) // MOE_TILE + n_exp
    counts = (cnt_ctx + cnt_den)[0, :n_exp].astype(jnp.int32)
    tile_expert, next_expert, n_valid, src_tok, dest = _route(idx, counts, MOE_TILE, n_tiles)
    y = _moe(h2_all, tile_expert, next_expert, n_valid, src_tok, dest, w_gate_up[0], b_gate_up[0],
             w_down[0], b_down[0])

    gate2 = mod[:, :, 5 * d:]
    fg = final_g.reshape(1, d)
    y_prompt = _combine(x1_ctx, y, wts_ctx, gate2[0:1], fg, t_all=t_all, row0=0,
                        rows_per_gate=t_ctx)
    y_sample = _combine(x1_den, y, wts_den, gate2[1:1 + dec_b], fg, t_all=t_all, row0=t_ctx,
                        rows_per_gate=dec_seq)
    return (y_prompt.reshape(bsz, seq, d), y_sample.reshape(dec_b, dec_seq, d),
            new_k.reshape(bsz, 1, heads, seq, 2, qk_dim), new_v)
```

```python
import functools
import math

import numpy as np
import jax
import jax.numpy as jnp
from jax import lax
from jax.experimental import pallas as pl
from jax.experimental.pallas import tpu as pltpu

F32 = jnp.float32
BF16 = jnp.bfloat16

GRID_W = 64
N_FGROUPS = 4
TOP_K = 4
SWIGLU_LIMIT = 7.0
SWIGLU_ALPHA = 1.702
ROPE_THETA = 10000.0
NORM_EPS = 1e-6
LAMBDA_INIT = 0.8 - 0.6 * math.exp(-0.3 * 0)

LANES = 128
ROW_CHUNK = 256
CTX_SEQS = 2
MOE_TILE = 256
MOE_RING = 3
FF_CHUNK = 256
COMBINE_TILE = 256
VMEM_LIMIT = 56 * 1024 * 1024


def _rsqrt_mean_sq(x):
    return lax.rsqrt(jnp.mean(x * x, axis=-1, keepdims=True) + NORM_EPS)


def _sigmoid(z):
    return 1.0 / (1.0 + jnp.exp(-z))


def _mod_kernel(c_ref, w_ref, b_ref, o_ref):
    c = c_ref[...]
    s = c * _sigmoid(c)
    o_ref[...] = jnp.dot(s.astype(BF16), w_ref[...].astype(BF16),
                         preferred_element_type=F32) + b_ref[...]


def _modulation(cvec, w_mod, b_mod):
    rows, d = cvec.shape
    n_out = w_mod.shape[1]
    return pl.pallas_call(
        _mod_kernel,
        grid=(n_out // d,),
        in_specs=[
            pl.BlockSpec((rows, d), lambda j: (0, 0)),
            pl.BlockSpec((d, d), lambda j: (0, j)),
            pl.BlockSpec((1, d), lambda j: (0, j)),
        ],
        out_specs=pl.BlockSpec((rows, d), lambda j: (0, j)),
        out_shape=jax.ShapeDtypeStruct((rows, n_out), F32),
        name="modulation",
    )(cvec, w_mod, b_mod.reshape(1, n_out))


def _loop(n, body):
    if n == 1:
        body(0)
    else:
        def step(i, carry):
            body(i)
            return carry
        lax.fori_loop(0, n, step, 0)


def _layer_kernel(*refs, n_live, n_inputs, **static):
    if n_live is None:
        _layer_body(*refs, **static)
        return
    b = pl.program_id(0)
    h2_ref = refs[n_inputs + 1]
    pl.when(b < n_live)(functools.partial(_layer_body, *refs, **static))

    @pl.when(b >= n_live)
    def _():
        h2_ref[...] = jnp.zeros(h2_ref.shape, F32)


def _layer_body(*refs, n, seqs, n_cache, heads, n_experts, rope, emit_kv):
    it = iter(refs)
    x_ref = next(it); mod_ref = next(it); g1_ref = next(it); win_ref = next(it)
    lamv_ref = next(it); subg_ref = next(it); fcs_ref = next(it); dftn_ref = next(it)
    wout_ref = next(it); g2_ref = next(it); rwh_ref = next(it); rwl_ref = next(it); rb_ref = next(it)
    if rope:
        ck_ref = next(it); cv_ref = next(it); cos_ref = next(it); sin_ref = next(it)
        next(it)
    x1_ref = next(it); h2_ref = next(it); idx_ref = next(it); wts_ref = next(it); cnt_ref = next(it)
    if emit_kv:
        newk_ref = next(it); newv_ref = next(it)
    q1_scr = next(it); q2_scr = next(it); kall = next(it); vall = next(it)
    f_scr = next(it); stk = next(it); mix = next(it)

    d = x_ref.shape[-1]
    qk_w = heads * LANES

    @pl.when(pl.program_id(0) == 0)
    def _():
        cnt_ref[...] = jnp.zeros(cnt_ref.shape, F32)
    rc = min(ROW_CHUNK, n)
    n_chunks = n // rc

    def mod_row(j):
        return mod_ref[0, :, j * d:(j + 1) * d]

    shift1, scale1, gate1 = mod_row(0), mod_row(1), mod_row(2)
    shift2, scale2, gate2 = mod_row(3), mod_row(4), mod_row(5)
    del gate2

    lv = lamv_ref[...]
    lam = (jnp.exp(jnp.sum(lv[0:1] * lv[1:2], axis=-1, keepdims=True))
           - jnp.exp(jnp.sum(lv[2:3] * lv[3:4], axis=-1, keepdims=True)) + LAMBDA_INIT)

    if rope:
        for hd in range(heads):
            kall[0, hd, 0:n_cache, :] = ck_ref[0, hd].astype(BF16)
            vall[0, hd, 0:n_cache, :] = cv_ref[0, hd].astype(BF16)

    lane = lax.broadcasted_iota(jnp.int32, (rc, LANES), 1)
    first_map = lane < (LANES // 2)

    first_of_pair = jnp.bitwise_and(lane, 31) < 16

    def rotate(t, cos, sin):
        partner = jnp.where(first_of_pair, pltpu.roll(t, LANES - 16, 1), pltpu.roll(t, 16, 1))
        return t * cos + partner * sin

    for seq in range(seqs):
        _layer_sequence(seq, locals())


def _layer_sequence(seq, env):
    (x_ref, g1_ref, win_ref, subg_ref, fcs_ref, dftn_ref, wout_ref, g2_ref, rwh_ref, rwl_ref, rb_ref,
     x1_ref, h2_ref, idx_ref, wts_ref, cnt_ref, n, n_cache, heads, n_experts, rope, emit_kv, d,
     qk_w, rc, n_chunks, shift1, scale1, gate1, shift2, scale2, lam, first_map, rotate) = (
        env[k] for k in (
            "x_ref g1_ref win_ref subg_ref fcs_ref dftn_ref wout_ref g2_ref rwh_ref rwl_ref rb_ref "
            "x1_ref h2_ref idx_ref wts_ref cnt_ref n n_cache heads n_experts rope emit_kv d "
            "qk_w rc n_chunks shift1 scale1 gate1 shift2 scale2 lam first_map rotate").split())
    cos_ref, sin_ref = env.get("cos_ref"), env.get("sin_ref")
    newk_ref, newv_ref = env.get("newk_ref"), env.get("newv_ref")
    xs = x_ref.at[seq]
    q1s, q2s, ks, vs = (env[k].at[seq] for k in ("q1_scr", "q2_scr", "kall", "vall"))
    fs, stks, mixs = (env[k].at[seq] for k in ("f_scr", "stk", "mix"))
    row0 = seq * n

    def project(c):
        r0 = pl.multiple_of(c * rc, rc)
        x = xs[pl.ds(r0, rc), :]
        h = (x * _rsqrt_mean_sq(x) * g1_ref[...]) * (1.0 + scale1) + shift1
        p = jnp.dot(h.astype(BF16), win_ref[...], preferred_element_type=F32)
        if rope:
            cos = cos_ref[pl.ds(r0, rc), :]
            sin = sin_ref[pl.ds(r0, rc), :]
        for hd in range(heads):
            qh = p[:, hd * LANES:(hd + 1) * LANES]
            kh = p[:, qk_w + hd * LANES:qk_w + (hd + 1) * LANES]
            vh = p[:, 2 * qk_w + hd * LANES:2 * qk_w + (hd + 1) * LANES]
            if rope:
                qh = rotate(qh, cos, sin)
                kh = rotate(kh, cos, sin)
            if emit_kv:
                newk_ref[seq, 0, hd, pl.ds(r0, rc), :] = kh
                newv_ref[seq, 0, hd, pl.ds(r0, rc), :] = vh
            qs = qh * (LANES // 2) ** -0.5
            head = slice(hd * LANES, (hd + 1) * LANES)
            q1s[pl.ds(r0, rc), head] = jnp.where(first_map, qs, 0.0).astype(BF16)
            q2s[pl.ds(r0, rc), head] = jnp.where(first_map, 0.0, qs).astype(BF16)
            ks[hd, pl.ds(n_cache + r0, rc), :] = kh.astype(BF16)
            vs[hd, pl.ds(n_cache + r0, rc), :] = vh.astype(BF16)
        fs[pl.ds(r0, rc), :] = p[:, 3 * qk_w:].astype(BF16)

    _loop(n_chunks, project)

    contract_last = (((1,), (1,)), ((), ()))

    def softmax(s):
        e = jnp.exp(s - jnp.max(s, axis=-1, keepdims=True))
        return e * (1.0 / jnp.sum(e, axis=-1, keepdims=True))

    def attend(c):
        r0 = pl.multiple_of(c * rc, rc)
        for hd in range(heads):
            kh = ks[hd]
            s1 = lax.dot_general(q1s[pl.ds(r0, rc), hd * LANES:(hd + 1) * LANES], kh,
                                 contract_last, preferred_element_type=F32)
            s2 = lax.dot_general(q2s[pl.ds(r0, rc), hd * LANES:(hd + 1) * LANES], kh,
                                 contract_last, preferred_element_type=F32)
            a = softmax(s1) - lam * softmax(s2)
            o = jnp.dot(a.astype(BF16), vs[hd], preferred_element_type=F32)
            o = o * _rsqrt_mean_sq(o) * subg_ref[...] * (1.0 - LAMBDA_INIT)
            mixs[pl.ds(r0, rc), hd * LANES:(hd + 1) * LANES] = o.astype(BF16)

    _loop(n_chunks, attend)

    def dft_channels(c):
        r0 = pl.multiple_of(c * rc, rc)
        for g in range(N_FGROUPS):
            a = jnp.dot(fs[pl.ds(r0, rc), g * LANES:(g + 1) * LANES], fcs_ref[...],
                        preferred_element_type=F32)
            stks[pl.ds(r0, rc), g * LANES:(g + 1) * LANES] = a[:, :LANES].astype(BF16)
            stks[pl.ds(pl.multiple_of(n + r0, rc), rc), g * LANES:(g + 1) * LANES] = (
                a[:, LANES:].astype(BF16))

    _loop(n_chunks, dft_channels)

    fscale = 1.0 / math.sqrt(n * LANES)

    def dft_positions(c):
        r0 = pl.multiple_of(c * rc, rc)
        y = jnp.dot(dftn_ref[pl.ds(r0, rc), :], stks[...], preferred_element_type=F32) * fscale
        mixs[pl.ds(r0, rc), qk_w:] = y.astype(BF16)

    _loop(n_chunks, dft_positions)

    klane = lax.broadcasted_iota(jnp.int32, (rc, LANES), 1)
    neg_inf = jnp.float32(-jnp.inf)

    def tail(c):
        r0 = pl.multiple_of(c * rc, rc)
        x = xs[pl.ds(r0, rc), :]
        mixed = jnp.dot(mixs[pl.ds(r0, rc), :], wout_ref[...], preferred_element_type=F32)
        x1 = x + gate1 * mixed
        x1_ref[pl.ds(row0 + r0, rc), :] = x1
        h2 = (x1 * _rsqrt_mean_sq(x1) * g2_ref[...]) * (1.0 + scale2) + shift2
        tpr = d // LANES
        for j in range(tpr):
            h2_ref[pl.ds((row0 + r0) * tpr + j, rc, stride=tpr), :] = (
                h2[:, j * LANES:(j + 1) * LANES])
        hi = h2.astype(BF16)
        lo = (h2 - hi.astype(F32)).astype(BF16)
        logits = (jnp.dot(hi, rwh_ref[...], preferred_element_type=F32)
                  + jnp.dot(lo, rwh_ref[...], preferred_element_type=F32)
                  + jnp.dot(hi, rwl_ref[...], preferred_element_type=F32))
        l = jnp.where(klane < n_experts, logits + rb_ref[...], neg_inf)
        vals, ids = [], []
        for _ in range(TOP_K):
            m = jnp.max(l, axis=-1, keepdims=True)
            cand = jnp.where(l == m, klane, LANES).astype(F32)
            i = jnp.min(cand, axis=-1, keepdims=True).astype(jnp.int32)
            vals.append(m)
            ids.append(i)
            l = jnp.where(klane == i, neg_inf, l)
        es = [jnp.exp(v - vals[0]) for v in vals]
        inv = 1.0 / functools.reduce(lambda a, b: a + b, es)
        idx_out = jnp.zeros((rc, LANES), jnp.int32)
        wts_out = jnp.zeros((rc, LANES), F32)
        for k in range(TOP_K):
            idx_out = jnp.where(klane == k, ids[k], idx_out)
            wts_out = jnp.where(klane == k, es[k] * inv, wts_out)
        idx_ref[pl.ds(row0 + r0, rc), :] = idx_out
        wts_ref[pl.ds(row0 + r0, rc), :] = wts_out
        hits = functools.reduce(lambda a, b: a + b,
                                [jnp.where(klane == i, 1.0, 0.0) for i in ids])
        cnt_ref[...] += jnp.sum(hits, axis=0, keepdims=True)

    _loop(n_chunks, tail)


def _const_spec(shape):
    nd = len(shape)
    return pl.BlockSpec(shape, lambda b: (0,) * nd, pipeline_mode=pl.Buffered(1))


def _layer(x, mod, g1, win, lamv, subg, fcs, dftn, wout, g2, rwh, rwl, rb, *, n_experts,
           mod_row0, mod_row_step, h2_rows, h2_block0, seqs=1, zero_blocks=0, cache=None,
           h2_buf=None):
    n_seq, n, d = x.shape
    bsz = n_seq // seqs
    heads = win.shape[1] // (4 * LANES)
    rope = cache is not None
    n_cache = cache[0].shape[2] if rope else 0
    nk = n_cache + n
    live = lambda b: jnp.minimum(b, bsz - 1)
    in_specs = [
        pl.BlockSpec((seqs, n, d), lambda b: (live(b), 0, 0)),
        pl.BlockSpec((1, 1, mod.shape[-1]),
                     lambda b: (mod_row0 + mod_row_step * live(b), 0, 0)),
        _const_spec(g1.shape), _const_spec(win.shape), _const_spec(lamv.shape),
        _const_spec(subg.shape), _const_spec(fcs.shape), _const_spec(dftn.shape),
        _const_spec(wout.shape), _const_spec(g2.shape), _const_spec(rwh.shape),
        _const_spec(rwl.shape), _const_spec(rb.shape),
    ]
    args = [x, mod, g1, win, lamv, subg, fcs, dftn, wout, g2, rwh, rwl, rb]
    out_specs = [
        pl.BlockSpec((seqs * n, d), lambda b: (live(b), 0)),
        pl.BlockSpec((seqs * n * (d // LANES), LANES), lambda b: (h2_block0 + b, 0)),
        pl.BlockSpec((seqs * n, LANES), lambda b: (live(b), 0)),
        pl.BlockSpec((seqs * n, LANES), lambda b: (live(b), 0)),
        pl.BlockSpec((1, LANES), lambda b: (0, 0)),
    ]
    out_shape = [
        jax.ShapeDtypeStruct((n_seq * n, d), F32),
        jax.ShapeDtypeStruct((h2_rows * (d // LANES), LANES), F32),
        jax.ShapeDtypeStruct((n_seq * n, LANES), jnp.int32),
        jax.ShapeDtypeStruct((n_seq * n, LANES), F32),
        jax.ShapeDtypeStruct((1, LANES), F32),
    ]
    aliases = {}
    if rope:
        ck, cv, cos, sin = cache
        in_specs += [
            pl.BlockSpec((1, heads, n_cache, LANES), lambda b: (live(b), 0, 0, 0)),
            pl.BlockSpec((1, heads, n_cache, LANES), lambda b: (live(b), 0, 0, 0)),
            _const_spec(cos.shape), _const_spec(sin.shape),
            pl.BlockSpec(memory_space=pl.ANY),
        ]
        args += [ck, cv, cos, sin, h2_buf]
        aliases = {len(args) - 1: 1}
    else:
        kv_spec = pl.BlockSpec((seqs, 1, heads, n, LANES), lambda b: (live(b), 0, 0, 0, 0))
        out_specs += [kv_spec, kv_spec]
        kv_shape = jax.ShapeDtypeStruct((n_seq, 1, heads, n, LANES), F32)
        out_shape += [kv_shape, kv_shape]
    scratch = [
        pltpu.VMEM((seqs, n, heads * LANES), BF16),
        pltpu.VMEM((seqs, n, heads * LANES), BF16),
        pltpu.VMEM((seqs, heads, nk, LANES), BF16),
        pltpu.VMEM((seqs, heads, nk, LANES), BF16),
        pltpu.VMEM((seqs, n, N_FGROUPS * LANES), BF16),
        pltpu.VMEM((seqs, 2 * n, N_FGROUPS * LANES), BF16),
        pltpu.VMEM((seqs, n, d), BF16),
    ]
    kern = functools.partial(_layer_kernel, n_live=bsz if zero_blocks else None,
                             n_inputs=len(args), n=n, seqs=seqs, n_cache=n_cache, heads=heads,
                             n_experts=n_experts, rope=rope, emit_kv=not rope)
    return pl.pallas_call(
        kern,
        grid=(bsz + zero_blocks,),
        in_specs=in_specs,
        out_specs=out_specs,
        out_shape=out_shape,
        scratch_shapes=scratch,
        input_output_aliases=aliases,
        compiler_params=pltpu.CompilerParams(dimension_semantics=("arbitrary",),
                                             vmem_limit_bytes=VMEM_LIMIT),
        name="layer_latent" if rope else "layer_context",
    )(*args)


def _moe_kernel(texp_ref, next_ref, nvalid_ref, tok_ref, tok_next_ref, tok_next2_ref, dst_ref,
                dst_prev_ref, h2_hbm, wgu_hbm, bgu_ref, wd_hbm, bd_ref, y_hbm, xbuf0, xbuf1, xbuf2,
                ybuf0, ybuf1, ybuf2, wgu_f32, wd_f32, wgu_bf, wd_bf, gsem, ssem, wsem, run_ref, *,
                tm):
    i = pl.program_id(0)
    nv = nvalid_ref[0]
    d_ff, d = wd_bf.shape
    tpr = d // LANES

    def weight_copies(e, s):
        return (pltpu.make_async_copy(wgu_hbm.at[e], wgu_f32.at[s], wsem.at[s, 0]),
                pltpu.make_async_copy(wd_hbm.at[e], wd_f32.at[s], wsem.at[s, 1]))

    def token_rows(t):
        return pl.ds(pl.multiple_of(t * tpr, tpr), tpr)

    xbuf, ybuf = (xbuf0, xbuf1, xbuf2), (ybuf0, ybuf1, ybuf2)

    def gather_row(idx_ref, slot, r, after=0):
        return pltpu.make_async_copy(h2_hbm.at[token_rows(idx_ref[0, 0, r] + after), :],
                                     xbuf[slot].at[pl.ds(r * tpr, tpr), :], gsem.at[slot])

    def scatter_row(idx_ref, slot, r, after=0):
        return pltpu.make_async_copy(ybuf[slot].at[pl.ds(r * tpr, tpr), :],
                                     y_hbm.at[token_rows(idx_ref[0, 0, r] + after), :],
                                     ssem.at[slot])

    def step(slot):
        ahead, behind = (slot + 1) % MOE_RING, (slot + 2) % MOE_RING
        for r in range(tm):
            gather_row(tok_ref, slot, r).wait()

        @pl.when(i >= 2)
        def _():
            for r in range(tm):
                scatter_row(dst_ref, slot, r).wait()

        x = jnp.concatenate(
            [xbuf[slot][pl.ds(j, tm, stride=tpr), :].astype(BF16) for j in range(tpr)], axis=1)
        n_groups = d_ff // FF_CHUNK
        per_group = tm // n_groups
        acts, after = [], 0
        for g in range(n_groups):
            for r in range(g * per_group, (g + 1) * per_group):
                gather_row(tok_next2_ref, behind, r, after).start()
            for r in range(g * per_group, (g + 1) * per_group):
                scatter_row(dst_prev_ref, behind, r, after).start()
            c0 = g * FF_CHUNK
            gate = (jnp.dot(x, wgu_bf[:, c0:c0 + FF_CHUNK], preferred_element_type=F32)
                    + bgu_ref[0, :, c0:c0 + FF_CHUNK])
            lin = (jnp.dot(x, wgu_bf[:, d_ff + c0:d_ff + c0 + FF_CHUNK],
                           preferred_element_type=F32)
                   + bgu_ref[0, :, d_ff + c0:d_ff + c0 + FF_CHUNK])
            glu = jnp.minimum(gate, SWIGLU_LIMIT)
            lin = jnp.clip(lin, -SWIGLU_LIMIT, SWIGLU_LIMIT)
            act = glu * _sigmoid(SWIGLU_ALPHA * glu) * (lin + 1.0)
            acts.append(act.astype(BF16))
            after = jnp.clip(act[0:1, 0:1].astype(jnp.int32), 0, 0)[0, 0]
        y = jnp.dot(jnp.concatenate(acts, axis=1), wd_bf[...],
                    preferred_element_type=F32) + bd_ref[0]
        for j in range(tpr):
            ybuf[slot][pl.ds(j, tm, stride=tpr), :] = y[:, j * LANES:(j + 1) * LANES]

        @pl.when(i == nv - 1)
        def _():
            for r in range(tm):
                scatter_row(dst_ref, slot, r).start()
            for r in range(tm):
                scatter_row(dst_ref, slot, r).wait()
            for r in range(tm):
                scatter_row(dst_prev_ref, behind, r).wait()

            @pl.when(i >= 1)
            def _():
                for r in range(tm):
                    scatter_row(dst_prev_ref, ahead, r).wait()

            for r in range(tm):
                gather_row(tok_next_ref, ahead, r).wait()
            for r in range(tm):
                gather_row(tok_next2_ref, behind, r).wait()

    @pl.when(i < nv)
    def _():
        @pl.when(i == 0)
        def _():
            run_ref[0] = 0
            for c in weight_copies(texp_ref[0], 0):
                c.start(priority=1)
            for r in range(tm):
                gather_row(tok_ref, 0, r).start()
            for r in range(tm):
                gather_row(tok_next_ref, 1, r).start()
            spare0 = y_hbm.shape[0] - 2 * tm * tpr
            for s in range(MOE_RING):
                ybuf[s][...] = jnp.zeros(ybuf[s].shape, F32)
            fills = [pltpu.make_async_copy(
                ybuf[s], y_hbm.at[pl.ds(spare0 + s * tm * tpr, tm * tpr), :], ssem.at[s])
                for s in range(2)]
            for f in fills:
                f.start()
            for f in fills:
                f.wait()

        @pl.when(jnp.logical_or(i == 0, texp_ref[i] != texp_ref[jnp.maximum(i - 1, 0)]))
        def _():
            run = run_ref[0]
            ws = run % 2
            for c in weight_copies(texp_ref[i], ws):
                c.wait()

            @pl.when(next_ref[i] >= 0)
            def _():
                for c in weight_copies(next_ref[i], 1 - ws):
                    c.start(priority=1)

            wgu_bf[...] = wgu_f32[ws].astype(BF16)
            wd_bf[...] = wd_f32[ws].astype(BF16)
            run_ref[0] = run + 1

        for s in range(MOE_RING):
            pl.when(i % MOE_RING == s)(functools.partial(step, s))


def _moe(h2, tile_expert, next_expert, n_valid, src_tok, dest, wgu, bgu, wd, bd):
    n_exp, d, two_f = wgu.shape
    tpr = d // LANES
    t_rows = h2.shape[0] // tpr
    d_ff = two_f // 2
    n_tiles = tile_expert.shape[0]
    tm = src_tok.shape[-1]
    last = n_tiles - 1
    smem_tile = functools.partial(pl.BlockSpec, (1, 1, tm), memory_space=pltpu.SMEM)
    grid_spec = pltpu.PrefetchScalarGridSpec(
        num_scalar_prefetch=3,
        grid=(n_tiles,),
        in_specs=[
            smem_tile(lambda i, te, nx, nv: (i, 0, 0)),
            smem_tile(lambda i, te, nx, nv: (jnp.minimum(i + 1, last), 0, 0)),
            smem_tile(lambda i, te, nx, nv: (jnp.minimum(i + 2, last), 0, 0)),
            smem_tile(lambda i, te, nx, nv: (i + 1, 0, 0)),
            smem_tile(lambda i, te, nx, nv: (i, 0, 0)),
            pl.BlockSpec(memory_space=pl.ANY),
            pl.BlockSpec(memory_space=pl.ANY),
            pl.BlockSpec((1, 1, two_f), lambda i, te, nx, nv: (te[i], 0, 0)),
            pl.BlockSpec(memory_space=pl.ANY),
            pl.BlockSpec((1, 1, d), lambda i, te, nx, nv: (te[i], 0, 0)),
        ],
        out_specs=pl.BlockSpec(memory_space=pl.ANY),
        scratch_shapes=[
            *([pltpu.VMEM((tm * tpr, LANES), F32)] * (2 * MOE_RING)),
            pltpu.VMEM((2, d, two_f), F32),
            pltpu.VMEM((2, d_ff, d), F32),
            pltpu.VMEM((d, two_f), BF16),
            pltpu.VMEM((d_ff, d), BF16),
            pltpu.SemaphoreType.DMA((MOE_RING,)),
            pltpu.SemaphoreType.DMA((MOE_RING,)),
            pltpu.SemaphoreType.DMA((2, 2)),
            pltpu.SMEM((1,), jnp.int32),
        ],
    )
    return pl.pallas_call(
        functools.partial(_moe_kernel, tm=tm),
        grid_spec=grid_spec,
        out_shape=jax.ShapeDtypeStruct(((TOP_K * t_rows + 2 * tm) * tpr, LANES), F32),
        compiler_params=pltpu.CompilerParams(dimension_semantics=("arbitrary",),
                                             vmem_limit_bytes=VMEM_LIMIT),
        name="routed_moe",
    )(tile_expert, next_expert, n_valid, src_tok, src_tok, src_tok, dest, dest, h2, wgu,
      bgu.reshape(n_exp, 1, two_f), wd, bd.reshape(n_exp, 1, d))


def _route(idx, counts, tm, n_tiles):
    t_rows, top_k = idx.shape
    n_exp = counts.shape[0]
    n_pairs = t_rows * top_k
    pad_bit = 16
    assert n_pairs <= 1 << pad_bit and tm <= 1 << pad_bit and n_tiles * tm == n_pairs + n_exp * tm
    pair_ids = np.arange(n_pairs, dtype=np.int32)
    real_keys = (idx.T.reshape(-1) << (pad_bit + 1)) | pair_ids
    pad_e = np.repeat(np.arange(n_exp, dtype=np.int32), tm)
    pad_j = np.tile(np.arange(tm, dtype=np.int32), n_exp)
    n_pad = (-counts) % tm
    unused = n_exp << (pad_bit + 1)
    pad_keys = jnp.where(pad_j < jnp.repeat(n_pad, tm),
                         (pad_e << (pad_bit + 1)) | (1 << pad_bit) | pad_j, unused)
    keys = jnp.sort(jnp.concatenate([real_keys, pad_keys]))
    is_real = jnp.logical_and((keys >> pad_bit) & 1 == 0, keys < unused)
    pair = keys & ((1 << pad_bit) - 1)
    slot = sum((pair >= k * t_rows).astype(jnp.int32) for k in range(1, top_k))
    pos = np.arange(n_tiles * tm, dtype=np.int32)
    spare = n_pairs + pos % (2 * tm)
    dest = jnp.concatenate([spare[:tm], jnp.where(is_real, pair, spare)])
    src_tok = jnp.where(is_real, pair - slot * t_rows, 0)
    n_valid = jnp.sum(counts + n_pad) // tm
    last_e = jnp.max(jnp.where(counts > 0, jnp.arange(n_exp, dtype=jnp.int32), 0))
    tile_expert = jnp.where(np.arange(n_tiles) < n_valid, keys[::tm] >> (pad_bit + 1), last_e)
    experts = jnp.arange(n_exp, dtype=jnp.int32)
    owner = jnp.where(counts > 0, experts, n_exp)
    following = jnp.concatenate([lax.cummin(owner, reverse=True)[1:],
                                 jnp.full((1,), n_exp, jnp.int32)])
    following = jnp.where(following >= n_exp, -1, following)
    next_expert = jnp.sum(jnp.where(tile_expert[:, None] == experts[None, :], following[None, :], 0),
                          axis=1)
    return (tile_expert.astype(jnp.int32), next_expert.astype(jnp.int32),
            n_valid.reshape(1).astype(jnp.int32), src_tok.reshape(n_tiles, 1, tm),
            dest.reshape(n_tiles + 1, 1, tm))


def _combine_kernel(x1_ref, y0_ref, y1_ref, y2_ref, y3_ref, wts_ref, gate_ref, fg_ref, o_ref):
    w = wts_ref[...]
    tc, d = x1_ref.shape
    tpr = d // LANES
    cols = []
    for j in range(tpr):
        acc = w[:, 0:1] * y0_ref[pl.ds(j, tc, stride=tpr), :]
        for k, y_ref in enumerate((y1_ref, y2_ref, y3_ref), start=1):
            acc = acc + w[:, k:k + 1] * y_ref[pl.ds(j, tc, stride=tpr), :]
        cols.append(acc)
    x = x1_ref[...] + gate_ref[0] * jnp.concatenate(cols, axis=1)
    o_ref[...] = x * _rsqrt_mean_sq(x) * fg_ref[...]


def _combine(x1, y, wts, gate2, final_g, *, t_all, row0, rows_per_gate):
    rows, d = x1.shape
    tc = COMBINE_TILE
    y_specs = [
        pl.BlockSpec((tc * (d // LANES), LANES), functools.partial(
            lambda i, k: ((k * t_all + row0) // tc + i, 0), k=k))
        for k in range(TOP_K)
    ]
    return pl.pallas_call(
        _combine_kernel,
        grid=(rows // tc,),
        in_specs=[pl.BlockSpec((tc, d), lambda i: (i, 0))] + y_specs + [
            pl.BlockSpec((tc, LANES), lambda i: (i, 0)),
            pl.BlockSpec((1, 1, d), lambda i: ((i * tc) // rows_per_gate, 0, 0)),
            pl.BlockSpec((1, d), lambda i: (0, 0)),
        ],
        out_specs=pl.BlockSpec((tc, d), lambda i: (i, 0)),
        out_shape=jax.ShapeDtypeStruct((rows, d), F32),
        compiler_params=pltpu.CompilerParams(dimension_semantics=("arbitrary",)),
        name="combine",
    )(x1, y, y, y, y, wts, gate2, final_g)


def _dft_tables(n):
    def angles(m):
        k = np.arange(m, dtype=np.int64)
        return (2.0 * np.pi / m) * ((k[:, None] * k[None, :]) % m)
    an = angles(n)
    ac = angles(LANES)
    dftn = np.concatenate([np.cos(an), -np.sin(an)], axis=1).astype(np.float32)
    fcs = np.concatenate([np.cos(ac), np.sin(ac)], axis=1).astype(np.float32)
    return jnp.asarray(dftn).astype(BF16), jnp.asarray(fcs).astype(BF16)


def _rope_tables(n, qk_dim):
    quarter = qk_dim // 4
    tok = np.arange(n)
    pos = np.stack([tok // GRID_W, tok % GRID_W], axis=-1).astype(np.float64)
    freqs = ROPE_THETA ** (-np.arange(quarter, dtype=np.float64) / quarter)
    ang = (pos[:, :, None] * freqs).reshape(n, 2 * quarter)
    cos, sin = np.cos(ang), np.sin(ang)
    row_c, col_c = cos[:, :quarter], cos[:, quarter:]
    row_s, col_s = sin[:, :quarter], sin[:, quarter:]
    cos_map = np.concatenate([row_c, row_c, col_c, col_c], axis=-1)
    sin_map = np.concatenate([-row_s, row_s, -col_s, col_s], axis=-1)
    reps = LANES // qk_dim
    return (np.tile(cos_map, (1, reps)).astype(np.float32),
            np.tile(sin_map, (1, reps)).astype(np.float32))


def kernel(x_prompt, x_sample, cache_k, cache_v, c, c_ctx, w_mod, b_mod, norm1_g, w_in, lambda_q1,
           lambda_k1, lambda_q2, lambda_k2, subln_g, w_out, norm2_g, router_w, router_b, w_gate_up,
           b_gate_up, w_down, b_down, final_g):
    bsz, seq, d = x_prompt.shape
    dec_b, dec_seq, _ = x_sample.shape
    heads, past, qk_dim = cache_k.shape[2], cache_k.shape[3], cache_k.shape[5]
    n_exp = router_w.shape[-1]
    t_ctx, t_den = bsz * seq, dec_b * dec_seq
    t_all = t_ctx + t_den
    assert bsz % CTX_SEQS == 0 and t_den % (CTX_SEQS * seq) == 0 and t_ctx % dec_seq == 0
    assert 2 * qk_dim == LANES and dec_seq % GRID_W == 0
    assert (t_all * TOP_K) % MOE_TILE == 0 and t_all % COMBINE_TILE == 0

    cvec = jnp.concatenate([c_ctx[None, :], c, jnp.zeros((8 - 1 - dec_b, d), F32)], axis=0)
    mod = _modulation(cvec, w_mod[0], b_mod[0])[:, None, :]

    win = w_in[0].astype(BF16)
    wout = w_out[0].astype(BF16)
    rw = jnp.pad(router_w[0], ((0, 0), (0, LANES - n_exp)))
    rwh = rw.astype(BF16)
    rwl = (rw - rwh.astype(F32)).astype(BF16)
    rb = jnp.pad(router_b[0], (0, LANES - n_exp)).reshape(1, LANES)
    lamv = jnp.stack([lambda_q1[0], lambda_k1[0], lambda_q2[0], lambda_k2[0]], axis=0)
    g1 = norm1_g[0].reshape(1, d)
    g2 = norm2_g[0].reshape(1, d)
    subg = subln_g[0].reshape(1, LANES)
    dft_ctx, fcs = _dft_tables(seq)
    dft_den, _ = _dft_tables(dec_seq)
    cos, sin = _rope_tables(dec_seq, qk_dim)

    shared = (g1, win, lamv, subg, fcs)
    tail = (wout, g2, rwh, rwl, rb)
    x1_ctx, h2_all, idx_ctx, wts_ctx, cnt_ctx, new_k, new_v = _layer(
        x_prompt, mod, *shared, dft_ctx, *tail, n_experts=n_exp, mod_row0=0, mod_row_step=0,
        h2_rows=t_all, h2_block0=0, seqs=CTX_SEQS, zero_blocks=t_den // (CTX_SEQS * seq))
    ck = cache_k[:, 0].reshape(dec_b, heads, past, LANES)
    cv = cache_v[:, 0]
    x1_den, h2_all, idx_den, wts_den, cnt_den = _layer(
        x_sample, mod, *shared, dft_den, *tail, n_experts=n_exp, mod_row0=1, mod_row_step=1,
        h2_rows=t_all, h2_block0=t_ctx // dec_seq, cache=(ck, cv, cos, sin), h2_buf=h2_all)

    idx = jnp.concatenate([idx_ctx[:, :TOP_K], idx_den[:, :TOP_K]], axis=0)
    n_tiles = (t_all * TOP_K) // MOE_TILE + n_exp
    counts = (cnt_ctx + cnt_den)[0, :n_exp].astype(jnp.int32)
    tile_expert, next_expert, n_valid, src_tok, dest = _route(idx, counts, MOE_TILE, n_tiles)
    y = _moe(h2_all, tile_expert, next_expert, n_valid, src_tok, dest, w_gate_up[0], b_gate_up[0],
             w_down[0], b_down[0])

    gate2 = mod[:, :, 5 * d:]
    fg = final_g.reshape(1, d)
    y_prompt = _combine(x1_ctx, y, wts_ctx, gate2[0:1], fg, t_all=t_all, row0=0,
                        rows_per_gate=t_ctx)
    y_sample = _combine(x1_den, y, wts_den, gate2[1:1 + dec_b], fg, t_all=t_all, row0=t_ctx,
                        rows_per_gate=dec_seq)
    return (y_prompt.reshape(bsz, seq, d), y_sample.reshape(dec_b, dec_seq, d),
            new_k.reshape(bsz, 1, heads, seq, 2, qk_dim), new_v)
```

```python
import functools
import math

import numpy as np
import jax
import jax.numpy as jnp
from jax import lax
from jax.experimental import pallas as pl
from jax.experimental.pallas import tpu as pltpu

F32 = jnp.float32
BF16 = jnp.bfloat16

GRID_W = 64
N_FGROUPS = 4
TOP_K = 4
SWIGLU_LIMIT = 7.0
SWIGLU_ALPHA = 1.702
ROPE_THETA = 10000.0
NORM_EPS = 1e-6
LAMBDA_INIT = 0.8 - 0.6 * math.exp(-0.3 * 0)

LANES = 128
SUBLANES = 8
ROW_CHUNK = 256
CTX_SEQS = 2
MOE_TILE = 256
MOE_RING = 3
COMBINE_TILE = 256
VMEM_LIMIT = 56 * 1024 * 1024


def _rsqrt_mean_sq(x):
    return lax.rsqrt(jnp.mean(x * x, axis=-1, keepdims=True) + NORM_EPS)


def _sigmoid(z):
    return 1.0 / (1.0 + jnp.exp(-z))


def _mod_kernel(c_ref, w_ref, b_ref, o_ref):
    c = c_ref[...]
    s = c * _sigmoid(c)
    o_ref[...] = jnp.dot(s.astype(BF16), w_ref[...].astype(BF16),
                         preferred_element_type=F32) + b_ref[...]


def _modulation(cvec, w_mod, b_mod):
    rows, d = cvec.shape
    n_out = w_mod.shape[1]
    return pl.pallas_call(
        _mod_kernel,
        grid=(n_out // d,),
        in_specs=[
            pl.BlockSpec((rows, d), lambda j: (0, 0)),
            pl.BlockSpec((d, d), lambda j: (0, j)),
            pl.BlockSpec((1, d), lambda j: (0, j)),
        ],
        out_specs=pl.BlockSpec((rows, d), lambda j: (0, j)),
        out_shape=jax.ShapeDtypeStruct((rows, n_out), F32),
        name="modulation",
    )(cvec, w_mod, b_mod.reshape(1, n_out))


def _loop(n, body):
    if n == 1:
        body(0)
    else:
        def step(i, carry):
            body(i)
            return carry
        lax.fori_loop(0, n, step, 0)


def _layer_kernel(*refs, n_live, n_inputs, **static):
    if n_live is None:
        _layer_body(*refs, **static)
        return
    b = pl.program_id(0)
    h2_ref = refs[n_inputs + 1]
    pl.when(b < n_live)(functools.partial(_layer_body, *refs, **static))

    @pl.when(b >= n_live)
    def _():
        h2_ref[...] = jnp.zeros(h2_ref.shape, F32)


def _layer_body(*refs, n, seqs, n_cache, heads, n_experts, rope, emit_kv):
    it = iter(refs)
    x_ref = next(it); mod_ref = next(it); g1_ref = next(it); win_ref = next(it)
    lamv_ref = next(it); subg_ref = next(it); fcs_ref = next(it); dftn_ref = next(it)
    wout_ref = next(it); g2_ref = next(it); rwh_ref = next(it); rwl_ref = next(it); rb_ref = next(it)
    if rope:
        ck_ref = next(it); cv_ref = next(it); cos_ref = next(it); sin_ref = next(it)
        next(it)
    x1_ref = next(it); h2_ref = next(it); idx_ref = next(it); wts_ref = next(it); cnt_ref = next(it)
    if emit_kv:
        newk_ref = next(it); newv_ref = next(it)
    q1_scr = next(it); q2_scr = next(it); kall = next(it); vall = next(it)
    f_scr = next(it); stk = next(it); mix = next(it)

    d = x_ref.shape[-1]
    qk_w = heads * LANES

    @pl.when(pl.program_id(0) == 0)
    def _():
        cnt_ref[...] = jnp.zeros(cnt_ref.shape, F32)
    rc = min(ROW_CHUNK, n)
    n_chunks = n // rc

    def mod_row(j):
        return mod_ref[0, :, j * d:(j + 1) * d]

    shift1, scale1, gate1 = mod_row(0), mod_row(1), mod_row(2)
    shift2, scale2, gate2 = mod_row(3), mod_row(4), mod_row(5)
    del gate2

    lv = lamv_ref[...]
    lam = (jnp.exp(jnp.sum(lv[0:1] * lv[1:2], axis=-1, keepdims=True))
           - jnp.exp(jnp.sum(lv[2:3] * lv[3:4], axis=-1, keepdims=True)) + LAMBDA_INIT)

    if rope:
        for hd in range(heads):
            kall[0, hd, 0:n_cache, :] = ck_ref[0, hd].astype(BF16)
            vall[0, hd, 0:n_cache, :] = cv_ref[0, hd].astype(BF16)

    lane = lax.broadcasted_iota(jnp.int32, (rc, LANES), 1)
    first_map = lane < (LANES // 2)

    first_of_pair = jnp.bitwise_and(lane, 31) < 16

    def rotate(t, cos, sin):
        partner = jnp.where(first_of_pair, pltpu.roll(t, LANES - 16, 1), pltpu.roll(t, 16, 1))
        return t * cos + partner * sin

    for seq in range(seqs):
        _layer_sequence(seq, locals())


def _layer_sequence(seq, env):
    (x_ref, g1_ref, win_ref, subg_ref, fcs_ref, dftn_ref, wout_ref, g2_ref, rwh_ref, rwl_ref, rb_ref,
     x1_ref, h2_ref, idx_ref, wts_ref, cnt_ref, n, n_cache, heads, n_experts, rope, emit_kv, d,
     qk_w, rc, n_chunks, shift1, scale1, gate1, shift2, scale2, lam, first_map, rotate) = (
        env[k] for k in (
            "x_ref g1_ref win_ref subg_ref fcs_ref dftn_ref wout_ref g2_ref rwh_ref rwl_ref rb_ref "
            "x1_ref h2_ref idx_ref wts_ref cnt_ref n n_cache heads n_experts rope emit_kv d "
            "qk_w rc n_chunks shift1 scale1 gate1 shift2 scale2 lam first_map rotate").split())
    cos_ref, sin_ref = env.get("cos_ref"), env.get("sin_ref")
    newk_ref, newv_ref = env.get("newk_ref"), env.get("newv_ref")
    xs = x_ref.at[seq]
    q1s, q2s, ks, vs = (env[k].at[seq] for k in ("q1_scr", "q2_scr", "kall", "vall"))
    fs, stks, mixs = (env[k].at[seq] for k in ("f_scr", "stk", "mix"))
    row0 = seq * n

    def project(c):
        r0 = pl.multiple_of(c * rc, rc)
        x = xs[pl.ds(r0, rc), :]
        h = (x * _rsqrt_mean_sq(x) * g1_ref[...]) * (1.0 + scale1) + shift1
        p = jnp.dot(h.astype(BF16), win_ref[...], preferred_element_type=F32)
        if rope:
            cos = cos_ref[pl.ds(r0, rc), :]
            sin = sin_ref[pl.ds(r0, rc), :]
        for hd in range(heads):
            qh = p[:, hd * LANES:(hd + 1) * LANES]
            kh = p[:, qk_w + hd * LANES:qk_w + (hd + 1) * LANES]
            vh = p[:, 2 * qk_w + hd * LANES:2 * qk_w + (hd + 1) * LANES]
            if rope:
                qh = rotate(qh, cos, sin)
                kh = rotate(kh, cos, sin)
            if emit_kv:
                newk_ref[seq, 0, hd, pl.ds(r0, rc), :] = kh
                newv_ref[seq, 0, hd, pl.ds(r0, rc), :] = vh
            qs = qh * (LANES // 2) ** -0.5
            head = slice(hd * LANES, (hd + 1) * LANES)
            q1s[pl.ds(r0, rc), head] = jnp.where(first_map, qs, 0.0).astype(BF16)
            q2s[pl.ds(r0, rc), head] = jnp.where(first_map, 0.0, qs).astype(BF16)
            ks[hd, pl.ds(n_cache + r0, rc), :] = kh.astype(BF16)
            vs[hd, pl.ds(n_cache + r0, rc), :] = vh.astype(BF16)
        fs[pl.ds(r0, rc), :] = p[:, 3 * qk_w:].astype(BF16)

    _loop(n_chunks, project)

    contract_last = (((1,), (1,)), ((), ()))

    def softmax(s):
        e = jnp.exp(s - jnp.max(s, axis=-1, keepdims=True))
        return e * (1.0 / jnp.sum(e, axis=-1, keepdims=True))

    def attend(c):
        r0 = pl.multiple_of(c * rc, rc)
        for hd in range(heads):
            kh = ks[hd]
            s1 = lax.dot_general(q1s[pl.ds(r0, rc), hd * LANES:(hd + 1) * LANES], kh,
                                 contract_last, preferred_element_type=F32)
            s2 = lax.dot_general(q2s[pl.ds(r0, rc), hd * LANES:(hd + 1) * LANES], kh,
                                 contract_last, preferred_element_type=F32)
            a = softmax(s1) - lam * softmax(s2)
            o = jnp.dot(a.astype(BF16), vs[hd], preferred_element_type=F32)
            o = o * _rsqrt_mean_sq(o) * subg_ref[...] * (1.0 - LAMBDA_INIT)
            mixs[pl.ds(r0, rc), hd * LANES:(hd + 1) * LANES] = o.astype(BF16)

    _loop(n_chunks, attend)

    def dft_channels(c):
        r0 = pl.multiple_of(c * rc, rc)
        for g in range(N_FGROUPS):
            a = jnp.dot(fs[pl.ds(r0, rc), g * LANES:(g + 1) * LANES], fcs_ref[...],
                        preferred_element_type=F32)
            stks[pl.ds(r0, rc), g * LANES:(g + 1) * LANES] = a[:, :LANES].astype(BF16)
            stks[pl.ds(pl.multiple_of(n + r0, rc), rc), g * LANES:(g + 1) * LANES] = (
                a[:, LANES:].astype(BF16))

    _loop(n_chunks, dft_channels)

    fscale = 1.0 / math.sqrt(n * LANES)

    def dft_positions(c):
        r0 = pl.multiple_of(c * rc, rc)
        y = jnp.dot(dftn_ref[pl.ds(r0, rc), :], stks[...], preferred_element_type=F32) * fscale
        mixs[pl.ds(r0, rc), qk_w:] = y.astype(BF16)

    _loop(n_chunks, dft_positions)

    klane = lax.broadcasted_iota(jnp.int32, (rc, LANES), 1)
    neg_inf = jnp.float32(-jnp.inf)

    def tail(c):
        r0 = pl.multiple_of(c * rc, rc)
        x = xs[pl.ds(r0, rc), :]
        mixed = jnp.dot(mixs[pl.ds(r0, rc), :], wout_ref[...], preferred_element_type=F32)
        x1 = x + gate1 * mixed
        x1_ref[pl.ds(row0 + r0, rc), :] = x1
        h2 = (x1 * _rsqrt_mean_sq(x1) * g2_ref[...]) * (1.0 + scale2) + shift2
        tpr = d // LANES
        for j in range(tpr):
            h2_ref[pl.ds((row0 + r0) * tpr + j, rc, stride=tpr), :] = (
                h2[:, j * LANES:(j + 1) * LANES])
        hi = h2.astype(BF16)
        lo = (h2 - hi.astype(F32)).astype(BF16)
        logits = (jnp.dot(hi, rwh_ref[...], preferred_element_type=F32)
                  + jnp.dot(lo, rwh_ref[...], preferred_element_type=F32)
                  + jnp.dot(hi, rwl_ref[...], preferred_element_type=F32))
        l = jnp.where(klane < n_experts, logits + rb_ref[...], neg_inf)
        lt = jnp.transpose(l)[:cnt_ref.shape[0]]
        eid = lax.broadcasted_iota(jnp.int32, lt.shape, 0).astype(F32)
        vals, ids = [], []
        for _ in range(TOP_K):
            m = jnp.max(lt, axis=0, keepdims=True)
            i = jnp.min(jnp.where(lt == m, eid, float(LANES)), axis=0, keepdims=True)
            vals.append(m)
            ids.append(i)
            lt = jnp.where(eid == i, neg_inf, lt)
        es = [jnp.exp(v - vals[0]) for v in vals]
        inv = 1.0 / functools.reduce(lambda a, b: a + b, es)
        krow = lax.broadcasted_iota(jnp.int32, (SUBLANES, rc), 0)
        idx_t = jnp.zeros((SUBLANES, rc), F32)
        wts_t = jnp.zeros((SUBLANES, rc), F32)
        for k in range(TOP_K):
            idx_t = jnp.where(krow == k, ids[k], idx_t)
            wts_t = jnp.where(krow == k, es[k] * inv, wts_t)
        pad = jnp.zeros((LANES - SUBLANES, rc), F32)
        idx_ref[pl.ds(row0 + r0, rc), :] = jnp.transpose(
            jnp.concatenate([idx_t, pad], axis=0)).astype(jnp.int32)
        wts_ref[pl.ds(row0 + r0, rc), :] = jnp.transpose(jnp.concatenate([wts_t, pad], axis=0))
        cnt_ref[...] += functools.reduce(lambda a, b: a + b,
                                         [jnp.where(eid == i, 1.0, 0.0) for i in ids])

    _loop(n_chunks, tail)


def _const_spec(shape):
    nd = len(shape)
    return pl.BlockSpec(shape, lambda b: (0,) * nd, pipeline_mode=pl.Buffered(1))


def _layer(x, mod, g1, win, lamv, subg, fcs, dftn, wout, g2, rwh, rwl, rb, *, n_experts,
           mod_row0, mod_row_step, h2_rows, h2_block0, seqs=1, zero_blocks=0, cache=None,
           h2_buf=None):
    n_seq, n, d = x.shape
    bsz = n_seq // seqs
    heads = win.shape[1] // (4 * LANES)
    rope = cache is not None
    n_cache = cache[0].shape[2] if rope else 0
    nk = n_cache + n
    live = lambda b: jnp.minimum(b, bsz - 1)
    cnt_shape = (-(-n_experts // SUBLANES) * SUBLANES, min(ROW_CHUNK, n))
    in_specs = [
        pl.BlockSpec((seqs, n, d), lambda b: (live(b), 0, 0)),
        pl.BlockSpec((1, 1, mod.shape[-1]),
                     lambda b: (mod_row0 + mod_row_step * live(b), 0, 0)),
        _const_spec(g1.shape), _const_spec(win.shape), _const_spec(lamv.shape),
        _const_spec(subg.shape), _const_spec(fcs.shape), _const_spec(dftn.shape),
        _const_spec(wout.shape), _const_spec(g2.shape), _const_spec(rwh.shape),
        _const_spec(rwl.shape), _const_spec(rb.shape),
    ]
    args = [x, mod, g1, win, lamv, subg, fcs, dftn, wout, g2, rwh, rwl, rb]
    out_specs = [
        pl.BlockSpec((seqs * n, d), lambda b: (live(b), 0)),
        pl.BlockSpec((seqs * n * (d // LANES), LANES), lambda b: (h2_block0 + b, 0)),
        pl.BlockSpec((seqs * n, LANES), lambda b: (live(b), 0)),
        pl.BlockSpec((seqs * n, LANES), lambda b: (live(b), 0)),
        pl.BlockSpec(cnt_shape, lambda b: (0, 0)),
    ]
    out_shape = [
        jax.ShapeDtypeStruct((n_seq * n, d), F32),
        jax.ShapeDtypeStruct((h2_rows * (d // LANES), LANES), F32),
        jax.ShapeDtypeStruct((n_seq * n, LANES), jnp.int32),
        jax.ShapeDtypeStruct((n_seq * n, LANES), F32),
        jax.ShapeDtypeStruct(cnt_shape, F32),
    ]
    aliases = {}
    if rope:
        ck, cv, cos, sin = cache
        in_specs += [
            pl.BlockSpec((1, heads, n_cache, LANES), lambda b: (live(b), 0, 0, 0)),
            pl.BlockSpec((1, heads, n_cache, LANES), lambda b: (live(b), 0, 0, 0)),
            _const_spec(cos.shape), _const_spec(sin.shape),
            pl.BlockSpec(memory_space=pl.ANY),
        ]
        args += [ck, cv, cos, sin, h2_buf]
        aliases = {len(args) - 1: 1}
    else:
        kv_spec = pl.BlockSpec((seqs, 1, heads, n, LANES), lambda b: (live(b), 0, 0, 0, 0))
        out_specs += [kv_spec, kv_spec]
        kv_shape = jax.ShapeDtypeStruct((n_seq, 1, heads, n, LANES), F32)
        out_shape += [kv_shape, kv_shape]
    scratch = [
        pltpu.VMEM((seqs, n, heads * LANES), BF16),
        pltpu.VMEM((seqs, n, heads * LANES), BF16),
        pltpu.VMEM((seqs, heads, nk, LANES), BF16),
        pltpu.VMEM((seqs, heads, nk, LANES), BF16),
        pltpu.VMEM((seqs, n, N_FGROUPS * LANES), BF16),
        pltpu.VMEM((seqs, 2 * n, N_FGROUPS * LANES), BF16),
        pltpu.VMEM((seqs, n, d), BF16),
    ]
    kern = functools.partial(_layer_kernel, n_live=bsz if zero_blocks else None,
                             n_inputs=len(args), n=n, seqs=seqs, n_cache=n_cache, heads=heads,
                             n_experts=n_experts, rope=rope, emit_kv=not rope)
    return pl.pallas_call(
        kern,
        grid=(bsz + zero_blocks,),
        in_specs=in_specs,
        out_specs=out_specs,
        out_shape=out_shape,
        scratch_shapes=scratch,
        input_output_aliases=aliases,
        compiler_params=pltpu.CompilerParams(dimension_semantics=("arbitrary",),
                                             vmem_limit_bytes=VMEM_LIMIT),
        name="layer_latent" if rope else "layer_context",
    )(*args)


def _moe_kernel(texp_ref, next_ref, nvalid_ref, tok_ref, tok_next_ref, tok_next2_ref, dst_ref,
                dst_prev_ref, h2_hbm, wgu_hbm, bgu_ref, wd_hbm, bd_ref, y_hbm, xbuf0, xbuf1, xbuf2,
                ybuf0, ybuf1, ybuf2, wgu_f32, wd_f32, wgu_bf, wd_bf, gsem, ssem, wsem, run_ref, *,
                tm):
    i = pl.program_id(0)
    nv = nvalid_ref[0]
    d_ff, d = wd_bf.shape
    tpr = d // LANES

    def weight_copies(e, s):
        return (pltpu.make_async_copy(wgu_hbm.at[e], wgu_f32.at[s], wsem.at[s, 0]),
                pltpu.make_async_copy(wd_hbm.at[e], wd_f32.at[s], wsem.at[s, 1]))

    def token_rows(t):
        return pl.ds(pl.multiple_of(t * tpr, tpr), tpr)

    xbuf, ybuf = (xbuf0, xbuf1, xbuf2), (ybuf0, ybuf1, ybuf2)

    def gather_row(idx_ref, slot, r):
        return pltpu.make_async_copy(h2_hbm.at[token_rows(idx_ref[0, 0, r]), :],
                                     xbuf[slot].at[pl.ds(r * tpr, tpr), :], gsem.at[slot])

    def scatter_row(idx_ref, slot, r):
        return pltpu.make_async_copy(ybuf[slot].at[pl.ds(r * tpr, tpr), :],
                                     y_hbm.at[token_rows(idx_ref[0, 0, r]), :], ssem.at[slot])

    def step(slot):
        ahead, behind = (slot + 1) % MOE_RING, (slot + 2) % MOE_RING
        for r in range(tm):
            gather_row(tok_ref, slot, r).wait()

        @pl.when(i >= 2)
        def _():
            for r in range(tm):
                scatter_row(dst_ref, slot, r).wait()

        for r in range(tm):
            gather_row(tok_next2_ref, behind, r).start()
        for r in range(tm):
            scatter_row(dst_prev_ref, behind, r).start()

        x = jnp.concatenate(
            [xbuf[slot][pl.ds(j, tm, stride=tpr), :].astype(BF16) for j in range(tpr)], axis=1)
        gu = jnp.dot(x, wgu_bf[...], preferred_element_type=F32) + bgu_ref[0]
        glu = jnp.minimum(gu[:, :d_ff], SWIGLU_LIMIT)
        lin = jnp.clip(gu[:, d_ff:], -SWIGLU_LIMIT, SWIGLU_LIMIT)
        act = glu * _sigmoid(SWIGLU_ALPHA * glu) * (lin + 1.0)
        y = jnp.dot(act.astype(BF16), wd_bf[...], preferred_element_type=F32) + bd_ref[0]
        for j in range(tpr):
            ybuf[slot][pl.ds(j, tm, stride=tpr), :] = y[:, j * LANES:(j + 1) * LANES]

        @pl.when(i == nv - 1)
        def _():
            for r in range(tm):
                scatter_row(dst_ref, slot, r).start()
            for r in range(tm):
                scatter_row(dst_ref, slot, r).wait()
            for r in range(tm):
                scatter_row(dst_prev_ref, behind, r).wait()

            @pl.when(i >= 1)
            def _():
                for r in range(tm):
                    scatter_row(dst_prev_ref, ahead, r).wait()

            for r in range(tm):
                gather_row(tok_next_ref, ahead, r).wait()
            for r in range(tm):
                gather_row(tok_next2_ref, behind, r).wait()

    @pl.when(i < nv)
    def _():
        @pl.when(i == 0)
        def _():
            run_ref[0] = 0
            for c in weight_copies(texp_ref[0], 0):
                c.start(priority=1)
            for r in range(tm):
                gather_row(tok_ref, 0, r).start()
            for r in range(tm):
                gather_row(tok_next_ref, 1, r).start()
            spare0 = y_hbm.shape[0] - 2 * tm * tpr
            for s in range(MOE_RING):
                ybuf[s][...] = jnp.zeros(ybuf[s].shape, F32)
            fills = [pltpu.make_async_copy(
                ybuf[s], y_hbm.at[pl.ds(spare0 + s * tm * tpr, tm * tpr), :], ssem.at[s])
                for s in range(2)]
            for f in fills:
                f.start()
            for f in fills:
                f.wait()

        @pl.when(jnp.logical_or(i == 0, texp_ref[i] != texp_ref[jnp.maximum(i - 1, 0)]))
        def _():
            run = run_ref[0]
            ws = run % 2
            for c in weight_copies(texp_ref[i], ws):
                c.wait()

            @pl.when(next_ref[i] >= 0)
            def _():
                for c in weight_copies(next_ref[i], 1 - ws):
                    c.start(priority=1)

            wgu_bf[...] = wgu_f32[ws].astype(BF16)
            wd_bf[...] = wd_f32[ws].astype(BF16)
            run_ref[0] = run + 1

        for s in range(MOE_RING):
            pl.when(i % MOE_RING == s)(functools.partial(step, s))


def _moe(h2, tile_expert, next_expert, n_valid, src_tok, dest, wgu, bgu, wd, bd):
    n_exp, d, two_f = wgu.shape
    tpr = d // LANES
    t_rows = h2.shape[0] // tpr
    d_ff = two_f // 2
    n_tiles = tile_expert.shape[0]
    tm = src_tok.shape[-1]
    last = n_tiles - 1
    smem_tile = functools.partial(pl.BlockSpec, (1, 1, tm), memory_space=pltpu.SMEM)
    grid_spec = pltpu.PrefetchScalarGridSpec(
        num_scalar_prefetch=3,
        grid=(n_tiles,),
        in_specs=[
            smem_tile(lambda i, te, nx, nv: (i, 0, 0)),
            smem_tile(lambda i, te, nx, nv: (jnp.minimum(i + 1, last), 0, 0)),
            smem_tile(lambda i, te, nx, nv: (jnp.minimum(i + 2, last), 0, 0)),
            smem_tile(lambda i, te, nx, nv: (i + 1, 0, 0)),
            smem_tile(lambda i, te, nx, nv: (i, 0, 0)),
            pl.BlockSpec(memory_space=pl.ANY),
            pl.BlockSpec(memory_space=pl.ANY),
            pl.BlockSpec((1, 1, two_f), lambda i, te, nx, nv: (te[i], 0, 0)),
            pl.BlockSpec(memory_space=pl.ANY),
            pl.BlockSpec((1, 1, d), lambda i, te, nx, nv: (te[i], 0, 0)),
        ],
        out_specs=pl.BlockSpec(memory_space=pl.ANY),
        scratch_shapes=[
            *([pltpu.VMEM((tm * tpr, LANES), F32)] * (2 * MOE_RING)),
            pltpu.VMEM((2, d, two_f), F32),
            pltpu.VMEM((2, d_ff, d), F32),
            pltpu.VMEM((d, two_f), BF16),
            pltpu.VMEM((d_ff, d), BF16),
            pltpu.SemaphoreType.DMA((MOE_RING,)),
            pltpu.SemaphoreType.DMA((MOE_RING,)),
            pltpu.SemaphoreType.DMA((2, 2)),
            pltpu.SMEM((1,), jnp.int32),
        ],
    )
    return pl.pallas_call(
        functools.partial(_moe_kernel, tm=tm),
        grid_spec=grid_spec,
        out_shape=jax.ShapeDtypeStruct(((TOP_K * t_rows + 2 * tm) * tpr, LANES), F32),
        compiler_params=pltpu.CompilerParams(dimension_semantics=("arbitrary",),
                                             vmem_limit_bytes=VMEM_LIMIT),
        name="routed_moe",
    )(tile_expert, next_expert, n_valid, src_tok, src_tok, src_tok, dest, dest, h2, wgu,
      bgu.reshape(n_exp, 1, two_f), wd, bd.reshape(n_exp, 1, d))


def _route(idx, counts, tm, n_tiles):
    t_rows, top_k = idx.shape
    n_exp = counts.shape[0]
    n_pairs = t_rows * top_k
    pad_bit = 16
    assert n_pairs <= 1 << pad_bit and tm <= 1 << pad_bit and n_tiles * tm == n_pairs + n_exp * tm
    pair_ids = np.arange(n_pairs, dtype=np.int32)
    real_keys = (idx.T.reshape(-1) << (pad_bit + 1)) | pair_ids
    pad_e = np.repeat(np.arange(n_exp, dtype=np.int32), tm)
    pad_j = np.tile(np.arange(tm, dtype=np.int32), n_exp)
    n_pad = (-counts) % tm
    unused = n_exp << (pad_bit + 1)
    pad_keys = jnp.where(pad_j < jnp.repeat(n_pad, tm),
                         (pad_e << (pad_bit + 1)) | (1 << pad_bit) | pad_j, unused)
    keys = jnp.sort(jnp.concatenate([real_keys, pad_keys]))
    is_real = jnp.logical_and((keys >> pad_bit) & 1 == 0, keys < unused)
    pair = keys & ((1 << pad_bit) - 1)
    slot = sum((pair >= k * t_rows).astype(jnp.int32) for k in range(1, top_k))
    pos = np.arange(n_tiles * tm, dtype=np.int32)
    spare = n_pairs + pos % (2 * tm)
    dest = jnp.concatenate([spare[:tm], jnp.where(is_real, pair, spare)])
    src_tok = jnp.where(is_real, pair - slot * t_rows, 0)
    n_valid = jnp.sum(counts + n_pad) // tm
    last_e = jnp.max(jnp.where(counts > 0, jnp.arange(n_exp, dtype=jnp.int32), 0))
    tile_expert = jnp.where(np.arange(n_tiles) < n_valid, keys[::tm] >> (pad_bit + 1), last_e)
    experts = jnp.arange(n_exp, dtype=jnp.int32)
    owner = jnp.where(counts > 0, experts, n_exp)
    following = jnp.concatenate([lax.cummin(owner, reverse=True)[1:],
                                 jnp.full((1,), n_exp, jnp.int32)])
    following = jnp.where(following >= n_exp, -1, following)
    next_expert = jnp.sum(jnp.where(tile_expert[:, None] == experts[None, :], following[None, :], 0),
                          axis=1)
    return (tile_expert.astype(jnp.int32), next_expert.astype(jnp.int32),
            n_valid.reshape(1).astype(jnp.int32), src_tok.reshape(n_tiles, 1, tm),
            dest.reshape(n_tiles + 1, 1, tm))


def _combine_kernel(x1_ref, y0_ref, y1_ref, y2_ref, y3_ref, wts_ref, gate_ref, fg_ref, o_ref):
    w = wts_ref[...]
    tc, d = x1_ref.shape
    tpr = d // LANES
    cols = []
    for j in range(tpr):
        acc = w[:, 0:1] * y0_ref[pl.ds(j, tc, stride=tpr), :]
        for k, y_ref in enumerate((y1_ref, y2_ref, y3_ref), start=1):
            acc = acc + w[:, k:k + 1] * y_ref[pl.ds(j, tc, stride=tpr), :]
        cols.append(acc)
    x = x1_ref[...] + gate_ref[0] * jnp.concatenate(cols, axis=1)
    o_ref[...] = x * _rsqrt_mean_sq(x) * fg_ref[...]


def _combine(x1, y, wts, gate2, final_g, *, t_all, row0, rows_per_gate):
    rows, d = x1.shape
    tc = COMBINE_TILE
    y_specs = [
        pl.BlockSpec((tc * (d // LANES), LANES), functools.partial(
            lambda i, k: ((k * t_all + row0) // tc + i, 0), k=k))
        for k in range(TOP_K)
    ]
    return pl.pallas_call(
        _combine_kernel,
        grid=(rows // tc,),
        in_specs=[pl.BlockSpec((tc, d), lambda i: (i, 0))] + y_specs + [
            pl.BlockSpec((tc, LANES), lambda i: (i, 0)),
            pl.BlockSpec((1, 1, d), lambda i: ((i * tc) // rows_per_gate, 0, 0)),
            pl.BlockSpec((1, d), lambda i: (0, 0)),
        ],
        out_specs=pl.BlockSpec((tc, d), lambda i: (i, 0)),
        out_shape=jax.ShapeDtypeStruct((rows, d), F32),
        compiler_params=pltpu.CompilerParams(dimension_semantics=("arbitrary",)),
        name="combine",
    )(x1, y, y, y, y, wts, gate2, final_g)


def _dft_tables(n):
    def angles(m):
        k = np.arange(m, dtype=np.int64)
        return (2.0 * np.pi / m) * ((k[:, None] * k[None, :]) % m)
    an = angles(n)
    ac = angles(LANES)
    dftn = np.concatenate([np.cos(an), -np.sin(an)], axis=1).astype(np.float32)
    fcs = np.concatenate([np.cos(ac), np.sin(ac)], axis=1).astype(np.float32)
    return jnp.asarray(dftn).astype(BF16), jnp.asarray(fcs).astype(BF16)


def _rope_tables(n, qk_dim):
    quarter = qk_dim // 4
    tok = np.arange(n)
    pos = np.stack([tok // GRID_W, tok % GRID_W], axis=-1).astype(np.float64)
    freqs = ROPE_THETA ** (-np.arange(quarter, dtype=np.float64) / quarter)
    ang = (pos[:, :, None] * freqs).reshape(n, 2 * quarter)
    cos, sin = np.cos(ang), np.sin(ang)
    row_c, col_c = cos[:, :quarter], cos[:, quarter:]
    row_s, col_s = sin[:, :quarter], sin[:, quarter:]
    cos_map = np.concatenate([row_c, row_c, col_c, col_c], axis=-1)
    sin_map = np.concatenate([-row_s, row_s, -col_s, col_s], axis=-1)
    reps = LANES // qk_dim
    return (np.tile(cos_map, (1, reps)).astype(np.float32),
            np.tile(sin_map, (1, reps)).astype(np.float32))


def kernel(x_prompt, x_sample, cache_k, cache_v, c, c_ctx, w_mod, b_mod, norm1_g, w_in, lambda_q1,
           lambda_k1, lambda_q2, lambda_k2, subln_g, w_out, norm2_g, router_w, router_b, w_gate_up,
           b_gate_up, w_down, b_down, final_g):
    bsz, seq, d = x_prompt.shape
    dec_b, dec_seq, _ = x_sample.shape
    heads, past, qk_dim = cache_k.shape[2], cache_k.shape[3], cache_k.shape[5]
    n_exp = router_w.shape[-1]
    t_ctx, t_den = bsz * seq, dec_b * dec_seq
    t_all = t_ctx + t_den
    assert bsz % CTX_SEQS == 0 and t_den % (CTX_SEQS * seq) == 0 and t_ctx % dec_seq == 0
    assert 2 * qk_dim == LANES and dec_seq % GRID_W == 0
    assert (t_all * TOP_K) % MOE_TILE == 0 and t_all % COMBINE_TILE == 0

    cvec = jnp.concatenate([c_ctx[None, :], c, jnp.zeros((8 - 1 - dec_b, d), F32)], axis=0)
    mod = _modulation(cvec, w_mod[0], b_mod[0])[:, None, :]

    win = w_in[0].astype(BF16)
    wout = w_out[0].astype(BF16)
    rw = jnp.pad(router_w[0], ((0, 0), (0, LANES - n_exp)))
    rwh = rw.astype(BF16)
    rwl = (rw - rwh.astype(F32)).astype(BF16)
    rb = jnp.pad(router_b[0], (0, LANES - n_exp)).reshape(1, LANES)
    lamv = jnp.stack([lambda_q1[0], lambda_k1[0], lambda_q2[0], lambda_k2[0]], axis=0)
    g1 = norm1_g[0].reshape(1, d)
    g2 = norm2_g[0].reshape(1, d)
    subg = subln_g[0].reshape(1, LANES)
    dft_ctx, fcs = _dft_tables(seq)
    dft_den, _ = _dft_tables(dec_seq)
    cos, sin = _rope_tables(dec_seq, qk_dim)

    shared = (g1, win, lamv, subg, fcs)
    tail = (wout, g2, rwh, rwl, rb)
    x1_ctx, h2_all, idx_ctx, wts_ctx, cnt_ctx, new_k, new_v = _layer(
        x_prompt, mod, *shared, dft_ctx, *tail, n_experts=n_exp, mod_row0=0, mod_row_step=0,
        h2_rows=t_all, h2_block0=0, seqs=CTX_SEQS, zero_blocks=t_den // (CTX_SEQS * seq))
    ck = cache_k[:, 0].reshape(dec_b, heads, past, LANES)
    cv = cache_v[:, 0]
    x1_den, h2_all, idx_den, wts_den, cnt_den = _layer(
        x_sample, mod, *shared, dft_den, *tail, n_experts=n_exp, mod_row0=1, mod_row_step=1,
        h2_rows=t_all, h2_block0=t_ctx // dec_seq, cache=(ck, cv, cos, sin), h2_buf=h2_all)

    idx = jnp.concatenate([idx_ctx[:, :TOP_K], idx_den[:, :TOP_K]], axis=0)
    n_tiles = (t_all * TOP_K) // MOE_TILE + n_exp
    counts = jnp.sum(cnt_ctx + cnt_den, axis=1)[:n_exp].astype(jnp.int32)
    tile_expert, next_expert, n_valid, src_tok, dest = _route(idx, counts, MOE_TILE, n_tiles)
    y = _moe(h2_all, tile_expert, next_expert, n_valid, src_tok, dest, w_gate_up[0], b_gate_up[0],
             w_down[0], b_down[0])

    gate2 = mod[:, :, 5 * d:]
    fg = final_g.reshape(1, d)
    y_prompt = _combine(x1_ctx, y, wts_ctx, gate2[0:1], fg, t_all=t_all, row0=0,
                        rows_per_gate=t_ctx)
    y_sample = _combine(x1_den, y, wts_den, gate2[1:1 + dec_b], fg, t_all=t_all, row0=t_ctx,
                        rows_per_gate=dec_seq)
    return (y_prompt.reshape(bsz, seq, d), y_sample.reshape(dec_b, dec_seq, d),
            new_k.reshape(bsz, 1, heads, seq, 2, qk_dim), new_v)
```

```python
import functools
import math

import numpy as np
import jax
import jax.numpy as jnp
from jax import lax
from jax.experimental import pallas as pl
from jax.experimental.pallas import tpu as pltpu

F32 = jnp.float32
BF16 = jnp.bfloat16

GRID_W = 64
N_FGROUPS = 4
TOP_K = 4
SWIGLU_LIMIT = 7.0
SWIGLU_ALPHA = 1.702
ROPE_THETA = 10000.0
NORM_EPS = 1e-6
LAMBDA_INIT = 0.8 - 0.6 * math.exp(-0.3 * 0)

LANES = 128
SUBLANES = 8
ROW_CHUNK = 256
CTX_SEQS = 2
MOE_TILE = 256
MOE_RING = 3
COMBINE_TILE = 512
VMEM_LIMIT = 56 * 1024 * 1024


def _rsqrt_mean_sq(x):
    return lax.rsqrt(jnp.mean(x * x, axis=-1, keepdims=True) + NORM_EPS)


def _sigmoid(z):
    return 1.0 / (1.0 + jnp.exp(-z))


def _mod_kernel(c_ref, w_ref, b_ref, o_ref):
    c = c_ref[...]
    s = c * _sigmoid(c)
    o_ref[...] = jnp.dot(s.astype(BF16), w_ref[...].astype(BF16),
                         preferred_element_type=F32) + b_ref[...]


def _modulation(cvec, w_mod, b_mod):
    rows, d = cvec.shape
    n_out = w_mod.shape[1]
    return pl.pallas_call(
        _mod_kernel,
        grid=(n_out // d,),
        in_specs=[
            pl.BlockSpec((rows, d), lambda j: (0, 0)),
            pl.BlockSpec((d, d), lambda j: (0, j)),
            pl.BlockSpec((1, d), lambda j: (0, j)),
        ],
        out_specs=pl.BlockSpec((rows, d), lambda j: (0, j)),
        out_shape=jax.ShapeDtypeStruct((rows, n_out), F32),
        name="modulation",
    )(cvec, w_mod, b_mod.reshape(1, n_out))


def _loop(n, body):
    if n == 1:
        body(0)
    else:
        def step(i, carry):
            body(i)
            return carry
        lax.fori_loop(0, n, step, 0)


def _layer_kernel(*refs, n_live, n_inputs, **static):
    if n_live is None:
        _layer_body(*refs, **static)
        return
    b = pl.program_id(0)
    h2_ref = refs[n_inputs + 1]
    pl.when(b < n_live)(functools.partial(_layer_body, *refs, **static))

    @pl.when(b >= n_live)
    def _():
        h2_ref[...] = jnp.zeros(h2_ref.shape, F32)


def _layer_body(*refs, n, seqs, n_cache, heads, n_experts, rope, emit_kv):
    it = iter(refs)
    x_ref = next(it); mod_ref = next(it); g1_ref = next(it); win_ref = next(it)
    lamv_ref = next(it); subg_ref = next(it); fcs_ref = next(it); dftn_ref = next(it)
    wout_ref = next(it); g2_ref = next(it); rwh_ref = next(it); rwl_ref = next(it); rb_ref = next(it)
    if rope:
        ck_ref = next(it); cv_ref = next(it); cos_ref = next(it); sin_ref = next(it)
        next(it)
    x1_ref = next(it); h2_ref = next(it); idx_ref = next(it); wts_ref = next(it); cnt_ref = next(it)
    if emit_kv:
        newk_ref = next(it); newv_ref = next(it)
    q1_scr = next(it); q2_scr = next(it); kall = next(it); vall = next(it)
    f_scr = next(it); stk = next(it); mix = next(it)

    d = x_ref.shape[-1]
    qk_w = heads * LANES

    @pl.when(pl.program_id(0) == 0)
    def _():
        cnt_ref[...] = jnp.zeros(cnt_ref.shape, F32)
    rc = min(ROW_CHUNK, n)
    n_chunks = n // rc

    def mod_row(j):
        return mod_ref[0, :, j * d:(j + 1) * d]

    shift1, scale1, gate1 = mod_row(0), mod_row(1), mod_row(2)
    shift2, scale2, gate2 = mod_row(3), mod_row(4), mod_row(5)
    del gate2

    lv = lamv_ref[...]
    lam = (jnp.exp(jnp.sum(lv[0:1] * lv[1:2], axis=-1, keepdims=True))
           - jnp.exp(jnp.sum(lv[2:3] * lv[3:4], axis=-1, keepdims=True)) + LAMBDA_INIT)

    if rope:
        for hd in range(heads):
            kall[0, hd, 0:n_cache, :] = ck_ref[0, hd].astype(BF16)
            vall[0, hd, 0:n_cache, :] = cv_ref[0, hd].astype(BF16)

    lane = lax.broadcasted_iota(jnp.int32, (rc, LANES), 1)
    first_map = lane < (LANES // 2)

    first_of_pair = jnp.bitwise_and(lane, 31) < 16

    def rotate(t, cos, sin):
        partner = jnp.where(first_of_pair, pltpu.roll(t, LANES - 16, 1), pltpu.roll(t, 16, 1))
        return t * cos + partner * sin

    for seq in range(seqs):
        _layer_sequence(seq, locals())


def _layer_sequence(seq, env):
    (x_ref, g1_ref, win_ref, subg_ref, fcs_ref, dftn_ref, wout_ref, g2_ref, rwh_ref, rwl_ref, rb_ref,
     x1_ref, h2_ref, idx_ref, wts_ref, cnt_ref, n, n_cache, heads, n_experts, rope, emit_kv, d,
     qk_w, rc, n_chunks, shift1, scale1, gate1, shift2, scale2, lam, first_map, rotate) = (
        env[k] for k in (
            "x_ref g1_ref win_ref subg_ref fcs_ref dftn_ref wout_ref g2_ref rwh_ref rwl_ref rb_ref "
            "x1_ref h2_ref idx_ref wts_ref cnt_ref n n_cache heads n_experts rope emit_kv d "
            "qk_w rc n_chunks shift1 scale1 gate1 shift2 scale2 lam first_map rotate").split())
    cos_ref, sin_ref = env.get("cos_ref"), env.get("sin_ref")
    newk_ref, newv_ref = env.get("newk_ref"), env.get("newv_ref")
    xs = x_ref.at[seq]
    q1s, q2s, ks, vs = (env[k].at[seq] for k in ("q1_scr", "q2_scr", "kall", "vall"))
    fs, stks, mixs = (env[k].at[seq] for k in ("f_scr", "stk", "mix"))
    row0 = seq * n

    def project(c):
        r0 = pl.multiple_of(c * rc, rc)
        x = xs[pl.ds(r0, rc), :]
        h = (x * _rsqrt_mean_sq(x) * g1_ref[...]) * (1.0 + scale1) + shift1
        p = jnp.dot(h.astype(BF16), win_ref[...], preferred_element_type=F32)
        if rope:
            cos = cos_ref[pl.ds(r0, rc), :]
            sin = sin_ref[pl.ds(r0, rc), :]
        for hd in range(heads):
            qh = p[:, hd * LANES:(hd + 1) * LANES]
            kh = p[:, qk_w + hd * LANES:qk_w + (hd + 1) * LANES]
            vh = p[:, 2 * qk_w + hd * LANES:2 * qk_w + (hd + 1) * LANES]
            if rope:
                qh = rotate(qh, cos, sin)
                kh = rotate(kh, cos, sin)
            if emit_kv:
                newk_ref[seq, 0, hd, pl.ds(r0, rc), :] = kh
                newv_ref[seq, 0, hd, pl.ds(r0, rc), :] = vh
            qs = qh * (LANES // 2) ** -0.5
            head = slice(hd * LANES, (hd + 1) * LANES)
            q1s[pl.ds(r0, rc), head] = jnp.where(first_map, qs, 0.0).astype(BF16)
            q2s[pl.ds(r0, rc), head] = jnp.where(first_map, 0.0, qs).astype(BF16)
            ks[hd, pl.ds(n_cache + r0, rc), :] = kh.astype(BF16)
            vs[hd, pl.ds(n_cache + r0, rc), :] = vh.astype(BF16)
        fs[pl.ds(r0, rc), :] = p[:, 3 * qk_w:].astype(BF16)

    _loop(n_chunks, project)

    contract_last = (((1,), (1,)), ((), ()))

    def softmax(s):
        e = jnp.exp(s - jnp.max(s, axis=-1, keepdims=True))
        return e * (1.0 / jnp.sum(e, axis=-1, keepdims=True))

    def attend(c):
        r0 = pl.multiple_of(c * rc, rc)
        for hd in range(heads):
            kh = ks[hd]
            s1 = lax.dot_general(q1s[pl.ds(r0, rc), hd * LANES:(hd + 1) * LANES], kh,
                                 contract_last, preferred_element_type=F32)
            s2 = lax.dot_general(q2s[pl.ds(r0, rc), hd * LANES:(hd + 1) * LANES], kh,
                                 contract_last, preferred_element_type=F32)
            a = softmax(s1) - lam * softmax(s2)
            o = jnp.dot(a.astype(BF16), vs[hd], preferred_element_type=F32)
            o = o * _rsqrt_mean_sq(o) * subg_ref[...] * (1.0 - LAMBDA_INIT)
            mixs[pl.ds(r0, rc), hd * LANES:(hd + 1) * LANES] = o.astype(BF16)

    _loop(n_chunks, attend)

    def dft_channels(c):
        r0 = pl.multiple_of(c * rc, rc)
        for g in range(N_FGROUPS):
            a = jnp.dot(fs[pl.ds(r0, rc), g * LANES:(g + 1) * LANES], fcs_ref[...],
                        preferred_element_type=F32)
            stks[pl.ds(r0, rc), g * LANES:(g + 1) * LANES] = a[:, :LANES].astype(BF16)
            stks[pl.ds(pl.multiple_of(n + r0, rc), rc), g * LANES:(g + 1) * LANES] = (
                a[:, LANES:].astype(BF16))

    _loop(n_chunks, dft_channels)

    fscale = 1.0 / math.sqrt(n * LANES)

    def dft_positions(c):
        r0 = pl.multiple_of(c * rc, rc)
        y = jnp.dot(dftn_ref[pl.ds(r0, rc), :], stks[...], preferred_element_type=F32) * fscale
        mixs[pl.ds(r0, rc), qk_w:] = y.astype(BF16)

    _loop(n_chunks, dft_positions)

    klane = lax.broadcasted_iota(jnp.int32, (rc, LANES), 1)
    neg_inf = jnp.float32(-jnp.inf)

    def tail(c):
        r0 = pl.multiple_of(c * rc, rc)
        x = xs[pl.ds(r0, rc), :]
        mixed = jnp.dot(mixs[pl.ds(r0, rc), :], wout_ref[...], preferred_element_type=F32)
        x1 = x + gate1 * mixed
        x1_ref[pl.ds(row0 + r0, rc), :] = x1
        h2 = (x1 * _rsqrt_mean_sq(x1) * g2_ref[...]) * (1.0 + scale2) + shift2
        tpr = d // LANES
        for j in range(tpr):
            h2_ref[pl.ds((row0 + r0) * tpr + j, rc, stride=tpr), :] = (
                h2[:, j * LANES:(j + 1) * LANES])
        hi = h2.astype(BF16)
        lo = (h2 - hi.astype(F32)).astype(BF16)
        logits = (jnp.dot(hi, rwh_ref[...], preferred_element_type=F32)
                  + jnp.dot(lo, rwh_ref[...], preferred_element_type=F32)
                  + jnp.dot(hi, rwl_ref[...], preferred_element_type=F32))
        l = jnp.where(klane < n_experts, logits + rb_ref[...], neg_inf)
        lt = jnp.transpose(l)[:cnt_ref.shape[0]]
        eid = lax.broadcasted_iota(jnp.int32, lt.shape, 0).astype(F32)
        vals, ids = [], []
        for _ in range(TOP_K):
            m = jnp.max(lt, axis=0, keepdims=True)
            i = jnp.min(jnp.where(lt == m, eid, float(LANES)), axis=0, keepdims=True)
            vals.append(m)
            ids.append(i)
            lt = jnp.where(eid == i, neg_inf, lt)
        es = [jnp.exp(v - vals[0]) for v in vals]
        inv = 1.0 / functools.reduce(lambda a, b: a + b, es)
        krow = lax.broadcasted_iota(jnp.int32, (SUBLANES, rc), 0)
        idx_t = jnp.zeros((SUBLANES, rc), F32)
        wts_t = jnp.zeros((SUBLANES, rc), F32)
        for k in range(TOP_K):
            idx_t = jnp.where(krow == k, ids[k], idx_t)
            wts_t = jnp.where(krow == k, es[k] * inv, wts_t)
        pad = jnp.zeros((LANES - SUBLANES, rc), F32)
        idx_ref[pl.ds(row0 + r0, rc), :] = jnp.transpose(
            jnp.concatenate([idx_t, pad], axis=0)).astype(jnp.int32)
        wts_ref[pl.ds(row0 + r0, rc), :] = jnp.transpose(jnp.concatenate([wts_t, pad], axis=0))
        cnt_ref[...] += functools.reduce(lambda a, b: a + b,
                                         [jnp.where(eid == i, 1.0, 0.0) for i in ids])

    _loop(n_chunks, tail)


def _const_spec(shape):
    nd = len(shape)
    return pl.BlockSpec(shape, lambda b: (0,) * nd, pipeline_mode=pl.Buffered(1))


def _layer(x, mod, g1, win, lamv, subg, fcs, dftn, wout, g2, rwh, rwl, rb, *, n_experts,
           mod_row0, mod_row_step, h2_rows, h2_block0, seqs=1, zero_blocks=0, cache=None,
           h2_buf=None):
    n_seq, n, d = x.shape
    bsz = n_seq // seqs
    heads = win.shape[1] // (4 * LANES)
    rope = cache is not None
    n_cache = cache[0].shape[2] if rope else 0
    nk = n_cache + n
    live = lambda b: jnp.minimum(b, bsz - 1)
    cnt_shape = (-(-n_experts // SUBLANES) * SUBLANES, min(ROW_CHUNK, n))
    in_specs = [
        pl.BlockSpec((seqs, n, d), lambda b: (live(b), 0, 0)),
        pl.BlockSpec((1, 1, mod.shape[-1]),
                     lambda b: (mod_row0 + mod_row_step * live(b), 0, 0)),
        _const_spec(g1.shape), _const_spec(win.shape), _const_spec(lamv.shape),
        _const_spec(subg.shape), _const_spec(fcs.shape), _const_spec(dftn.shape),
        _const_spec(wout.shape), _const_spec(g2.shape), _const_spec(rwh.shape),
        _const_spec(rwl.shape), _const_spec(rb.shape),
    ]
    args = [x, mod, g1, win, lamv, subg, fcs, dftn, wout, g2, rwh, rwl, rb]
    out_specs = [
        pl.BlockSpec((seqs * n, d), lambda b: (live(b), 0)),
        pl.BlockSpec((seqs * n * (d // LANES), LANES), lambda b: (h2_block0 + b, 0)),
        pl.BlockSpec((seqs * n, LANES), lambda b: (live(b), 0)),
        pl.BlockSpec((seqs * n, LANES), lambda b: (live(b), 0)),
        pl.BlockSpec(cnt_shape, lambda b: (0, 0)),
    ]
    out_shape = [
        jax.ShapeDtypeStruct((n_seq * n, d), F32),
        jax.ShapeDtypeStruct((h2_rows * (d // LANES), LANES), F32),
        jax.ShapeDtypeStruct((n_seq * n, LANES), jnp.int32),
        jax.ShapeDtypeStruct((n_seq * n, LANES), F32),
        jax.ShapeDtypeStruct(cnt_shape, F32),
    ]
    aliases = {}
    if rope:
        ck, cv, cos, sin = cache
        in_specs += [
            pl.BlockSpec((1, heads, n_cache, LANES), lambda b: (live(b), 0, 0, 0)),
            pl.BlockSpec((1, heads, n_cache, LANES), lambda b: (live(b), 0, 0, 0)),
            _const_spec(cos.shape), _const_spec(sin.shape),
            pl.BlockSpec(memory_space=pl.ANY),
        ]
        args += [ck, cv, cos, sin, h2_buf]
        aliases = {len(args) - 1: 1}
    else:
        kv_spec = pl.BlockSpec((seqs, 1, heads, n, LANES), lambda b: (live(b), 0, 0, 0, 0))
        out_specs += [kv_spec, kv_spec]
        kv_shape = jax.ShapeDtypeStruct((n_seq, 1, heads, n, LANES), F32)
        out_shape += [kv_shape, kv_shape]
    scratch = [
        pltpu.VMEM((seqs, n, heads * LANES), BF16),
        pltpu.VMEM((seqs, n, heads * LANES), BF16),
        pltpu.VMEM((seqs, heads, nk, LANES), BF16),
        pltpu.VMEM((seqs, heads, nk, LANES), BF16),
        pltpu.VMEM((seqs, n, N_FGROUPS * LANES), BF16),
        pltpu.VMEM((seqs, 2 * n, N_FGROUPS * LANES), BF16),
        pltpu.VMEM((seqs, n, d), BF16),
    ]
    kern = functools.partial(_layer_kernel, n_live=bsz if zero_blocks else None,
                             n_inputs=len(args), n=n, seqs=seqs, n_cache=n_cache, heads=heads,
                             n_experts=n_experts, rope=rope, emit_kv=not rope)
    return pl.pallas_call(
        kern,
        grid=(bsz + zero_blocks,),
        in_specs=in_specs,
        out_specs=out_specs,
        out_shape=out_shape,
        scratch_shapes=scratch,
        input_output_aliases=aliases,
        compiler_params=pltpu.CompilerParams(dimension_semantics=("arbitrary",),
                                             vmem_limit_bytes=VMEM_LIMIT),
        name="layer_latent" if rope else "layer_context",
    )(*args)


def _moe_kernel(texp_ref, next_ref, nvalid_ref, tok_ref, tok_next_ref, tok_next2_ref, dst_ref,
                dst_prev_ref, h2_hbm, wgu_hbm, bgu_ref, wd_hbm, bd_ref, y_hbm, xbuf0, xbuf1, xbuf2,
                ybuf0, ybuf1, ybuf2, wgu_f32, wd_f32, wgu_bf, wd_bf, gsem, ssem, wsem, run_ref, *,
                tm):
    i = pl.program_id(0)
    nv = nvalid_ref[0]
    d_ff, d = wd_bf.shape
    tpr = d // LANES

    def weight_copies(e, s):
        return (pltpu.make_async_copy(wgu_hbm.at[e], wgu_f32.at[s], wsem.at[s, 0]),
                pltpu.make_async_copy(wd_hbm.at[e], wd_f32.at[s], wsem.at[s, 1]))

    def token_rows(t):
        return pl.ds(pl.multiple_of(t * tpr, tpr), tpr)

    xbuf, ybuf = (xbuf0, xbuf1, xbuf2), (ybuf0, ybuf1, ybuf2)

    def gather_row(idx_ref, slot, r):
        return pltpu.make_async_copy(h2_hbm.at[token_rows(idx_ref[0, 0, r]), :],
                                     xbuf[slot].at[pl.ds(r * tpr, tpr), :], gsem.at[slot])

    def scatter_row(idx_ref, slot, r):
        return pltpu.make_async_copy(ybuf[slot].at[pl.ds(r * tpr, tpr), :],
                                     y_hbm.at[token_rows(idx_ref[0, 0, r]), :], ssem.at[slot])

    def step(slot):
        ahead, behind = (slot + 1) % MOE_RING, (slot + 2) % MOE_RING
        for r in range(tm):
            gather_row(tok_ref, slot, r).wait()

        @pl.when(i >= 2)
        def _():
            for r in range(tm):
                scatter_row(dst_ref, slot, r).wait()

        for r in range(tm):
            gather_row(tok_next2_ref, behind, r).start()
        for r in range(tm):
            scatter_row(dst_prev_ref, behind, r).start()

        x = jnp.concatenate(
            [xbuf[slot][pl.ds(j, tm, stride=tpr), :].astype(BF16) for j in range(tpr)], axis=1)
        gu = jnp.dot(x, wgu_bf[...], preferred_element_type=F32) + bgu_ref[0]
        glu = jnp.minimum(gu[:, :d_ff], SWIGLU_LIMIT)
        lin = jnp.clip(gu[:, d_ff:], -SWIGLU_LIMIT, SWIGLU_LIMIT)
        act = glu * _sigmoid(SWIGLU_ALPHA * glu) * (lin + 1.0)
        y = jnp.dot(act.astype(BF16), wd_bf[...], preferred_element_type=F32) + bd_ref[0]
        for j in range(tpr):
            ybuf[slot][pl.ds(j, tm, stride=tpr), :] = y[:, j * LANES:(j + 1) * LANES]

        @pl.when(i == nv - 1)
        def _():
            for r in range(tm):
                scatter_row(dst_ref, slot, r).start()
            for r in range(tm):
                scatter_row(dst_ref, slot, r).wait()
            for r in range(tm):
                scatter_row(dst_prev_ref, behind, r).wait()

            @pl.when(i >= 1)
            def _():
                for r in range(tm):
                    scatter_row(dst_prev_ref, ahead, r).wait()

            for r in range(tm):
                gather_row(tok_next_ref, ahead, r).wait()
            for r in range(tm):
                gather_row(tok_next2_ref, behind, r).wait()

    @pl.when(i < nv)
    def _():
        @pl.when(i == 0)
        def _():
            run_ref[0] = 0
            for c in weight_copies(texp_ref[0], 0):
                c.start(priority=1)
            for r in range(tm):
                gather_row(tok_ref, 0, r).start()
            for r in range(tm):
                gather_row(tok_next_ref, 1, r).start()
            spare0 = y_hbm.shape[0] - 2 * tm * tpr
            for s in range(MOE_RING):
                ybuf[s][...] = jnp.zeros(ybuf[s].shape, F32)
            fills = [pltpu.make_async_copy(
                ybuf[s], y_hbm.at[pl.ds(spare0 + s * tm * tpr, tm * tpr), :], ssem.at[s])
                for s in range(2)]
            for f in fills:
                f.start()
            for f in fills:
                f.wait()

        @pl.when(jnp.logical_or(i == 0, texp_ref[i] != texp_ref[jnp.maximum(i - 1, 0)]))
        def _():
            run = run_ref[0]
            ws = run % 2
            for c in weight_copies(texp_ref[i], ws):
                c.wait()

            @pl.when(next_ref[i] >= 0)
            def _():
                for c in weight_copies(next_ref[i], 1 - ws):
                    c.start(priority=1)

            wgu_bf[...] = wgu_f32[ws].astype(BF16)
            wd_bf[...] = wd_f32[ws].astype(BF16)
            run_ref[0] = run + 1

        for s in range(MOE_RING):
            pl.when(i % MOE_RING == s)(functools.partial(step, s))


def _moe(h2, tile_expert, next_expert, n_valid, src_tok, dest, wgu, bgu, wd, bd):
    n_exp, d, two_f = wgu.shape
    tpr = d // LANES
    t_rows = h2.shape[0] // tpr
    d_ff = two_f // 2
    n_tiles = tile_expert.shape[0]
    tm = src_tok.shape[-1]
    last = n_tiles - 1
    smem_tile = functools.partial(pl.BlockSpec, (1, 1, tm), memory_space=pltpu.SMEM)
    grid_spec = pltpu.PrefetchScalarGridSpec(
        num_scalar_prefetch=3,
        grid=(n_tiles,),
        in_specs=[
            smem_tile(lambda i, te, nx, nv: (i, 0, 0)),
            smem_tile(lambda i, te, nx, nv: (jnp.minimum(i + 1, last), 0, 0)),
            smem_tile(lambda i, te, nx, nv: (jnp.minimum(i + 2, last), 0, 0)),
            smem_tile(lambda i, te, nx, nv: (i + 1, 0, 0)),
            smem_tile(lambda i, te, nx, nv: (i, 0, 0)),
            pl.BlockSpec(memory_space=pl.ANY),
            pl.BlockSpec(memory_space=pl.ANY),
            pl.BlockSpec((1, 1, two_f), lambda i, te, nx, nv: (te[i], 0, 0)),
            pl.BlockSpec(memory_space=pl.ANY),
            pl.BlockSpec((1, 1, d), lambda i, te, nx, nv: (te[i], 0, 0)),
        ],
        out_specs=pl.BlockSpec(memory_space=pl.ANY),
        scratch_shapes=[
            *([pltpu.VMEM((tm * tpr, LANES), F32)] * (2 * MOE_RING)),
            pltpu.VMEM((2, d, two_f), F32),
            pltpu.VMEM((2, d_ff, d), F32),
            pltpu.VMEM((d, two_f), BF16),
            pltpu.VMEM((d_ff, d), BF16),
            pltpu.SemaphoreType.DMA((MOE_RING,)),
            pltpu.SemaphoreType.DMA((MOE_RING,)),
            pltpu.SemaphoreType.DMA((2, 2)),
            pltpu.SMEM((1,), jnp.int32),
        ],
    )
    return pl.pallas_call(
        functools.partial(_moe_kernel, tm=tm),
        grid_spec=grid_spec,
        out_shape=jax.ShapeDtypeStruct(((TOP_K * t_rows + 2 * tm) * tpr, LANES), F32),
        compiler_params=pltpu.CompilerParams(dimension_semantics=("arbitrary",),
                                             vmem_limit_bytes=VMEM_LIMIT),
        name="routed_moe",
    )(tile_expert, next_expert, n_valid, src_tok, src_tok, src_tok, dest, dest, h2, wgu,
      bgu.reshape(n_exp, 1, two_f), wd, bd.reshape(n_exp, 1, d))


def _route(idx, counts, tm, n_tiles):
    t_rows, top_k = idx.shape
    n_exp = counts.shape[0]
    n_pairs = t_rows * top_k
    pad_bit = 16
    assert n_pairs <= 1 << pad_bit and tm <= 1 << pad_bit and n_tiles * tm == n_pairs + n_exp * tm
    pair_ids = np.arange(n_pairs, dtype=np.int32)
    real_keys = (idx.T.reshape(-1) << (pad_bit + 1)) | pair_ids
    pad_e = np.repeat(np.arange(n_exp, dtype=np.int32), tm)
    pad_j = np.tile(np.arange(tm, dtype=np.int32), n_exp)
    n_pad = (-counts) % tm
    unused = n_exp << (pad_bit + 1)
    pad_keys = jnp.where(pad_j < jnp.repeat(n_pad, tm),
                         (pad_e << (pad_bit + 1)) | (1 << pad_bit) | pad_j, unused)
    keys = lax.sort(jnp.concatenate([real_keys, pad_keys]), is_stable=False)
    is_real = jnp.logical_and((keys >> pad_bit) & 1 == 0, keys < unused)
    pair = keys & ((1 << pad_bit) - 1)
    slot = sum((pair >= k * t_rows).astype(jnp.int32) for k in range(1, top_k))
    pos = np.arange(n_tiles * tm, dtype=np.int32)
    spare = n_pairs + pos % (2 * tm)
    dest = jnp.concatenate([spare[:tm], jnp.where(is_real, pair, spare)])
    src_tok = jnp.where(is_real, pair - slot * t_rows, 0)
    n_valid = jnp.sum(counts + n_pad) // tm
    last_e = jnp.max(jnp.where(counts > 0, jnp.arange(n_exp, dtype=jnp.int32), 0))
    tile_expert = jnp.where(np.arange(n_tiles) < n_valid, keys[::tm] >> (pad_bit + 1), last_e)
    experts = jnp.arange(n_exp, dtype=jnp.int32)
    owner = jnp.where(counts > 0, experts, n_exp)
    following = jnp.concatenate([lax.cummin(owner, reverse=True)[1:],
                                 jnp.full((1,), n_exp, jnp.int32)])
    following = jnp.where(following >= n_exp, -1, following)
    next_expert = jnp.sum(jnp.where(tile_expert[:, None] == experts[None, :], following[None, :], 0),
                          axis=1)
    return (tile_expert.astype(jnp.int32), next_expert.astype(jnp.int32),
            n_valid.reshape(1).astype(jnp.int32), src_tok.reshape(n_tiles, 1, tm),
            dest.reshape(n_tiles + 1, 1, tm))


def _combine_kernel(x1_ref, y0_ref, y1_ref, y2_ref, y3_ref, wts_ref, gate_ref, fg_ref, o_ref):
    w = wts_ref[...]
    tc, d = x1_ref.shape
    tpr = d // LANES
    cols = []
    for j in range(tpr):
        acc = w[:, 0:1] * y0_ref[pl.ds(j, tc, stride=tpr), :]
        for k, y_ref in enumerate((y1_ref, y2_ref, y3_ref), start=1):
            acc = acc + w[:, k:k + 1] * y_ref[pl.ds(j, tc, stride=tpr), :]
        cols.append(acc)
    x = x1_ref[...] + gate_ref[0] * jnp.concatenate(cols, axis=1)
    o_ref[...] = x * _rsqrt_mean_sq(x) * fg_ref[...]


def _combine(x1, y, wts, gate2, final_g, *, t_all, row0, rows_per_gate):
    rows, d = x1.shape
    tc = COMBINE_TILE
    y_specs = [
        pl.BlockSpec((tc * (d // LANES), LANES), functools.partial(
            lambda i, k: ((k * t_all + row0) // tc + i, 0), k=k))
        for k in range(TOP_K)
    ]
    return pl.pallas_call(
        _combine_kernel,
        grid=(rows // tc,),
        in_specs=[pl.BlockSpec((tc, d), lambda i: (i, 0))] + y_specs + [
            pl.BlockSpec((tc, LANES), lambda i: (i, 0)),
            pl.BlockSpec((1, 1, d), lambda i: ((i * tc) // rows_per_gate, 0, 0)),
            pl.BlockSpec((1, d), lambda i: (0, 0)),
        ],
        out_specs=pl.BlockSpec((tc, d), lambda i: (i, 0)),
        out_shape=jax.ShapeDtypeStruct((rows, d), F32),
        compiler_params=pltpu.CompilerParams(dimension_semantics=("arbitrary",)),
        name="combine",
    )(x1, y, y, y, y, wts, gate2, final_g)


def _dft_tables(n):
    def angles(m):
        k = np.arange(m, dtype=np.int64)
        return (2.0 * np.pi / m) * ((k[:, None] * k[None, :]) % m)
    an = angles(n)
    ac = angles(LANES)
    dftn = np.concatenate([np.cos(an), -np.sin(an)], axis=1).astype(np.float32)
    fcs = np.concatenate([np.cos(ac), np.sin(ac)], axis=1).astype(np.float32)
    return jnp.asarray(dftn).astype(BF16), jnp.asarray(fcs).astype(BF16)


def _rope_tables(n, qk_dim):
    quarter = qk_dim // 4
    tok = np.arange(n)
    pos = np.stack([tok // GRID_W, tok % GRID_W], axis=-1).astype(np.float64)
    freqs = ROPE_THETA ** (-np.arange(quarter, dtype=np.float64) / quarter)
    ang = (pos[:, :, None] * freqs).reshape(n, 2 * quarter)
    cos, sin = np.cos(ang), np.sin(ang)
    row_c, col_c = cos[:, :quarter], cos[:, quarter:]
    row_s, col_s = sin[:, :quarter], sin[:, quarter:]
    cos_map = np.concatenate([row_c, row_c, col_c, col_c], axis=-1)
    sin_map = np.concatenate([-row_s, row_s, -col_s, col_s], axis=-1)
    reps = LANES // qk_dim
    return (np.tile(cos_map, (1, reps)).astype(np.float32),
            np.tile(sin_map, (1, reps)).astype(np.float32))


def kernel(x_prompt, x_sample, cache_k, cache_v, c, c_ctx, w_mod, b_mod, norm1_g, w_in, lambda_q1,
           lambda_k1, lambda_q2, lambda_k2, subln_g, w_out, norm2_g, router_w, router_b, w_gate_up,
           b_gate_up, w_down, b_down, final_g):
    bsz, seq, d = x_prompt.shape
    dec_b, dec_seq, _ = x_sample.shape
    heads, past, qk_dim = cache_k.shape[2], cache_k.shape[3], cache_k.shape[5]
    n_exp = router_w.shape[-1]
    t_ctx, t_den = bsz * seq, dec_b * dec_seq
    t_all = t_ctx + t_den
    assert bsz % CTX_SEQS == 0 and t_den % (CTX_SEQS * seq) == 0 and t_ctx % dec_seq == 0
    assert 2 * qk_dim == LANES and dec_seq % GRID_W == 0
    assert (t_all * TOP_K) % MOE_TILE == 0 and t_all % COMBINE_TILE == 0

    cvec = jnp.concatenate([c_ctx[None, :], c, jnp.zeros((8 - 1 - dec_b, d), F32)], axis=0)
    mod = _modulation(cvec, w_mod[0], b_mod[0])[:, None, :]

    win = w_in[0].astype(BF16)
    wout = w_out[0].astype(BF16)
    rw = jnp.pad(router_w[0], ((0, 0), (0, LANES - n_exp)))
    rwh = rw.astype(BF16)
    rwl = (rw - rwh.astype(F32)).astype(BF16)
    rb = jnp.pad(router_b[0], (0, LANES - n_exp)).reshape(1, LANES)
    lamv = jnp.stack([lambda_q1[0], lambda_k1[0], lambda_q2[0], lambda_k2[0]], axis=0)
    g1 = norm1_g[0].reshape(1, d)
    g2 = norm2_g[0].reshape(1, d)
    subg = subln_g[0].reshape(1, LANES)
    dft_ctx, fcs = _dft_tables(seq)
    dft_den, _ = _dft_tables(dec_seq)
    cos, sin = _rope_tables(dec_seq, qk_dim)

    shared = (g1, win, lamv, subg, fcs)
    tail = (wout, g2, rwh, rwl, rb)
    x1_ctx, h2_all, idx_ctx, wts_ctx, cnt_ctx, new_k, new_v = _layer(
        x_prompt, mod, *shared, dft_ctx, *tail, n_experts=n_exp, mod_row0=0, mod_row_step=0,
        h2_rows=t_all, h2_block0=0, seqs=CTX_SEQS, zero_blocks=t_den // (CTX_SEQS * seq))
    ck = cache_k[:, 0].reshape(dec_b, heads, past, LANES)
    cv = cache_v[:, 0]
    x1_den, h2_all, idx_den, wts_den, cnt_den = _layer(
        x_sample, mod, *shared, dft_den, *tail, n_experts=n_exp, mod_row0=1, mod_row_step=1,
        h2_rows=t_all, h2_block0=t_ctx // dec_seq, cache=(ck, cv, cos, sin), h2_buf=h2_all)

    idx = jnp.concatenate([idx_ctx[:, :TOP_K], idx_den[:, :TOP_K]], axis=0)
    n_tiles = (t_all * TOP_K) // MOE_TILE + n_exp
    counts = jnp.sum(cnt_ctx + cnt_den, axis=1)[:n_exp].astype(jnp.int32)
    tile_expert, next_expert, n_valid, src_tok, dest = _route(idx, counts, MOE_TILE, n_tiles)
    y = _moe(h2_all, tile_expert, next_expert, n_valid, src_tok, dest, w_gate_up[0], b_gate_up[0],
             w_down[0], b_down[0])

    gate2 = mod[:, :, 5 * d:]
    fg = final_g.reshape(1, d)
    y_prompt = _combine(x1_ctx, y, wts_ctx, gate2[0:1], fg, t_all=t_all, row0=0,
                        rows_per_gate=t_ctx)
    y_sample = _combine(x1_den, y, wts_den, gate2[1:1 + dec_b], fg, t_all=t_all, row0=t_ctx,
                        rows_per_gate=dec_seq)
    return (y_prompt.reshape(bsz, seq, d), y_sample.reshape(dec_b, dec_seq, d),
            new_k.reshape(bsz, 1, heads, seq, 2, qk_dim), new_v)
```

```python
import functools
import math

import numpy as np
import jax
import jax.numpy as jnp
from jax import lax
from jax.experimental import pallas as pl
from jax.experimental.pallas import tpu as pltpu

F32 = jnp.float32
BF16 = jnp.bfloat16

GRID_W = 64
N_FGROUPS = 4
TOP_K = 4
SWIGLU_LIMIT = 7.0
SWIGLU_ALPHA = 1.702
ROPE_THETA = 10000.0
NORM_EPS = 1e-6
LAMBDA_INIT = 0.8 - 0.6 * math.exp(-0.3 * 0)

LANES = 128
SUBLANES = 8
ROW_CHUNK = 256
CTX_SEQS = 2
MOE_TILE = 256
MOE_RING = 3
COMBINE_TILE = 512
VMEM_LIMIT = 56 * 1024 * 1024


def _rsqrt_mean_sq(x):
    return lax.rsqrt(jnp.mean(x * x, axis=-1, keepdims=True) + NORM_EPS)


def _sigmoid(z):
    return 1.0 / (1.0 + jnp.exp(-z))


def _mod_kernel(c_ref, w_ref, b_ref, o_ref):
    c = c_ref[...]
    s = c * _sigmoid(c)
    o_ref[...] = jnp.dot(s.astype(BF16), w_ref[...].astype(BF16),
                         preferred_element_type=F32) + b_ref[...]


def _modulation(cvec, w_mod, b_mod):
    rows, d = cvec.shape
    n_out = w_mod.shape[1]
    return pl.pallas_call(
        _mod_kernel,
        grid=(n_out // d,),
        in_specs=[
            pl.BlockSpec((rows, d), lambda j: (0, 0)),
            pl.BlockSpec((d, d), lambda j: (0, j)),
            pl.BlockSpec((1, d), lambda j: (0, j)),
        ],
        out_specs=pl.BlockSpec((rows, d), lambda j: (0, j)),
        out_shape=jax.ShapeDtypeStruct((rows, n_out), F32),
        name="modulation",
    )(cvec, w_mod, b_mod.reshape(1, n_out))


def _loop(n, body):
    if n == 1:
        body(0)
    else:
        def step(i, carry):
            body(i)
            return carry
        lax.fori_loop(0, n, step, 0)


def _layer_kernel(*refs, n_live, n_inputs, **static):
    if n_live is None:
        _layer_body(*refs, **static)
        return
    b = pl.program_id(0)
    h2_ref = refs[n_inputs + 1]
    pl.when(b < n_live)(functools.partial(_layer_body, *refs, **static))

    @pl.when(b >= n_live)
    def _():
        h2_ref[...] = jnp.zeros(h2_ref.shape, F32)


def _layer_body(*refs, n, seqs, n_cache, heads, n_experts, rope, emit_kv):
    it = iter(refs)
    x_ref = next(it); mod_ref = next(it); g1_ref = next(it); win_ref = next(it)
    lamv_ref = next(it); subg_ref = next(it); fcs_ref = next(it); dftn_ref = next(it)
    wout_ref = next(it); g2_ref = next(it); rwh_ref = next(it); rwl_ref = next(it); rb_ref = next(it)
    cos_ref = sin_ref = newk_ref = newv_ref = None
    if rope:
        ck_ref = next(it); cv_ref = next(it); cos_ref = next(it); sin_ref = next(it)
        next(it)
    x1_ref = next(it); h2_ref = next(it); idx_ref = next(it); wts_ref = next(it); cnt_ref = next(it)
    if emit_kv:
        newk_ref = next(it); newv_ref = next(it)
    q1_scr = next(it); q2_scr = next(it); kall = next(it); vall = next(it)
    f_scr = next(it); stk = next(it); mix = next(it)

    d = x_ref.shape[-1]
    qk_w = heads * LANES

    @pl.when(pl.program_id(0) == 0)
    def _():
        cnt_ref[...] = jnp.zeros(cnt_ref.shape, F32)
    rc = min(ROW_CHUNK, n)
    n_chunks = n // rc

    def mod_row(j):
        return mod_ref[0, :, j * d:(j + 1) * d]

    shift1, scale1, gate1 = mod_row(0), mod_row(1), mod_row(2)
    shift2, scale2, gate2 = mod_row(3), mod_row(4), mod_row(5)
    del gate2

    lv = lamv_ref[...]
    lam = (jnp.exp(jnp.sum(lv[0:1] * lv[1:2], axis=-1, keepdims=True))
           - jnp.exp(jnp.sum(lv[2:3] * lv[3:4], axis=-1, keepdims=True)) + LAMBDA_INIT)

    if rope:
        for hd in range(heads):
            kall[0, hd, 0:n_cache, :] = ck_ref[0, hd].astype(BF16)
            vall[0, hd, 0:n_cache, :] = cv_ref[0, hd].astype(BF16)

    lane = lax.broadcasted_iota(jnp.int32, (rc, LANES), 1)
    first_map = lane < (LANES // 2)

    first_of_pair = jnp.bitwise_and(lane, 31) < 16

    def rotate(t, cos, sin):
        partner = jnp.where(first_of_pair, pltpu.roll(t, LANES - 16, 1), pltpu.roll(t, 16, 1))
        return t * cos + partner * sin

    shared = dict(
        x_ref=x_ref, g1_ref=g1_ref, win_ref=win_ref, subg_ref=subg_ref, fcs_ref=fcs_ref,
        dftn_ref=dftn_ref, wout_ref=wout_ref, g2_ref=g2_ref, rwh_ref=rwh_ref, rwl_ref=rwl_ref,
        rb_ref=rb_ref, cos_ref=cos_ref, sin_ref=sin_ref, x1_ref=x1_ref, h2_ref=h2_ref,
        idx_ref=idx_ref, wts_ref=wts_ref, cnt_ref=cnt_ref, newk_ref=newk_ref, newv_ref=newv_ref,
        q1_scr=q1_scr, q2_scr=q2_scr, kall=kall, vall=vall, f_scr=f_scr, stk=stk, mix=mix,
        n=n, n_cache=n_cache, heads=heads, n_experts=n_experts, rope=rope, emit_kv=emit_kv, d=d,
        qk_w=qk_w, rc=rc, n_chunks=n_chunks, shift1=shift1, scale1=scale1, gate1=gate1,
        shift2=shift2, scale2=scale2, lam=lam, first_map=first_map, rotate=rotate)
    for seq in range(seqs):
        _layer_sequence(seq, **shared)


def _layer_sequence(seq, *, x_ref, g1_ref, win_ref, subg_ref, fcs_ref, dftn_ref, wout_ref, g2_ref,
                    rwh_ref, rwl_ref, rb_ref, cos_ref, sin_ref, x1_ref, h2_ref, idx_ref, wts_ref,
                    cnt_ref, newk_ref, newv_ref, q1_scr, q2_scr, kall, vall, f_scr, stk, mix, n,
                    n_cache, heads, n_experts, rope, emit_kv, d, qk_w, rc, n_chunks, shift1, scale1,
                    gate1, shift2, scale2, lam, first_map, rotate):
    xs = x_ref.at[seq]
    q1s, q2s, ks, vs = q1_scr.at[seq], q2_scr.at[seq], kall.at[seq], vall.at[seq]
    fs, stks, mixs = f_scr.at[seq], stk.at[seq], mix.at[seq]
    row0 = seq * n

    def project(c):
        r0 = pl.multiple_of(c * rc, rc)
        x = xs[pl.ds(r0, rc), :]
        h = (x * _rsqrt_mean_sq(x) * g1_ref[...]) * (1.0 + scale1) + shift1
        p = jnp.dot(h.astype(BF16), win_ref[...], preferred_element_type=F32)
        if rope:
            cos = cos_ref[pl.ds(r0, rc), :]
            sin = sin_ref[pl.ds(r0, rc), :]
        for hd in range(heads):
            qh = p[:, hd * LANES:(hd + 1) * LANES]
            kh = p[:, qk_w + hd * LANES:qk_w + (hd + 1) * LANES]
            vh = p[:, 2 * qk_w + hd * LANES:2 * qk_w + (hd + 1) * LANES]
            if rope:
                qh = rotate(qh, cos, sin)
                kh = rotate(kh, cos, sin)
            if emit_kv:
                newk_ref[seq, 0, hd, pl.ds(r0, rc), :] = kh
                newv_ref[seq, 0, hd, pl.ds(r0, rc), :] = vh
            qs = qh * (LANES // 2) ** -0.5
            head = slice(hd * LANES, (hd + 1) * LANES)
            q1s[pl.ds(r0, rc), head] = jnp.where(first_map, qs, 0.0).astype(BF16)
            q2s[pl.ds(r0, rc), head] = jnp.where(first_map, 0.0, qs).astype(BF16)
            ks[hd, pl.ds(n_cache + r0, rc), :] = kh.astype(BF16)
            vs[hd, pl.ds(n_cache + r0, rc), :] = vh.astype(BF16)
        fs[pl.ds(r0, rc), :] = p[:, 3 * qk_w:].astype(BF16)

    _loop(n_chunks, project)

    contract_last = (((1,), (1,)), ((), ()))

    def softmax(s):
        e = jnp.exp(s - jnp.max(s, axis=-1, keepdims=True))
        return e * (1.0 / jnp.sum(e, axis=-1, keepdims=True))

    def attend(c):
        r0 = pl.multiple_of(c * rc, rc)
        for hd in range(heads):
            kh = ks[hd]
            s1 = lax.dot_general(q1s[pl.ds(r0, rc), hd * LANES:(hd + 1) * LANES], kh,
                                 contract_last, preferred_element_type=F32)
            s2 = lax.dot_general(q2s[pl.ds(r0, rc), hd * LANES:(hd + 1) * LANES], kh,
                                 contract_last, preferred_element_type=F32)
            a = softmax(s1) - lam * softmax(s2)
            o = jnp.dot(a.astype(BF16), vs[hd], preferred_element_type=F32)
            o = o * _rsqrt_mean_sq(o) * subg_ref[...] * (1.0 - LAMBDA_INIT)
            mixs[pl.ds(r0, rc), hd * LANES:(hd + 1) * LANES] = o.astype(BF16)

    _loop(n_chunks, attend)

    def dft_channels(c):
        r0 = pl.multiple_of(c * rc, rc)
        for g in range(N_FGROUPS):
            a = jnp.dot(fs[pl.ds(r0, rc), g * LANES:(g + 1) * LANES], fcs_ref[...],
                        preferred_element_type=F32)
            stks[pl.ds(r0, rc), g * LANES:(g + 1) * LANES] = a[:, :LANES].astype(BF16)
            stks[pl.ds(pl.multiple_of(n + r0, rc), rc), g * LANES:(g + 1) * LANES] = (
                a[:, LANES:].astype(BF16))

    _loop(n_chunks, dft_channels)

    fscale = 1.0 / math.sqrt(n * LANES)

    def dft_positions(c):
        r0 = pl.multiple_of(c * rc, rc)
        y = jnp.dot(dftn_ref[pl.ds(r0, rc), :], stks[...], preferred_element_type=F32) * fscale
        mixs[pl.ds(r0, rc), qk_w:] = y.astype(BF16)

    _loop(n_chunks, dft_positions)

    klane = lax.broadcasted_iota(jnp.int32, (rc, LANES), 1)
    neg_inf = jnp.float32(-jnp.inf)

    def tail(c):
        r0 = pl.multiple_of(c * rc, rc)
        x = xs[pl.ds(r0, rc), :]
        mixed = jnp.dot(mixs[pl.ds(r0, rc), :], wout_ref[...], preferred_element_type=F32)
        x1 = x + gate1 * mixed
        x1_ref[pl.ds(row0 + r0, rc), :] = x1
        h2 = (x1 * _rsqrt_mean_sq(x1) * g2_ref[...]) * (1.0 + scale2) + shift2
        tpr = d // LANES
        for j in range(tpr):
            h2_ref[pl.ds((row0 + r0) * tpr + j, rc, stride=tpr), :] = (
                h2[:, j * LANES:(j + 1) * LANES])
        hi = h2.astype(BF16)
        lo = (h2 - hi.astype(F32)).astype(BF16)
        logits = (jnp.dot(hi, rwh_ref[...], preferred_element_type=F32)
                  + jnp.dot(lo, rwh_ref[...], preferred_element_type=F32)
                  + jnp.dot(hi, rwl_ref[...], preferred_element_type=F32))
        l = jnp.where(klane < n_experts, logits + rb_ref[...], neg_inf)
        lt = jnp.transpose(l)[:cnt_ref.shape[0]]
        eid = lax.broadcasted_iota(jnp.int32, lt.shape, 0).astype(F32)
        vals, ids = [], []
        for _ in range(TOP_K):
            m = jnp.max(lt, axis=0, keepdims=True)
            i = jnp.min(jnp.where(lt == m, eid, float(LANES)), axis=0, keepdims=True)
            vals.append(m)
            ids.append(i)
            lt = jnp.where(eid == i, neg_inf, lt)
        es = [jnp.exp(v - vals[0]) for v in vals]
        inv = 1.0 / functools.reduce(lambda a, b: a + b, es)
        krow = lax.broadcasted_iota(jnp.int32, (SUBLANES, rc), 0)
        idx_t = jnp.zeros((SUBLANES, rc), F32)
        wts_t = jnp.zeros((SUBLANES, rc), F32)
        for k in range(TOP_K):
            idx_t = jnp.where(krow == k, ids[k], idx_t)
            wts_t = jnp.where(krow == k, es[k] * inv, wts_t)
        pad = jnp.zeros((LANES - SUBLANES, rc), F32)
        idx_ref[pl.ds(row0 + r0, rc), :] = jnp.transpose(
            jnp.concatenate([idx_t, pad], axis=0)).astype(jnp.int32)
        wts_ref[pl.ds(row0 + r0, rc), :] = jnp.transpose(jnp.concatenate([wts_t, pad], axis=0))
        cnt_ref[...] += functools.reduce(lambda a, b: a + b,
                                         [jnp.where(eid == i, 1.0, 0.0) for i in ids])

    _loop(n_chunks, tail)


def _const_spec(shape):
    nd = len(shape)
    return pl.BlockSpec(shape, lambda b: (0,) * nd, pipeline_mode=pl.Buffered(1))


def _layer(x, mod, g1, win, lamv, subg, fcs, dftn, wout, g2, rwh, rwl, rb, *, n_experts,
           mod_row0, mod_row_step, h2_rows, h2_block0, seqs=1, zero_blocks=0, cache=None,
           h2_buf=None):
    n_seq, n, d = x.shape
    bsz = n_seq // seqs
    heads = win.shape[1] // (4 * LANES)
    rope = cache is not None
    n_cache = cache[0].shape[2] if rope else 0
    nk = n_cache + n
    live = lambda b: jnp.minimum(b, bsz - 1)
    cnt_shape = (-(-n_experts // SUBLANES) * SUBLANES, min(ROW_CHUNK, n))
    in_specs = [
        pl.BlockSpec((seqs, n, d), lambda b: (live(b), 0, 0)),
        pl.BlockSpec((1, 1, mod.shape[-1]),
                     lambda b: (mod_row0 + mod_row_step * live(b), 0, 0)),
        _const_spec(g1.shape), _const_spec(win.shape), _const_spec(lamv.shape),
        _const_spec(subg.shape), _const_spec(fcs.shape), _const_spec(dftn.shape),
        _const_spec(wout.shape), _const_spec(g2.shape), _const_spec(rwh.shape),
        _const_spec(rwl.shape), _const_spec(rb.shape),
    ]
    args = [x, mod, g1, win, lamv, subg, fcs, dftn, wout, g2, rwh, rwl, rb]
    out_specs = [
        pl.BlockSpec((seqs * n, d), lambda b: (live(b), 0)),
        pl.BlockSpec((seqs * n * (d // LANES), LANES), lambda b: (h2_block0 + b, 0)),
        pl.BlockSpec((seqs * n, LANES), lambda b: (live(b), 0)),
        pl.BlockSpec((seqs * n, LANES), lambda b: (live(b), 0)),
        pl.BlockSpec(cnt_shape, lambda b: (0, 0)),
    ]
    out_shape = [
        jax.ShapeDtypeStruct((n_seq * n, d), F32),
        jax.ShapeDtypeStruct((h2_rows * (d // LANES), LANES), F32),
        jax.ShapeDtypeStruct((n_seq * n, LANES), jnp.int32),
        jax.ShapeDtypeStruct((n_seq * n, LANES), F32),
        jax.ShapeDtypeStruct(cnt_shape, F32),
    ]
    aliases = {}
    if rope:
        ck, cv, cos, sin = cache
        in_specs += [
            pl.BlockSpec((1, heads, n_cache, LANES), lambda b: (live(b), 0, 0, 0)),
            pl.BlockSpec((1, heads, n_cache, LANES), lambda b: (live(b), 0, 0, 0)),
            _const_spec(cos.shape), _const_spec(sin.shape),
            pl.BlockSpec(memory_space=pl.ANY),
        ]
        args += [ck, cv, cos, sin, h2_buf]
        aliases = {len(args) - 1: 1}
    else:
        kv_spec = pl.BlockSpec((seqs, 1, heads, n, LANES), lambda b: (live(b), 0, 0, 0, 0))
        out_specs += [kv_spec, kv_spec]
        kv_shape = jax.ShapeDtypeStruct((n_seq, 1, heads, n, LANES), F32)
        out_shape += [kv_shape, kv_shape]
    scratch = [
        pltpu.VMEM((seqs, n, heads * LANES), BF16),
        pltpu.VMEM((seqs, n, heads * LANES), BF16),
        pltpu.VMEM((seqs, heads, nk, LANES), BF16),
        pltpu.VMEM((seqs, heads, nk, LANES), BF16),
        pltpu.VMEM((seqs, n, N_FGROUPS * LANES), BF16),
        pltpu.VMEM((seqs, 2 * n, N_FGROUPS * LANES), BF16),
        pltpu.VMEM((seqs, n, d), BF16),
    ]
    kern = functools.partial(_layer_kernel, n_live=bsz if zero_blocks else None,
                             n_inputs=len(args), n=n, seqs=seqs, n_cache=n_cache, heads=heads,
                             n_experts=n_experts, rope=rope, emit_kv=not rope)
    return pl.pallas_call(
        kern,
        grid=(bsz + zero_blocks,),
        in_specs=in_specs,
        out_specs=out_specs,
        out_shape=out_shape,
        scratch_shapes=scratch,
        input_output_aliases=aliases,
        compiler_params=pltpu.CompilerParams(dimension_semantics=("arbitrary",),
                                             vmem_limit_bytes=VMEM_LIMIT),
        name="layer_latent" if rope else "layer_context",
    )(*args)


def _moe_kernel(texp_ref, next_ref, nvalid_ref, tok_ref, tok_next_ref, tok_next2_ref, dst_ref,
                dst_prev_ref, h2_hbm, wgu_hbm, bgu_ref, wd_hbm, bd_ref, y_hbm, xbuf0, xbuf1, xbuf2,
                ybuf0, ybuf1, ybuf2, wgu_f32, wd_f32, wgu_bf, wd_bf, gsem, ssem, wsem, run_ref, *,
                tm):
    i = pl.program_id(0)
    nv = nvalid_ref[0]
    d_ff, d = wd_bf.shape
    tpr = d // LANES

    def weight_copies(e, s):
        return (pltpu.make_async_copy(wgu_hbm.at[e], wgu_f32.at[s], wsem.at[s, 0]),
                pltpu.make_async_copy(wd_hbm.at[e], wd_f32.at[s], wsem.at[s, 1]))

    def token_rows(t):
        return pl.ds(pl.multiple_of(t * tpr, tpr), tpr)

    xbuf, ybuf = (xbuf0, xbuf1, xbuf2), (ybuf0, ybuf1, ybuf2)

    def gather_row(idx_ref, slot, r):
        return pltpu.make_async_copy(h2_hbm.at[token_rows(idx_ref[0, 0, r]), :],
                                     xbuf[slot].at[pl.ds(r * tpr, tpr), :], gsem.at[slot])

    def scatter_row(idx_ref, slot, r):
        return pltpu.make_async_copy(ybuf[slot].at[pl.ds(r * tpr, tpr), :],
                                     y_hbm.at[token_rows(idx_ref[0, 0, r]), :], ssem.at[slot])

    def step(slot):
        ahead, behind = (slot + 1) % MOE_RING, (slot + 2) % MOE_RING
        for r in range(tm):
            gather_row(tok_ref, slot, r).wait()

        @pl.when(i >= 2)
        def _():
            for r in range(tm):
                scatter_row(dst_ref, slot, r).wait()

        for r in range(tm):
            gather_row(tok_next2_ref, behind, r).start()
        for r in range(tm):
            scatter_row(dst_prev_ref, behind, r).start()

        x = jnp.concatenate(
            [xbuf[slot][pl.ds(j, tm, stride=tpr), :].astype(BF16) for j in range(tpr)], axis=1)
        gu = jnp.dot(x, wgu_bf[...], preferred_element_type=F32) + bgu_ref[0]
        glu = jnp.minimum(gu[:, :d_ff], SWIGLU_LIMIT)
        lin = jnp.clip(gu[:, d_ff:], -SWIGLU_LIMIT, SWIGLU_LIMIT)
        act = glu * _sigmoid(SWIGLU_ALPHA * glu) * (lin + 1.0)
        y = jnp.dot(act.astype(BF16), wd_bf[...], preferred_element_type=F32) + bd_ref[0]
        for j in range(tpr):
            ybuf[slot][pl.ds(j, tm, stride=tpr), :] = y[:, j * LANES:(j + 1) * LANES]

        @pl.when(i == nv - 1)
        def _():
            for r in range(tm):
                scatter_row(dst_ref, slot, r).start()
            for r in range(tm):
                scatter_row(dst_ref, slot, r).wait()
            for r in range(tm):
                scatter_row(dst_prev_ref, behind, r).wait()

            @pl.when(i >= 1)
            def _():
                for r in range(tm):
                    scatter_row(dst_prev_ref, ahead, r).wait()

            for r in range(tm):
                gather_row(tok_next_ref, ahead, r).wait()
            for r in range(tm):
                gather_row(tok_next2_ref, behind, r).wait()

    @pl.when(i < nv)
    def _():
        @pl.when(i == 0)
        def _():
            run_ref[0] = 0
            for c in weight_copies(texp_ref[0], 0):
                c.start(priority=1)
            for r in range(tm):
                gather_row(tok_ref, 0, r).start()
            for r in range(tm):
                gather_row(tok_next_ref, 1, r).start()
            spare0 = y_hbm.shape[0] - 2 * tm * tpr
            for s in range(MOE_RING):
                ybuf[s][...] = jnp.zeros(ybuf[s].shape, F32)
            fills = [pltpu.make_async_copy(
                ybuf[s], y_hbm.at[pl.ds(spare0 + s * tm * tpr, tm * tpr), :], ssem.at[s])
                for s in range(2)]
            for f in fills:
                f.start()
            for f in fills:
                f.wait()

        @pl.when(jnp.logical_or(i == 0, texp_ref[i] != texp_ref[jnp.maximum(i - 1, 0)]))
        def _():
            run = run_ref[0]
            ws = run % 2
            for c in weight_copies(texp_ref[i], ws):
                c.wait()

            @pl.when(next_ref[i] >= 0)
            def _():
                for c in weight_copies(next_ref[i], 1 - ws):
                    c.start(priority=1)

            wgu_bf[...] = wgu_f32[ws].astype(BF16)
            wd_bf[...] = wd_f32[ws].astype(BF16)
            run_ref[0] = run + 1

        for s in range(MOE_RING):
            pl.when(i % MOE_RING == s)(functools.partial(step, s))


def _moe(h2, tile_expert, next_expert, n_valid, src_tok, dest, wgu, bgu, wd, bd):
    n_exp, d, two_f = wgu.shape
    tpr = d // LANES
    t_rows = h2.shape[0] // tpr
    d_ff = two_f // 2
    n_tiles = tile_expert.shape[0]
    tm = src_tok.shape[-1]
    last = n_tiles - 1
    smem_tile = functools.partial(pl.BlockSpec, (1, 1, tm), memory_space=pltpu.SMEM)
    grid_spec = pltpu.PrefetchScalarGridSpec(
        num_scalar_prefetch=3,
        grid=(n_tiles,),
        in_specs=[
            smem_tile(lambda i, te, nx, nv: (i, 0, 0)),
            smem_tile(lambda i, te, nx, nv: (jnp.minimum(i + 1, last), 0, 0)),
            smem_tile(lambda i, te, nx, nv: (jnp.minimum(i + 2, last), 0, 0)),
            smem_tile(lambda i, te, nx, nv: (i + 1, 0, 0)),
            smem_tile(lambda i, te, nx, nv: (i, 0, 0)),
            pl.BlockSpec(memory_space=pl.ANY),
            pl.BlockSpec(memory_space=pl.ANY),
            pl.BlockSpec((1, 1, two_f), lambda i, te, nx, nv: (te[i], 0, 0)),
            pl.BlockSpec(memory_space=pl.ANY),
            pl.BlockSpec((1, 1, d), lambda i, te, nx, nv: (te[i], 0, 0)),
        ],
        out_specs=pl.BlockSpec(memory_space=pl.ANY),
        scratch_shapes=[
            *([pltpu.VMEM((tm * tpr, LANES), F32)] * (2 * MOE_RING)),
            pltpu.VMEM((2, d, two_f), F32),
            pltpu.VMEM((2, d_ff, d), F32),
            pltpu.VMEM((d, two_f), BF16),
            pltpu.VMEM((d_ff, d), BF16),
            pltpu.SemaphoreType.DMA((MOE_RING,)),
            pltpu.SemaphoreType.DMA((MOE_RING,)),
            pltpu.SemaphoreType.DMA((2, 2)),
            pltpu.SMEM((1,), jnp.int32),
        ],
    )
    return pl.pallas_call(
        functools.partial(_moe_kernel, tm=tm),
        grid_spec=grid_spec,
        out_shape=jax.ShapeDtypeStruct(((TOP_K * t_rows + 2 * tm) * tpr, LANES), F32),
        compiler_params=pltpu.CompilerParams(dimension_semantics=("arbitrary",),
                                             vmem_limit_bytes=VMEM_LIMIT),
        name="routed_moe",
    )(tile_expert, next_expert, n_valid, src_tok, src_tok, src_tok, dest, dest, h2, wgu,
      bgu.reshape(n_exp, 1, two_f), wd, bd.reshape(n_exp, 1, d))


def _route(idx, counts, tm, n_tiles):
    t_rows, top_k = idx.shape
    n_exp = counts.shape[0]
    n_pairs = t_rows * top_k
    pad_bit = 16
    assert n_pairs <= 1 << pad_bit and tm <= 1 << pad_bit and n_tiles * tm == n_pairs + n_exp * tm
    pair_ids = np.arange(n_pairs, dtype=np.int32)
    real_keys = (idx.T.reshape(-1) << (pad_bit + 1)) | pair_ids
    pad_e = np.repeat(np.arange(n_exp, dtype=np.int32), tm)
    pad_j = np.tile(np.arange(tm, dtype=np.int32), n_exp)
    n_pad = (-counts) % tm
    unused = n_exp << (pad_bit + 1)
    pad_keys = jnp.where(pad_j < jnp.repeat(n_pad, tm),
                         (pad_e << (pad_bit + 1)) | (1 << pad_bit) | pad_j, unused)
    keys = lax.sort(jnp.concatenate([real_keys, pad_keys]), is_stable=False)
    is_real = jnp.logical_and((keys >> pad_bit) & 1 == 0, keys < unused)
    pair = keys & ((1 << pad_bit) - 1)
    slot = sum((pair >= k * t_rows).astype(jnp.int32) for k in range(1, top_k))
    pos = np.arange(n_tiles * tm, dtype=np.int32)
    spare = n_pairs + pos % (2 * tm)
    dest = jnp.concatenate([spare[:tm], jnp.where(is_real, pair, spare)])
    src_tok = jnp.where(is_real, pair - slot * t_rows, 0)
    n_valid = jnp.sum(counts + n_pad) // tm
    last_e = jnp.max(jnp.where(counts > 0, jnp.arange(n_exp, dtype=jnp.int32), 0))
    tile_expert = jnp.where(np.arange(n_tiles) < n_valid, keys[::tm] >> (pad_bit + 1), last_e)
    experts = jnp.arange(n_exp, dtype=jnp.int32)
    owner = jnp.where(counts > 0, experts, n_exp)
    following = jnp.concatenate([lax.cummin(owner, reverse=True)[1:],
                                 jnp.full((1,), n_exp, jnp.int32)])
    following = jnp.where(following >= n_exp, -1, following)
    next_expert = jnp.sum(jnp.where(tile_expert[:, None] == experts[None, :], following[None, :], 0),
                          axis=1)
    return (tile_expert.astype(jnp.int32), next_expert.astype(jnp.int32),
            n_valid.reshape(1).astype(jnp.int32), src_tok.reshape(n_tiles, 1, tm),
            dest.reshape(n_tiles + 1, 1, tm))


def _combine_kernel(x1_ref, y0_ref, y1_ref, y2_ref, y3_ref, wts_ref, gate_ref, fg_ref, o_ref):
    w = wts_ref[...]
    tc, d = x1_ref.shape
    tpr = d // LANES
    cols = []
    for j in range(tpr):
        acc = w[:, 0:1] * y0_ref[pl.ds(j, tc, stride=tpr), :]
        for k, y_ref in enumerate((y1_ref, y2_ref, y3_ref), start=1):
            acc = acc + w[:, k:k + 1] * y_ref[pl.ds(j, tc, stride=tpr), :]
        cols.append(acc)
    x = x1_ref[...] + gate_ref[0] * jnp.concatenate(cols, axis=1)
    o_ref[...] = x * _rsqrt_mean_sq(x) * fg_ref[...]


def _combine(x1, y, wts, gate2, final_g, *, t_all, row0, rows_per_gate):
    rows, d = x1.shape
    tc = COMBINE_TILE
    y_specs = [
        pl.BlockSpec((tc * (d // LANES), LANES), functools.partial(
            lambda i, k: ((k * t_all + row0) // tc + i, 0), k=k))
        for k in range(TOP_K)
    ]
    return pl.pallas_call(
        _combine_kernel,
        grid=(rows // tc,),
        in_specs=[pl.BlockSpec((tc, d), lambda i: (i, 0))] + y_specs + [
            pl.BlockSpec((tc, LANES), lambda i: (i, 0)),
            pl.BlockSpec((1, 1, d), lambda i: ((i * tc) // rows_per_gate, 0, 0)),
            pl.BlockSpec((1, d), lambda i: (0, 0)),
        ],
        out_specs=pl.BlockSpec((tc, d), lambda i: (i, 0)),
        out_shape=jax.ShapeDtypeStruct((rows, d), F32),
        compiler_params=pltpu.CompilerParams(dimension_semantics=("arbitrary",)),
        name="combine",
    )(x1, y, y, y, y, wts, gate2, final_g)


def _dft_tables(n):
    def angles(m):
        k = np.arange(m, dtype=np.int64)
        return (2.0 * np.pi / m) * ((k[:, None] * k[None, :]) % m)
    an = angles(n)
    ac = angles(LANES)
    dftn = np.concatenate([np.cos(an), -np.sin(an)], axis=1).astype(np.float32)
    fcs = np.concatenate([np.cos(ac), np.sin(ac)], axis=1).astype(np.float32)
    return jnp.asarray(dftn).astype(BF16), jnp.asarray(fcs).astype(BF16)


def _rope_tables(n, qk_dim):
    quarter = qk_dim // 4
    tok = np.arange(n)
    pos = np.stack([tok // GRID_W, tok % GRID_W], axis=-1).astype(np.float64)
    freqs = ROPE_THETA ** (-np.arange(quarter, dtype=np.float64) / quarter)
    ang = (pos[:, :, None] * freqs).reshape(n, 2 * quarter)
    cos, sin = np.cos(ang), np.sin(ang)
    row_c, col_c = cos[:, :quarter], cos[:, quarter:]
    row_s, col_s = sin[:, :quarter], sin[:, quarter:]
    cos_map = np.concatenate([row_c, row_c, col_c, col_c], axis=-1)
    sin_map = np.concatenate([-row_s, row_s, -col_s, col_s], axis=-1)
    reps = LANES // qk_dim
    return (np.tile(cos_map, (1, reps)).astype(np.float32),
            np.tile(sin_map, (1, reps)).astype(np.float32))


def kernel(x_prompt, x_sample, cache_k, cache_v, c, c_ctx, w_mod, b_mod, norm1_g, w_in, lambda_q1,
           lambda_k1, lambda_q2, lambda_k2, subln_g, w_out, norm2_g, router_w, router_b, w_gate_up,
           b_gate_up, w_down, b_down, final_g):
    bsz, seq, d = x_prompt.shape
    dec_b, dec_seq, _ = x_sample.shape
    heads, past, qk_dim = cache_k.shape[2], cache_k.shape[3], cache_k.shape[5]
    n_exp = router_w.shape[-1]
    t_ctx, t_den = bsz * seq, dec_b * dec_seq
    t_all = t_ctx + t_den
    assert bsz % CTX_SEQS == 0 and t_den % (CTX_SEQS * seq) == 0 and t_ctx % dec_seq == 0
    assert 2 * qk_dim == LANES and dec_seq % GRID_W == 0
    assert (t_all * TOP_K) % MOE_TILE == 0 and t_all % COMBINE_TILE == 0

    cvec = jnp.concatenate([c_ctx[None, :], c, jnp.zeros((8 - 1 - dec_b, d), F32)], axis=0)
    mod = _modulation(cvec, w_mod[0], b_mod[0])[:, None, :]

    win = w_in[0].astype(BF16)
    wout = w_out[0].astype(BF16)
    rw = jnp.pad(router_w[0], ((0, 0), (0, LANES - n_exp)))
    rwh = rw.astype(BF16)
    rwl = (rw - rwh.astype(F32)).astype(BF16)
    rb = jnp.pad(router_b[0], (0, LANES - n_exp)).reshape(1, LANES)
    lamv = jnp.stack([lambda_q1[0], lambda_k1[0], lambda_q2[0], lambda_k2[0]], axis=0)
    g1 = norm1_g[0].reshape(1, d)
    g2 = norm2_g[0].reshape(1, d)
    subg = subln_g[0].reshape(1, LANES)
    dft_ctx, fcs = _dft_tables(seq)
    dft_den, _ = _dft_tables(dec_seq)
    cos, sin = _rope_tables(dec_seq, qk_dim)

    shared = (g1, win, lamv, subg, fcs)
    tail = (wout, g2, rwh, rwl, rb)
    x1_ctx, h2_all, idx_ctx, wts_ctx, cnt_ctx, new_k, new_v = _layer(
        x_prompt, mod, *shared, dft_ctx, *tail, n_experts=n_exp, mod_row0=0, mod_row_step=0,
        h2_rows=t_all, h2_block0=0, seqs=CTX_SEQS, zero_blocks=t_den // (CTX_SEQS * seq))
    ck = cache_k[:, 0].reshape(dec_b, heads, past, LANES)
    cv = cache_v[:, 0]
    x1_den, h2_all, idx_den, wts_den, cnt_den = _layer(
        x_sample, mod, *shared, dft_den, *tail, n_experts=n_exp, mod_row0=1, mod_row_step=1,
        h2_rows=t_all, h2_block0=t_ctx // dec_seq, cache=(ck, cv, cos, sin), h2_buf=h2_all)

    idx = jnp.concatenate([idx_ctx[:, :TOP_K], idx_den[:, :TOP_K]], axis=0)
    n_tiles = (t_all * TOP_K) // MOE_TILE + n_exp
    counts = jnp.sum(cnt_ctx + cnt_den, axis=1)[:n_exp].astype(jnp.int32)
    tile_expert, next_expert, n_valid, src_tok, dest = _route(idx, counts, MOE_TILE, n_tiles)
    y = _moe(h2_all, tile_expert, next_expert, n_valid, src_tok, dest, w_gate_up[0], b_gate_up[0],
             w_down[0], b_down[0])

    gate2 = mod[:, :, 5 * d:]
    fg = final_g.reshape(1, d)
    y_prompt = _combine(x1_ctx, y, wts_ctx, gate2[0:1], fg, t_all=t_all, row0=0,
                        rows_per_gate=t_ctx)
    y_sample = _combine(x1_den, y, wts_den, gate2[1:1 + dec_b], fg, t_all=t_all, row0=t_ctx,
                        rows_per_gate=dec_seq)
    return (y_prompt.reshape(bsz, seq, d), y_sample.reshape(dec_b, dec_seq, d),
            new_k.reshape(bsz, 1, heads, seq, 2, qk_dim), new_v)
```

```python
import functools
import math

import numpy as np
import jax
import jax.numpy as jnp
from jax import lax
from jax.experimental import pallas as pl
from jax.experimental.pallas import tpu as pltpu

F32 = jnp.float32
BF16 = jnp.bfloat16

GRID_W = 64
N_FGROUPS = 4
TOP_K = 4
SWIGLU_LIMIT = 7.0
SWIGLU_ALPHA = 1.702
ROPE_THETA = 10000.0
NORM_EPS = 1e-6
LAMBDA_INIT = 0.8 - 0.6 * math.exp(-0.3 * 0)

LANES = 128
SUBLANES = 8
ROW_CHUNK = 256
CTX_SEQS = 2
MOE_TILE = 256
MOE_RING = 3
COMBINE_TILE = 512
VMEM_LIMIT = 56 * 1024 * 1024


def _rsqrt_mean_sq(x):
    return lax.rsqrt(jnp.mean(x * x, axis=-1, keepdims=True) + NORM_EPS)


def _sigmoid(z):
    return 1.0 / (1.0 + jnp.exp(-z))


def _mod_kernel(c_ref, w_ref, b_ref, o_ref):
    c = c_ref[...]
    s = c * _sigmoid(c)
    o_ref[...] = jnp.dot(s.astype(BF16), w_ref[...].astype(BF16),
                         preferred_element_type=F32) + b_ref[...]


def _modulation(cvec, w_mod, b_mod):
    rows, d = cvec.shape
    n_out = w_mod.shape[1]
    return pl.pallas_call(
        _mod_kernel,
        grid=(n_out // d,),
        in_specs=[
            pl.BlockSpec((rows, d), lambda j: (0, 0)),
            pl.BlockSpec((d, d), lambda j: (0, j)),
            pl.BlockSpec((1, d), lambda j: (0, j)),
        ],
        out_specs=pl.BlockSpec((rows, d), lambda j: (0, j)),
        out_shape=jax.ShapeDtypeStruct((rows, n_out), F32),
        name="modulation",
    )(cvec, w_mod, b_mod.reshape(1, n_out))


def _loop(n, body):
    if n == 1:
        body(0)
    else:
        def step(i, carry):
            body(i)
            return carry
        lax.fori_loop(0, n, step, 0)


def _layer_kernel(*refs, n_live, n_inputs, **static):
    if n_live is None:
        _layer_body(*refs, **static)
        return
    b = pl.program_id(0)
    h2_ref = refs[n_inputs + 1]
    pl.when(b < n_live)(functools.partial(_layer_body, *refs, **static))

    @pl.when(b >= n_live)
    def _():
        h2_ref[...] = jnp.zeros(h2_ref.shape, F32)


def _layer_body(*refs, n, seqs, n_cache, heads, n_experts, rope, emit_kv):
    it = iter(refs)
    x_ref = next(it); mod_ref = next(it); g1_ref = next(it); win_ref = next(it)
    lamv_ref = next(it); subg_ref = next(it); fcs_ref = next(it); dftn_ref = next(it)
    wout_ref = next(it); g2_ref = next(it); rwh_ref = next(it); rwl_ref = next(it); rb_ref = next(it)
    cos_ref = sin_ref = newk_ref = newv_ref = None
    if rope:
        ck_ref = next(it); cv_ref = next(it); cos_ref = next(it); sin_ref = next(it)
        next(it)
    x1_ref = next(it); h2_ref = next(it); idx_ref = next(it); wts_ref = next(it); cnt_ref = next(it)
    if emit_kv:
        newk_ref = next(it); newv_ref = next(it)
    q1_scr = next(it); q2_scr = next(it); kall = next(it); vall = next(it)
    f_scr = next(it); stk = next(it); mix = next(it)

    d = x_ref.shape[-1]
    qk_w = heads * LANES

    @pl.when(pl.program_id(0) == 0)
    def _():
        cnt_ref[...] = jnp.zeros(cnt_ref.shape, F32)
    rc = min(ROW_CHUNK, n)
    n_chunks = n // rc

    def mod_row(j):
        return mod_ref[0, :, j * d:(j + 1) * d]

    shift1, scale1, gate1 = mod_row(0), mod_row(1), mod_row(2)
    shift2, scale2, gate2 = mod_row(3), mod_row(4), mod_row(5)
    del gate2

    lv = lamv_ref[...]
    lam = (jnp.exp(jnp.sum(lv[0:1] * lv[1:2], axis=-1, keepdims=True))
           - jnp.exp(jnp.sum(lv[2:3] * lv[3:4], axis=-1, keepdims=True)) + LAMBDA_INIT)

    if rope:
        for hd in range(heads):
            kall[0, hd, 0:n_cache, :] = ck_ref[0, hd].astype(BF16)
            vall[0, hd, 0:n_cache, :] = cv_ref[0, hd].astype(BF16)

    lane = lax.broadcasted_iota(jnp.int32, (rc, LANES), 1)
    first_map = lane < (LANES // 2)

    first_of_pair = jnp.bitwise_and(lane, 31) < 16

    def rotate(t, cos, sin):
        partner = jnp.where(first_of_pair, pltpu.roll(t, LANES - 16, 1), pltpu.roll(t, 16, 1))
        return t * cos + partner * sin

    shared = dict(
        x_ref=x_ref, g1_ref=g1_ref, win_ref=win_ref, subg_ref=subg_ref, fcs_ref=fcs_ref,
        dftn_ref=dftn_ref, wout_ref=wout_ref, g2_ref=g2_ref, rwh_ref=rwh_ref, rwl_ref=rwl_ref,
        rb_ref=rb_ref, cos_ref=cos_ref, sin_ref=sin_ref, x1_ref=x1_ref, h2_ref=h2_ref,
        idx_ref=idx_ref, wts_ref=wts_ref, cnt_ref=cnt_ref, newk_ref=newk_ref, newv_ref=newv_ref,
        q1_scr=q1_scr, q2_scr=q2_scr, kall=kall, vall=vall, f_scr=f_scr, stk=stk, mix=mix,
        n=n, n_cache=n_cache, heads=heads, n_experts=n_experts, rope=rope, emit_kv=emit_kv, d=d,
        qk_w=qk_w, rc=rc, n_chunks=n_chunks, shift1=shift1, scale1=scale1, gate1=gate1,
        shift2=shift2, scale2=scale2, lam=lam, first_map=first_map, rotate=rotate)
    for seq in range(seqs):
        _layer_sequence(seq, **shared)


def _layer_sequence(seq, *, x_ref, g1_ref, win_ref, subg_ref, fcs_ref, dftn_ref, wout_ref, g2_ref,
                    rwh_ref, rwl_ref, rb_ref, cos_ref, sin_ref, x1_ref, h2_ref, idx_ref, wts_ref,
                    cnt_ref, newk_ref, newv_ref, q1_scr, q2_scr, kall, vall, f_scr, stk, mix, n,
                    n_cache, heads, n_experts, rope, emit_kv, d, qk_w, rc, n_chunks, shift1, scale1,
                    gate1, shift2, scale2, lam, first_map, rotate):
    xs = x_ref.at[seq]
    q1s, q2s, ks, vs = q1_scr.at[seq], q2_scr.at[seq], kall.at[seq], vall.at[seq]
    fs, stks, mixs = f_scr.at[seq], stk.at[seq], mix.at[seq]
    row0 = seq * n

    def project(c):
        r0 = pl.multiple_of(c * rc, rc)
        x = xs[pl.ds(r0, rc), :]
        h = (x * _rsqrt_mean_sq(x) * g1_ref[...]) * (1.0 + scale1) + shift1
        p = jnp.dot(h.astype(BF16), win_ref[...], preferred_element_type=F32)
        if rope:
            cos = cos_ref[pl.ds(r0, rc), :]
            sin = sin_ref[pl.ds(r0, rc), :]
        for hd in range(heads):
            qh = p[:, hd * LANES:(hd + 1) * LANES]
            kh = p[:, qk_w + hd * LANES:qk_w + (hd + 1) * LANES]
            vh = p[:, 2 * qk_w + hd * LANES:2 * qk_w + (hd + 1) * LANES]
            if rope:
                qh = rotate(qh, cos, sin)
                kh = rotate(kh, cos, sin)
            if emit_kv:
                newk_ref[seq, 0, hd, pl.ds(r0, rc), :] = kh
                newv_ref[seq, 0, hd, pl.ds(r0, rc), :] = vh
            qs = qh * (LANES // 2) ** -0.5
            head = slice(hd * LANES, (hd + 1) * LANES)
            q1s[pl.ds(r0, rc), head] = jnp.where(first_map, qs, 0.0).astype(BF16)
            q2s[pl.ds(r0, rc), head] = jnp.where(first_map, 0.0, qs).astype(BF16)
            ks[hd, pl.ds(n_cache + r0, rc), :] = kh.astype(BF16)
            vs[hd, pl.ds(n_cache + r0, rc), :] = vh.astype(BF16)
        fs[pl.ds(r0, rc), :] = p[:, 3 * qk_w:].astype(BF16)

    _loop(n_chunks, project)

    contract_last = (((1,), (1,)), ((), ()))

    def softmax(s):
        e = jnp.exp(s - jnp.max(s, axis=-1, keepdims=True))
        return e * (1.0 / jnp.sum(e, axis=-1, keepdims=True))

    def attend(c):
        r0 = pl.multiple_of(c * rc, rc)
        for hd in range(heads):
            kh = ks[hd]
            s1 = lax.dot_general(q1s[pl.ds(r0, rc), hd * LANES:(hd + 1) * LANES], kh,
                                 contract_last, preferred_element_type=F32)
            s2 = lax.dot_general(q2s[pl.ds(r0, rc), hd * LANES:(hd + 1) * LANES], kh,
                                 contract_last, preferred_element_type=F32)
            a = softmax(s1) - lam * softmax(s2)
            o = jnp.dot(a.astype(BF16), vs[hd], preferred_element_type=F32)
            o = o * _rsqrt_mean_sq(o) * subg_ref[...] * (1.0 - LAMBDA_INIT)
            mixs[pl.ds(r0, rc), hd * LANES:(hd + 1) * LANES] = o.astype(BF16)

    _loop(n_chunks, attend)

    def dft_channels(c):
        r0 = pl.multiple_of(c * rc, rc)
        for g in range(N_FGROUPS):
            a = jnp.dot(fs[pl.ds(r0, rc), g * LANES:(g + 1) * LANES], fcs_ref[...],
                        preferred_element_type=F32)
            stks[pl.ds(r0, rc), g * LANES:(g + 1) * LANES] = a[:, :LANES].astype(BF16)
            stks[pl.ds(pl.multiple_of(n + r0, rc), rc), g * LANES:(g + 1) * LANES] = (
                a[:, LANES:].astype(BF16))

    _loop(n_chunks, dft_channels)

    fscale = 1.0 / math.sqrt(n * LANES)

    def dft_positions(c):
        r0 = pl.multiple_of(c * rc, rc)
        y = jnp.dot(dftn_ref[pl.ds(r0, rc), :], stks[...], preferred_element_type=F32) * fscale
        mixs[pl.ds(r0, rc), qk_w:] = y.astype(BF16)

    _loop(n_chunks, dft_positions)

    klane = lax.broadcasted_iota(jnp.int32, (rc, LANES), 1)
    neg_inf = jnp.float32(-jnp.inf)

    def tail(c):
        r0 = pl.multiple_of(c * rc, rc)
        x = xs[pl.ds(r0, rc), :]
        mixed = jnp.dot(mixs[pl.ds(r0, rc), :], wout_ref[...], preferred_element_type=F32)
        x1 = x + gate1 * mixed
        x1_ref[pl.ds(row0 + r0, rc), :] = x1
        h2 = (x1 * _rsqrt_mean_sq(x1) * g2_ref[...]) * (1.0 + scale2) + shift2
        tpr = d // LANES
        for j in range(tpr):
            h2_ref[pl.ds((row0 + r0) * tpr + j, rc, stride=tpr), :] = (
                h2[:, j * LANES:(j + 1) * LANES])
        hi = h2.astype(BF16)
        lo = (h2 - hi.astype(F32)).astype(BF16)
        hh = jnp.dot(hi, rwl_ref[...], preferred_element_type=F32)
        logits = (hh[:, :LANES] + jnp.dot(lo, rwh_ref[...], preferred_element_type=F32)
                  + hh[:, LANES:])
        l = jnp.where(klane < n_experts, logits + rb_ref[...], neg_inf)
        lt = jnp.transpose(l)[:cnt_ref.shape[0]]
        eid = lax.broadcasted_iota(jnp.int32, lt.shape, 0).astype(F32)
        vals, ids = [], []
        for _ in range(TOP_K):
            m = jnp.max(lt, axis=0, keepdims=True)
            i = jnp.min(jnp.where(lt == m, eid, float(LANES)), axis=0, keepdims=True)
            vals.append(m)
            ids.append(i)
            lt = jnp.where(eid == i, neg_inf, lt)
        es = [jnp.exp(v - vals[0]) for v in vals]
        inv = 1.0 / functools.reduce(lambda a, b: a + b, es)
        krow = lax.broadcasted_iota(jnp.int32, (SUBLANES, rc), 0)
        idx_t = jnp.zeros((SUBLANES, rc), F32)
        wts_t = jnp.zeros((SUBLANES, rc), F32)
        for k in range(TOP_K):
            idx_t = jnp.where(krow == k, ids[k], idx_t)
            wts_t = jnp.where(krow == k, es[k] * inv, wts_t)
        pad = jnp.zeros((LANES - SUBLANES, rc), F32)
        idx_ref[pl.ds(row0 + r0, rc), :] = jnp.transpose(
            jnp.concatenate([idx_t, pad], axis=0)).astype(jnp.int32)
        wts_ref[pl.ds(row0 + r0, rc), :] = jnp.transpose(jnp.concatenate([wts_t, pad], axis=0))
        cnt_ref[...] += functools.reduce(lambda a, b: a + b,
                                         [jnp.where(eid == i, 1.0, 0.0) for i in ids])

    _loop(n_chunks, tail)


def _const_spec(shape):
    nd = len(shape)
    return pl.BlockSpec(shape, lambda b: (0,) * nd, pipeline_mode=pl.Buffered(1))


def _layer(x, mod, g1, win, lamv, subg, fcs, dftn, wout, g2, rwh, rwl, rb, *, n_experts,
           mod_row0, mod_row_step, h2_rows, h2_block0, seqs=1, zero_blocks=0, cache=None,
           h2_buf=None):
    n_seq, n, d = x.shape
    bsz = n_seq // seqs
    heads = win.shape[1] // (4 * LANES)
    rope = cache is not None
    n_cache = cache[0].shape[2] if rope else 0
    nk = n_cache + n
    live = lambda b: jnp.minimum(b, bsz - 1)
    cnt_shape = (-(-n_experts // SUBLANES) * SUBLANES, min(ROW_CHUNK, n))
    in_specs = [
        pl.BlockSpec((seqs, n, d), lambda b: (live(b), 0, 0)),
        pl.BlockSpec((1, 1, mod.shape[-1]),
                     lambda b: (mod_row0 + mod_row_step * live(b), 0, 0)),
        _const_spec(g1.shape), _const_spec(win.shape), _const_spec(lamv.shape),
        _const_spec(subg.shape), _const_spec(fcs.shape), _const_spec(dftn.shape),
        _const_spec(wout.shape), _const_spec(g2.shape), _const_spec(rwh.shape),
        _const_spec(rwl.shape), _const_spec(rb.shape),
    ]
    args = [x, mod, g1, win, lamv, subg, fcs, dftn, wout, g2, rwh, rwl, rb]
    out_specs = [
        pl.BlockSpec((seqs * n, d), lambda b: (live(b), 0)),
        pl.BlockSpec((seqs * n * (d // LANES), LANES), lambda b: (h2_block0 + b, 0)),
        pl.BlockSpec((seqs * n, LANES), lambda b: (live(b), 0)),
        pl.BlockSpec((seqs * n, LANES), lambda b: (live(b), 0)),
        pl.BlockSpec(cnt_shape, lambda b: (0, 0)),
    ]
    out_shape = [
        jax.ShapeDtypeStruct((n_seq * n, d), F32),
        jax.ShapeDtypeStruct((h2_rows * (d // LANES), LANES), F32),
        jax.ShapeDtypeStruct((n_seq * n, LANES), jnp.int32),
        jax.ShapeDtypeStruct((n_seq * n, LANES), F32),
        jax.ShapeDtypeStruct(cnt_shape, F32),
    ]
    aliases = {}
    if rope:
        ck, cv, cos, sin = cache
        in_specs += [
            pl.BlockSpec((1, heads, n_cache, LANES), lambda b: (live(b), 0, 0, 0)),
            pl.BlockSpec((1, heads, n_cache, LANES), lambda b: (live(b), 0, 0, 0)),
            _const_spec(cos.shape), _const_spec(sin.shape),
            pl.BlockSpec(memory_space=pl.ANY),
        ]
        args += [ck, cv, cos, sin, h2_buf]
        aliases = {len(args) - 1: 1}
    else:
        kv_spec = pl.BlockSpec((seqs, 1, heads, n, LANES), lambda b: (live(b), 0, 0, 0, 0))
        out_specs += [kv_spec, kv_spec]
        kv_shape = jax.ShapeDtypeStruct((n_seq, 1, heads, n, LANES), F32)
        out_shape += [kv_shape, kv_shape]
    scratch = [
        pltpu.VMEM((seqs, n, heads * LANES), BF16),
        pltpu.VMEM((seqs, n, heads * LANES), BF16),
        pltpu.VMEM((seqs, heads, nk, LANES), BF16),
        pltpu.VMEM((seqs, heads, nk, LANES), BF16),
        pltpu.VMEM((seqs, n, N_FGROUPS * LANES), BF16),
        pltpu.VMEM((seqs, 2 * n, N_FGROUPS * LANES), BF16),
        pltpu.VMEM((seqs, n, d), BF16),
    ]
    kern = functools.partial(_layer_kernel, n_live=bsz if zero_blocks else None,
                             n_inputs=len(args), n=n, seqs=seqs, n_cache=n_cache, heads=heads,
                             n_experts=n_experts, rope=rope, emit_kv=not rope)
    return pl.pallas_call(
        kern,
        grid=(bsz + zero_blocks,),
        in_specs=in_specs,
        out_specs=out_specs,
        out_shape=out_shape,
        scratch_shapes=scratch,
        input_output_aliases=aliases,
        compiler_params=pltpu.CompilerParams(dimension_semantics=("arbitrary",),
                                             vmem_limit_bytes=VMEM_LIMIT),
        name="layer_latent" if rope else "layer_context",
    )(*args)


def _moe_kernel(texp_ref, next_ref, nvalid_ref, tok_ref, tok_next_ref, tok_next2_ref, dst_ref,
                dst_prev_ref, h2_hbm, wgu_hbm, bgu_ref, wd_hbm, bd_ref, y_hbm, xbuf0, xbuf1, xbuf2,
                ybuf0, ybuf1, ybuf2, wgu_f32, wd_f32, wgu_bf, wd_bf, gsem, ssem, wsem, run_ref, *,
                tm):
    i = pl.program_id(0)
    nv = nvalid_ref[0]
    d_ff, d = wd_bf.shape
    tpr = d // LANES

    def weight_copies(e, s):
        return (pltpu.make_async_copy(wgu_hbm.at[e], wgu_f32.at[s], wsem.at[s, 0]),
                pltpu.make_async_copy(wd_hbm.at[e], wd_f32.at[s], wsem.at[s, 1]))

    def token_rows(t):
        return pl.ds(pl.multiple_of(t * tpr, tpr), tpr)

    xbuf, ybuf = (xbuf0, xbuf1, xbuf2), (ybuf0, ybuf1, ybuf2)

    def gather_row(idx_ref, slot, r):
        return pltpu.make_async_copy(h2_hbm.at[token_rows(idx_ref[0, 0, r]), :],
                                     xbuf[slot].at[pl.ds(r * tpr, tpr), :], gsem.at[slot])

    def scatter_row(idx_ref, slot, r):
        return pltpu.make_async_copy(ybuf[slot].at[pl.ds(r * tpr, tpr), :],
                                     y_hbm.at[token_rows(idx_ref[0, 0, r]), :], ssem.at[slot])

    def step(slot):
        ahead, behind = (slot + 1) % MOE_RING, (slot + 2) % MOE_RING
        for r in range(tm):
            gather_row(tok_ref, slot, r).wait()

        @pl.when(i >= 2)
        def _():
            for r in range(tm):
                scatter_row(dst_ref, slot, r).wait()

        for r in range(tm):
            gather_row(tok_next2_ref, behind, r).start()
        for r in range(tm):
            scatter_row(dst_prev_ref, behind, r).start()

        x = jnp.concatenate(
            [xbuf[slot][pl.ds(j, tm, stride=tpr), :].astype(BF16) for j in range(tpr)], axis=1)
        gu = jnp.dot(x, wgu_bf[...], preferred_element_type=F32) + bgu_ref[0]
        glu = jnp.minimum(gu[:, :d_ff], SWIGLU_LIMIT)
        lin = jnp.clip(gu[:, d_ff:], -SWIGLU_LIMIT, SWIGLU_LIMIT)
        act = glu * _sigmoid(SWIGLU_ALPHA * glu) * (lin + 1.0)
        y = jnp.dot(act.astype(BF16), wd_bf[...], preferred_element_type=F32) + bd_ref[0]
        for j in range(tpr):
            ybuf[slot][pl.ds(j, tm, stride=tpr), :] = y[:, j * LANES:(j + 1) * LANES]

        @pl.when(i == nv - 1)
        def _():
            for r in range(tm):
                scatter_row(dst_ref, slot, r).start()
            for r in range(tm):
                scatter_row(dst_ref, slot, r).wait()
            for r in range(tm):
                scatter_row(dst_prev_ref, behind, r).wait()

            @pl.when(i >= 1)
            def _():
                for r in range(tm):
                    scatter_row(dst_prev_ref, ahead, r).wait()

            for r in range(tm):
                gather_row(tok_next_ref, ahead, r).wait()
            for r in range(tm):
                gather_row(tok_next2_ref, behind, r).wait()

    @pl.when(i < nv)
    def _():
        @pl.when(i == 0)
        def _():
            run_ref[0] = 0
            for c in weight_copies(texp_ref[0], 0):
                c.start(priority=1)
            for r in range(tm):
                gather_row(tok_ref, 0, r).start()
            for r in range(tm):
                gather_row(tok_next_ref, 1, r).start()
            spare0 = y_hbm.shape[0] - 2 * tm * tpr
            for s in range(MOE_RING):
                ybuf[s][...] = jnp.zeros(ybuf[s].shape, F32)
            fills = [pltpu.make_async_copy(
                ybuf[s], y_hbm.at[pl.ds(spare0 + s * tm * tpr, tm * tpr), :], ssem.at[s])
                for s in range(2)]
            for f in fills:
                f.start()
            for f in fills:
                f.wait()

        @pl.when(jnp.logical_or(i == 0, texp_ref[i] != texp_ref[jnp.maximum(i - 1, 0)]))
        def _():
            run = run_ref[0]
            ws = run % 2
            for c in weight_copies(texp_ref[i], ws):
                c.wait()

            @pl.when(next_ref[i] >= 0)
            def _():
                for c in weight_copies(next_ref[i], 1 - ws):
                    c.start(priority=1)

            wgu_bf[...] = wgu_f32[ws].astype(BF16)
            wd_bf[...] = wd_f32[ws].astype(BF16)
            run_ref[0] = run + 1

        for s in range(MOE_RING):
            pl.when(i % MOE_RING == s)(functools.partial(step, s))


def _moe(h2, tile_expert, next_expert, n_valid, src_tok, dest, wgu, bgu, wd, bd):
    n_exp, d, two_f = wgu.shape
    tpr = d // LANES
    t_rows = h2.shape[0] // tpr
    d_ff = two_f // 2
    n_tiles = tile_expert.shape[0]
    tm = src_tok.shape[-1]
    last = n_tiles - 1
    smem_tile = functools.partial(pl.BlockSpec, (1, 1, tm), memory_space=pltpu.SMEM)
    grid_spec = pltpu.PrefetchScalarGridSpec(
        num_scalar_prefetch=3,
        grid=(n_tiles,),
        in_specs=[
            smem_tile(lambda i, te, nx, nv: (i, 0, 0)),
            smem_tile(lambda i, te, nx, nv: (jnp.minimum(i + 1, last), 0, 0)),
            smem_tile(lambda i, te, nx, nv: (jnp.minimum(i + 2, last), 0, 0)),
            smem_tile(lambda i, te, nx, nv: (i + 1, 0, 0)),
            smem_tile(lambda i, te, nx, nv: (i, 0, 0)),
            pl.BlockSpec(memory_space=pl.ANY),
            pl.BlockSpec(memory_space=pl.ANY),
            pl.BlockSpec((1, 1, two_f), lambda i, te, nx, nv: (te[i], 0, 0)),
            pl.BlockSpec(memory_space=pl.ANY),
            pl.BlockSpec((1, 1, d), lambda i, te, nx, nv: (te[i], 0, 0)),
        ],
        out_specs=pl.BlockSpec(memory_space=pl.ANY),
        scratch_shapes=[
            *([pltpu.VMEM((tm * tpr, LANES), F32)] * (2 * MOE_RING)),
            pltpu.VMEM((2, d, two_f), F32),
            pltpu.VMEM((2, d_ff, d), F32),
            pltpu.VMEM((d, two_f), BF16),
            pltpu.VMEM((d_ff, d), BF16),
            pltpu.SemaphoreType.DMA((MOE_RING,)),
            pltpu.SemaphoreType.DMA((MOE_RING,)),
            pltpu.SemaphoreType.DMA((2, 2)),
            pltpu.SMEM((1,), jnp.int32),
        ],
    )
    return pl.pallas_call(
        functools.partial(_moe_kernel, tm=tm),
        grid_spec=grid_spec,
        out_shape=jax.ShapeDtypeStruct(((TOP_K * t_rows + 2 * tm) * tpr, LANES), F32),
        compiler_params=pltpu.CompilerParams(dimension_semantics=("arbitrary",),
                                             vmem_limit_bytes=VMEM_LIMIT),
        name="routed_moe",
    )(tile_expert, next_expert, n_valid, src_tok, src_tok, src_tok, dest, dest, h2, wgu,
      bgu.reshape(n_exp, 1, two_f), wd, bd.reshape(n_exp, 1, d))


def _route(idx, counts, tm, n_tiles):
    t_rows, top_k = idx.shape
    n_exp = counts.shape[0]
    n_pairs = t_rows * top_k
    pad_bit = 16
    assert n_pairs <= 1 << pad_bit and tm <= 1 << pad_bit and n_tiles * tm == n_pairs + n_exp * tm
    pair_ids = np.arange(n_pairs, dtype=np.int32)
    real_keys = (idx.T.reshape(-1) << (pad_bit + 1)) | pair_ids
    pad_e = np.repeat(np.arange(n_exp, dtype=np.int32), tm)
    pad_j = np.tile(np.arange(tm, dtype=np.int32), n_exp)
    n_pad = (-counts) % tm
    unused = n_exp << (pad_bit + 1)
    pad_keys = jnp.where(pad_j < jnp.repeat(n_pad, tm),
                         (pad_e << (pad_bit + 1)) | (1 << pad_bit) | pad_j, unused)
    keys = lax.sort(jnp.concatenate([real_keys, pad_keys]), is_stable=False)
    is_real = jnp.logical_and((keys >> pad_bit) & 1 == 0, keys < unused)
    pair = keys & ((1 << pad_bit) - 1)
    slot = sum((pair >= k * t_rows).astype(jnp.int32) for k in range(1, top_k))
    pos = np.arange(n_tiles * tm, dtype=np.int32)
    spare = n_pairs + pos % (2 * tm)
    dest = jnp.concatenate([spare[:tm], jnp.where(is_real, pair, spare)])
    src_tok = jnp.where(is_real, pair - slot * t_rows, 0)
    n_valid = jnp.sum(counts + n_pad) // tm
    last_e = jnp.max(jnp.where(counts > 0, jnp.arange(n_exp, dtype=jnp.int32), 0))
    tile_expert = jnp.where(np.arange(n_tiles) < n_valid, keys[::tm] >> (pad_bit + 1), last_e)
    experts = jnp.arange(n_exp, dtype=jnp.int32)
    owner = jnp.where(counts > 0, experts, n_exp)
    following = jnp.concatenate([lax.cummin(owner, reverse=True)[1:],
                                 jnp.full((1,), n_exp, jnp.int32)])
    following = jnp.where(following >= n_exp, -1, following)
    next_expert = jnp.sum(jnp.where(tile_expert[:, None] == experts[None, :], following[None, :], 0),
                          axis=1)
    return (tile_expert.astype(jnp.int32), next_expert.astype(jnp.int32),
            n_valid.reshape(1).astype(jnp.int32), src_tok.reshape(n_tiles, 1, tm),
            dest.reshape(n_tiles + 1, 1, tm))


def _combine_kernel(x1_ref, y0_ref, y1_ref, y2_ref, y3_ref, wts_ref, gate_ref, fg_ref, o_ref):
    w = wts_ref[...]
    tc, d = x1_ref.shape
    tpr = d // LANES
    cols = []
    for j in range(tpr):
        acc = w[:, 0:1] * y0_ref[pl.ds(j, tc, stride=tpr), :]
        for k, y_ref in enumerate((y1_ref, y2_ref, y3_ref), start=1):
            acc = acc + w[:, k:k + 1] * y_ref[pl.ds(j, tc, stride=tpr), :]
        cols.append(acc)
    x = x1_ref[...] + gate_ref[0] * jnp.concatenate(cols, axis=1)
    o_ref[...] = x * _rsqrt_mean_sq(x) * fg_ref[...]


def _combine(x1, y, wts, gate2, final_g, *, t_all, row0, rows_per_gate):
    rows, d = x1.shape
    tc = COMBINE_TILE
    y_specs = [
        pl.BlockSpec((tc * (d // LANES), LANES), functools.partial(
            lambda i, k: ((k * t_all + row0) // tc + i, 0), k=k))
        for k in range(TOP_K)
    ]
    return pl.pallas_call(
        _combine_kernel,
        grid=(rows // tc,),
        in_specs=[pl.BlockSpec((tc, d), lambda i: (i, 0))] + y_specs + [
            pl.BlockSpec((tc, LANES), lambda i: (i, 0)),
            pl.BlockSpec((1, 1, d), lambda i: ((i * tc) // rows_per_gate, 0, 0)),
            pl.BlockSpec((1, d), lambda i: (0, 0)),
        ],
        out_specs=pl.BlockSpec((tc, d), lambda i: (i, 0)),
        out_shape=jax.ShapeDtypeStruct((rows, d), F32),
        compiler_params=pltpu.CompilerParams(dimension_semantics=("arbitrary",)),
        name="combine",
    )(x1, y, y, y, y, wts, gate2, final_g)


def _dft_tables(n):
    def angles(m):
        k = np.arange(m, dtype=np.int64)
        return (2.0 * np.pi / m) * ((k[:, None] * k[None, :]) % m)
    an = angles(n)
    ac = angles(LANES)
    dftn = np.concatenate([np.cos(an), -np.sin(an)], axis=1).astype(np.float32)
    fcs = np.concatenate([np.cos(ac), np.sin(ac)], axis=1).astype(np.float32)
    return jnp.asarray(dftn).astype(BF16), jnp.asarray(fcs).astype(BF16)


def _rope_tables(n, qk_dim):
    quarter = qk_dim // 4
    tok = np.arange(n)
    pos = np.stack([tok // GRID_W, tok % GRID_W], axis=-1).astype(np.float64)
    freqs = ROPE_THETA ** (-np.arange(quarter, dtype=np.float64) / quarter)
    ang = (pos[:, :, None] * freqs).reshape(n, 2 * quarter)
    cos, sin = np.cos(ang), np.sin(ang)
    row_c, col_c = cos[:, :quarter], cos[:, quarter:]
    row_s, col_s = sin[:, :quarter], sin[:, quarter:]
    cos_map = np.concatenate([row_c, row_c, col_c, col_c], axis=-1)
    sin_map = np.concatenate([-row_s, row_s, -col_s, col_s], axis=-1)
    reps = LANES // qk_dim
    return (np.tile(cos_map, (1, reps)).astype(np.float32),
            np.tile(sin_map, (1, reps)).astype(np.float32))


def kernel(x_prompt, x_sample, cache_k, cache_v, c, c_ctx, w_mod, b_mod, norm1_g, w_in, lambda_q1,
           lambda_k1, lambda_q2, lambda_k2, subln_g, w_out, norm2_g, router_w, router_b, w_gate_up,
           b_gate_up, w_down, b_down, final_g):
    bsz, seq, d = x_prompt.shape
    dec_b, dec_seq, _ = x_sample.shape
    heads, past, qk_dim = cache_k.shape[2], cache_k.shape[3], cache_k.shape[5]
    n_exp = router_w.shape[-1]
    t_ctx, t_den = bsz * seq, dec_b * dec_seq
    t_all = t_ctx + t_den
    assert bsz % CTX_SEQS == 0 and t_den % (CTX_SEQS * seq) == 0 and t_ctx % dec_seq == 0
    assert 2 * qk_dim == LANES and dec_seq % GRID_W == 0
    assert (t_all * TOP_K) % MOE_TILE == 0 and t_all % COMBINE_TILE == 0

    cvec = jnp.concatenate([c_ctx[None, :], c, jnp.zeros((8 - 1 - dec_b, d), F32)], axis=0)
    mod = _modulation(cvec, w_mod[0], b_mod[0])[:, None, :]

    win = w_in[0].astype(BF16)
    wout = w_out[0].astype(BF16)
    rw = jnp.pad(router_w[0], ((0, 0), (0, LANES - n_exp)))
    rwh = rw.astype(BF16)
    rwl = jnp.concatenate([rwh, (rw - rwh.astype(F32)).astype(BF16)], axis=1)
    rb = jnp.pad(router_b[0], (0, LANES - n_exp)).reshape(1, LANES)
    lamv = jnp.stack([lambda_q1[0], lambda_k1[0], lambda_q2[0], lambda_k2[0]], axis=0)
    g1 = norm1_g[0].reshape(1, d)
    g2 = norm2_g[0].reshape(1, d)
    subg = subln_g[0].reshape(1, LANES)
    dft_ctx, fcs = _dft_tables(seq)
    dft_den, _ = _dft_tables(dec_seq)
    cos, sin = _rope_tables(dec_seq, qk_dim)

    shared = (g1, win, lamv, subg, fcs)
    tail = (wout, g2, rwh, rwl, rb)
    x1_ctx, h2_all, idx_ctx, wts_ctx, cnt_ctx, new_k, new_v = _layer(
        x_prompt, mod, *shared, dft_ctx, *tail, n_experts=n_exp, mod_row0=0, mod_row_step=0,
        h2_rows=t_all, h2_block0=0, seqs=CTX_SEQS, zero_blocks=t_den // (CTX_SEQS * seq))
    ck = cache_k[:, 0].reshape(dec_b, heads, past, LANES)
    cv = cache_v[:, 0]
    x1_den, h2_all, idx_den, wts_den, cnt_den = _layer(
        x_sample, mod, *shared, dft_den, *tail, n_experts=n_exp, mod_row0=1, mod_row_step=1,
        h2_rows=t_all, h2_block0=t_ctx // dec_seq, cache=(ck, cv, cos, sin), h2_buf=h2_all)

    idx = jnp.concatenate([idx_ctx[:, :TOP_K], idx_den[:, :TOP_K]], axis=0)
    n_tiles = (t_all * TOP_K) // MOE_TILE + n_exp
    counts = jnp.sum(cnt_ctx + cnt_den, axis=1)[:n_exp].astype(jnp.int32)
    tile_expert, next_expert, n_valid, src_tok, dest = _route(idx, counts, MOE_TILE, n_tiles)
    y = _moe(h2_all, tile_expert, next_expert, n_valid, src_tok, dest, w_gate_up[0], b_gate_up[0],
             w_down[0], b_down[0])

    gate2 = mod[:, :, 5 * d:]
    fg = final_g.reshape(1, d)
    y_prompt = _combine(x1_ctx, y, wts_ctx, gate2[0:1], fg, t_all=t_all, row0=0,
                        rows_per_gate=t_ctx)
    y_sample = _combine(x1_den, y, wts_den, gate2[1:1 + dec_b], fg, t_all=t_all, row0=t_ctx,
                        rows_per_gate=dec_seq)
    return (y_prompt.reshape(bsz, seq, d), y_sample.reshape(dec_b, dec_seq, d),
            new_k.reshape(bsz, 1, heads, seq, 2, qk_dim), new_v)
```
